```python
import math
import jax, jax.numpy as jnp
from jax import lax
import numpy as np

D_MODEL = 2048
BATCH = 2
SEQ = 4096
DEPTH = 1

M_HEADS = 4
M_DQK = 256
M_DV = 512
CONV_K = 4
R_HEADS = 8
R_DQK = 128
R_DV = 256
D_FF = 4 * D_MODEL
CHUNK = 128
ROPE_BASE = 10000.0
LN_EPS = 1e-5
DN_ALPHA = (2.0 * DEPTH) ** 0.25
DN_BETA = (8.0 * DEPTH) ** -0.25

M_QK_W = M_HEADS * M_DQK
M_V_W = M_HEADS * M_DV
R_QK_W = R_HEADS * R_DQK
R_V_W = R_HEADS * R_DV
SPLITS = (M_QK_W, M_QK_W, M_V_W, M_V_W, M_HEADS, M_HEADS,
          R_QK_W, R_QK_W, R_V_W, R_V_W, D_MODEL, D_MODEL)
D_IN = int(sum(SPLITS))
SPLIT_IDX = [int(c) for c in np.cumsum(SPLITS)[:-1]]

kernel_name = "hybrid_mlstm_retention_deepnorm"


def layer_norm(x, g, b):
    xf = x.astype(jnp.float32)
    mu = jnp.mean(xf, axis=-1, keepdims=True)
    var = jnp.mean(jnp.square(xf - mu), axis=-1, keepdims=True)
    y = (xf - mu) * lax.rsqrt(var + LN_EPS)
    return (y * g.astype(jnp.float32) + b.astype(jnp.float32)).astype(x.dtype)


def head_norm(h, g):
    mu = jnp.mean(h, axis=-1, keepdims=True)
    var = jnp.mean(jnp.square(h - mu), axis=-1, keepdims=True)
    return (h - mu) * lax.rsqrt(var + LN_EPS) * g.astype(jnp.float32)


def causal_dwconv(x, w, b):
    T = x.shape[1]
    xp = jnp.pad(x, ((0, 0), (CONV_K - 1, 0), (0, 0)))
    y = b
    for k in range(CONV_K):
        y = y + xp[:, k:k + T] * w[k]
    return y


def rotary(x, pos):
    half = x.shape[-1] // 2
    inv_freq = ROPE_BASE ** (-jnp.arange(half, dtype=jnp.float32) / half)
    ang = pos.astype(jnp.float32)[..., None] * inv_freq
    cos = jnp.cos(ang)[:, :, None, :]
    sin = jnp.sin(ang)[:, :, None, :]
    x1, x2 = x[..., :half], x[..., half:]
    return jnp.concatenate([x1 * cos - x2 * sin, x1 * sin + x2 * cos], axis=-1)


def to_chunks(x):
    B, T, H, d = x.shape
    return x.reshape(B, T // CHUNK, CHUNK, H, d).transpose(1, 0, 3, 2, 4)


def gate_chunks(g):
    B, T, H = g.shape
    return g.reshape(B, T // CHUNK, CHUNK, H).transpose(1, 0, 3, 2)


def from_chunks(y):
    NC, B, H, L, d = y.shape
    return y.transpose(1, 0, 3, 2, 4).reshape(B, NC * L, H, d)


def mlstm_chunkwise(q, k, v, log_i, log_f):
    B, T, H, dqk = q.shape
    dv = v.shape[-1]
    q = q * (dqk ** -0.5)
    causal = jnp.tril(jnp.ones((CHUNK, CHUNK), dtype=bool))

    def step(carry, xs):
        C, n, m = carry
        q_, k_, v_, li, lf = xs
        b = jnp.cumsum(lf, axis=-1)
        d_log = jnp.where(causal, b[..., :, None] - b[..., None, :] + li[..., None, :], -jnp.inf)
        m_inter = b + m[..., None]
        m_t = jnp.maximum(m_inter, jnp.max(d_log, axis=-1))
        w_intra = jnp.exp(d_log - m_t[..., None])
        w_inter = jnp.exp(m_inter - m_t)
        s = jnp.einsum('bhtd,bhsd->bhts', q_, k_) * w_intra
        num = (jnp.einsum('bhts,bhsv->bhtv', s, v_)
               + w_inter[..., None] * jnp.einsum('bhtd,bhdv->bhtv', q_, C))
        den = jnp.sum(s, axis=-1) + w_inter * jnp.einsum('bhtd,bhd->bht', q_, n)
        h = num / jnp.maximum(jnp.abs(den), jnp.exp(-m_t))[..., None]
        g = b[..., -1]
        w_log = g[..., None] - b + li
        m_new = jnp.maximum(g + m, jnp.max(w_log, axis=-1))
        w_s = jnp.exp(w_log - m_new[..., None])
        decay = jnp.exp(g + m - m_new)
        C = decay[..., None, None] * C + jnp.einsum('bhs,bhsd,bhsv->bhdv', w_s, k_, v_)
        n = decay[..., None] * n + jnp.einsum('bhs,bhsd->bhd', w_s, k_)
        return (C, n, m_new), h

    init = (jnp.zeros((B, H, dqk, dv), jnp.float32),
            jnp.zeros((B, H, dqk), jnp.float32),
            jnp.zeros((B, H), jnp.float32))
    xs = (to_chunks(q), to_chunks(k), to_chunks(v), gate_chunks(log_i), gate_chunks(log_f))
    _, h = lax.scan(step, init, xs)
    return from_chunks(h)


def retention_chunkwise(q, k, v, log_gamma):
    B, T, H, dk = q.shape
    dv = v.shape[-1]
    q = q * (dk ** -0.5)
    idx = jnp.arange(CHUNK, dtype=jnp.float32)
    rel = idx[:, None] - idx[None, :]
    causal = rel >= 0
    decay_intra = jnp.where(causal, jnp.exp(jnp.maximum(rel, 0.0)[None] * log_gamma[:, None, None]), 0.0)
    xi = jnp.exp((idx + 1.0)[None, :] * log_gamma[:, None])
    zeta = jnp.exp((CHUNK - 1.0 - idx)[None, :] * log_gamma[:, None])
    g_chunk = jnp.exp(CHUNK * log_gamma)

    def step(R, xs):
        q_, k_, v_ = xs
        inner = jnp.einsum('bhts,bhsv->bhtv', jnp.einsum('bhtd,bhsd->bhts', q_, k_) * decay_intra, v_)
        cross = jnp.einsum('bhtd,bhdv->bhtv', q_, R) * xi[..., None]
        R = g_chunk[:, None, None] * R + jnp.einsum('bhsd,bhsv->bhdv', k_ * zeta[..., None], v_)
        return R, inner + cross

    R0 = jnp.zeros((B, H, dk, dv), jnp.float32)
    _, o = lax.scan(step, R0, (to_chunks(q), to_chunks(k), to_chunks(v)))
    return from_chunks(o)


def hybrid_mixer(x, positions, w_in, b_in, m_conv_w, m_conv_b, m_norm_g, r_norm_g,
                 w_branch_m, w_branch_r, w_out, b_out):
    B, T, _ = x.shape
    f32 = jnp.float32
    u = jnp.einsum('btd,de->bte', x, w_in) + b_in
    mq, mk, mv, mo, mi, mf, rq, rk, rv, rg, ga, gb = jnp.split(u, SPLIT_IDX, axis=-1)

    qk = jax.nn.silu(causal_dwconv(jnp.concatenate([mq, mk], axis=-1), m_conv_w, m_conv_b))
    mq, mk = qk[..., :M_QK_W], qk[..., M_QK_W:]
    hm = mlstm_chunkwise(mq.astype(f32).reshape(B, T, M_HEADS, M_DQK),
                         mk.astype(f32).reshape(B, T, M_HEADS, M_DQK),
                         mv.astype(f32).reshape(B, T, M_HEADS, M_DV),
                         mi.astype(f32),
                         jax.nn.log_sigmoid(mf.astype(f32)))
    hm = head_norm(hm, m_norm_g) * jax.nn.sigmoid(mo.astype(f32)).reshape(B, T, M_HEADS, M_DV)
    ya = jnp.einsum('bte,ed->btd', hm.reshape(B, T, M_V_W).astype(x.dtype), w_branch_m)

    log_gamma = jnp.log1p(-(2.0 ** (-5.0 - jnp.arange(R_HEADS, dtype=f32))))
    rq4 = rotary(rq.astype(f32).reshape(B, T, R_HEADS, R_DQK), positions)
    rk4 = rotary(rk.astype(f32).reshape(B, T, R_HEADS, R_DQK), positions)
    hr = retention_chunkwise(rq4, rk4, rv.astype(f32).reshape(B, T, R_HEADS, R_DV), log_gamma)
    hr = head_norm(hr, r_norm_g) * jax.nn.silu(rg.astype(f32)).reshape(B, T, R_HEADS, R_DV)
    yb = jnp.einsum('bte,ed->btd', hr.reshape(B, T, R_V_W).astype(x.dtype), w_branch_r)

    merged = jax.nn.sigmoid(ga) * ya + jax.nn.sigmoid(gb) * yb
    return jnp.einsum('btd,de->bte', merged, w_out) + b_out


def sq_relu_mlp(x, w1, b1, w2, b2):
    h = jnp.square(jax.nn.relu(jnp.einsum('btd,df->btf', x, w1) + b1))
    return jnp.einsum('btf,fd->btd', h, w2) + b2


def setup_inputs(seed: int = 0) -> dict:
    key = jax.random.key(seed)
    ks = jax.random.split(key, 24)
    f32 = jnp.float32
    nrm = lambda k, shape, s: (jax.random.normal(k, shape, f32) * s)

    x = jax.random.normal(ks[0], (BATCH, SEQ, D_MODEL), f32)
    positions = jnp.broadcast_to(jnp.arange(SEQ, dtype=jnp.int32), (BATCH, SEQ))

    col_scale = np.ones((D_IN,), np.float32)
    offs = [0] + SPLIT_IDX + [D_IN]
    for j in (2, 8):
        col_scale[offs[j]:offs[j + 1]] = DN_BETA
    w_in = nrm(ks[1], (DEPTH, D_MODEL, D_IN), D_MODEL ** -0.5) * jnp.asarray(col_scale)
    b_in = nrm(ks[2], (DEPTH, D_IN), 0.02)
    f_lo = offs[5]
    f_bias = jnp.linspace(3.0, 6.0, M_HEADS, dtype=f32) + nrm(ks[3], (DEPTH, M_HEADS), 0.1)
    b_in = b_in.at[:, f_lo:f_lo + M_HEADS].set(f_bias)

    m_conv_w = nrm(ks[4], (DEPTH, CONV_K, 2 * M_QK_W), CONV_K ** -0.5)
    m_conv_b = nrm(ks[5], (DEPTH, 2 * M_QK_W), 0.02)
    m_norm_g = 1.0 + nrm(ks[6], (DEPTH, M_HEADS, M_DV), 0.02)
    r_norm_g = 1.0 + nrm(ks[7], (DEPTH, R_HEADS, R_DV), 0.02)
    w_branch_m = nrm(ks[8], (DEPTH, M_V_W, D_MODEL), DN_BETA * M_V_W ** -0.5)
    w_branch_r = nrm(ks[9], (DEPTH, R_V_W, D_MODEL), DN_BETA * R_V_W ** -0.5)
    w_out = nrm(ks[10], (DEPTH, D_MODEL, D_MODEL), DN_BETA * D_MODEL ** -0.5)
    b_out = nrm(ks[11], (DEPTH, D_MODEL), 0.02)
    ln1_g = 1.0 + nrm(ks[12], (DEPTH, D_MODEL), 0.02)
    ln1_b = nrm(ks[13], (DEPTH, D_MODEL), 0.02)
    w_ff1 = nrm(ks[14], (DEPTH, D_MODEL, D_FF), DN_BETA * D_MODEL ** -0.5)
    b_ff1 = nrm(ks[15], (DEPTH, D_FF), 0.02)
    w_ff2 = nrm(ks[16], (DEPTH, D_FF, D_MODEL), DN_BETA * D_FF ** -0.5)
    b_ff2 = nrm(ks[17], (DEPTH, D_MODEL), 0.02)
    ln2_g = 1.0 + nrm(ks[18], (DEPTH, D_MODEL), 0.02)
    ln2_b = nrm(ks[19], (DEPTH, D_MODEL), 0.02)
    return {"x": x, "positions": positions, "w_in": w_in, "b_in": b_in,
            "m_conv_w": m_conv_w, "m_conv_b": m_conv_b, "m_norm_g": m_norm_g, "r_norm_g": r_norm_g,
            "w_branch_m": w_branch_m, "w_branch_r": w_branch_r, "w_out": w_out, "b_out": b_out,
            "ln1_g": ln1_g, "ln1_b": ln1_b, "w_ff1": w_ff1, "b_ff1": b_ff1,
            "w_ff2": w_ff2, "b_ff2": b_ff2, "ln2_g": ln2_g, "ln2_b": ln2_b}


def reference(x, positions, w_in, b_in, m_conv_w, m_conv_b, m_norm_g, r_norm_g,
              w_branch_m, w_branch_r, w_out, b_out, ln1_g, ln1_b,
              w_ff1, b_ff1, w_ff2, b_ff2, ln2_g, ln2_b):
    h = x
    for l in range(DEPTH):
        y = hybrid_mixer(h, positions, w_in[l], b_in[l], m_conv_w[l], m_conv_b[l],
                         m_norm_g[l], r_norm_g[l], w_branch_m[l], w_branch_r[l], w_out[l], b_out[l])
        h = layer_norm(DN_ALPHA * h + y, ln1_g[l], ln1_b[l])
        f = sq_relu_mlp(h, w_ff1[l], b_ff1[l], w_ff2[l], b_ff2[l])
        h = layer_norm(DN_ALPHA * h + f, ln2_g[l], ln2_b[l])
    return h
```

```python
import functools
import math

import jax
import jax.numpy as jnp
from jax import lax
from jax.experimental import pallas as pl
from jax.experimental.pallas import tpu as pltpu

M_HEADS, M_DQK, M_DV = 4, 256, 512
R_HEADS, R_DQK, R_DV = 8, 128, 256
CONV_K = 4
CHUNK = 128
ROPE_BASE = 10000.0
LN_EPS = 1e-5
M_QK_W = M_HEADS * M_DQK
M_V_W = M_HEADS * M_DV
R_QK_W = R_HEADS * R_DQK
R_V_W = R_HEADS * R_DV

LANES = 128
SUBLANES = 8
GATE_PAD = LANES
MIB = 1024 * 1024

F32 = jnp.float32
BF16 = jnp.bfloat16


def _sigmoid(x):
    return 1.0 / (1.0 + jnp.exp(-x))


def _log_sigmoid(x):
    return jnp.minimum(x, 0.0) - jnp.log1p(jnp.exp(-jnp.abs(x)))


def _dot(a, b):
    return jnp.dot(a, b, preferred_element_type=F32)


def _dot_nt(a, b):
    return lax.dot_general(a, b, (((1,), (1,)), ((), ())), preferred_element_type=F32)


def _layer_norm(z, g, b):
    mu = jnp.mean(z, axis=-1, keepdims=True)
    d = z - mu
    var = jnp.mean(d * d, axis=-1, keepdims=True)
    return d * lax.rsqrt(var + LN_EPS) * g + b


def _head_norm(h, g):
    mu = jnp.mean(h, axis=-1, keepdims=True)
    d = h - mu
    var = jnp.mean(d * d, axis=-1, keepdims=True)
    return d * lax.rsqrt(var + LN_EPS) * g


def _proj_kernel(x_ref, w_ref, b_ref, wg_ref, bg_ref, u_ref, g_ref, xb_ref):
    @pl.when(pl.program_id(1) == 0)
    def _():
        xb = x_ref[...].astype(BF16)
        xb_ref[...] = xb
        g_ref[...] = _dot(xb, wg_ref[...]) + bg_ref[...]

    u_ref[...] = (_dot(xb_ref[...], w_ref[...]) + b_ref[...]).astype(u_ref.dtype)


def _proj(x2d, w_main, b_main, w_gate, b_gate, *, bm=1024, bn=1024):
    m, d = x2d.shape
    n = w_main.shape[1]
    return pl.pallas_call(
        _proj_kernel,
        out_shape=(jax.ShapeDtypeStruct((m, n), BF16), jax.ShapeDtypeStruct((m, GATE_PAD), F32)),
        grid=(m // bm, n // bn),
        in_specs=[
            pl.BlockSpec((bm, d), lambda i, j: (i, 0)),
            pl.BlockSpec((d, bn), lambda i, j: (0, j)),
            pl.BlockSpec((1, bn), lambda i, j: (0, j)),
            pl.BlockSpec((d, GATE_PAD), lambda i, j: (0, 0)),
            pl.BlockSpec((1, GATE_PAD), lambda i, j: (0, 0)),
        ],
        out_specs=(
            pl.BlockSpec((bm, bn), lambda i, j: (i, j)),
            pl.BlockSpec((bm, GATE_PAD), lambda i, j: (i, 0)),
        ),
        scratch_shapes=[pltpu.VMEM((bm, d), BF16)],
        compiler_params=pltpu.CompilerParams(
            dimension_semantics=("arbitrary", "arbitrary"), vmem_limit_bytes=48 * MIB),
        name="proj",
    )(x2d, w_main, b_main, w_gate, b_gate)


def _mlstm_kernel(q_ref, k_ref, v_ref, o_ref, g_ref, cw_ref, cb_ref, ng_ref, out_ref,
                  c_ref, n_ref, m_ref, xbuf_ref):
    L = CHUNK
    hist = SUBLANES

    @pl.when(pl.program_id(1) == 0)
    def _():
        c_ref[...] = jnp.zeros_like(c_ref)
        n_ref[...] = jnp.zeros_like(n_ref)
        m_ref[...] = jnp.zeros_like(m_ref)
        xbuf_ref[0:hist, :] = jnp.zeros((hist, 2 * M_QK_W), F32)

    xbuf_ref[hist:hist + L, 0:M_QK_W] = q_ref[...].astype(F32)
    xbuf_ref[hist:hist + L, M_QK_W:2 * M_QK_W] = k_ref[...].astype(F32)
    conv = cb_ref[...]
    for j in range(CONV_K):
        start = hist - (CONV_K - 1) + j
        conv = conv + cw_ref[j:j + 1, :] * xbuf_ref[start:start + L, :]
    xbuf_ref[0:hist, :] = xbuf_ref[L:L + hist, :]
    qk = conv * _sigmoid(conv)

    g = g_ref[...]
    lfg = _log_sigmoid(g)
    g_rows = g.T[0:2 * M_HEADS, :]
    lf_rows = _log_sigmoid(g_rows)

    row_t = lax.broadcasted_iota(jnp.int32, (L, L), 0)
    col_s = lax.broadcasted_iota(jnp.int32, (L, L), 1)
    causal = col_s <= row_t
    lane = lax.broadcasted_iota(jnp.int32, (L, GATE_PAD), 1)

    for h in range(M_HEADS):
        li_row = g_rows[h:h + 1, :]
        lf_row = lf_rows[M_HEADS + h:M_HEADS + h + 1, :]
        li_col = jnp.sum(jnp.where(lane == h, g, 0.0), axis=1, keepdims=True)
        lf_col = jnp.sum(jnp.where(lane == M_HEADS + h, lfg, 0.0), axis=1, keepdims=True)
        b_col = jnp.sum(jnp.where(causal, lf_row, 0.0), axis=1, keepdims=True)
        b_row = jnp.sum(jnp.where(row_t <= col_s, lf_col, 0.0), axis=0, keepdims=True)
        g_tot = jnp.sum(lf_row, axis=1, keepdims=True)
        m_prev = m_ref[h:h + 1, 0:1]

        d_log = jnp.where(causal, b_col - b_row + li_row, -jnp.inf)
        m_inter = b_col + m_prev
        m_t = jnp.maximum(m_inter, jnp.max(d_log, axis=1, keepdims=True))
        w_intra = jnp.exp(d_log - m_t)
        w_inter = jnp.exp(m_inter - m_t)

        w_log_row = g_tot - b_row + li_row
        w_log_col = g_tot - b_col + li_col
        m_new = jnp.maximum(g_tot + m_prev, jnp.max(w_log_row, axis=1, keepdims=True))
        w_s = jnp.exp(w_log_col - m_new)
        decay = jnp.exp(g_tot + m_prev - m_new)

        qh = qk[:, h * M_DQK:(h + 1) * M_DQK] * (M_DQK ** -0.5)
        kh = qk[:, M_QK_W + h * M_DQK:M_QK_W + (h + 1) * M_DQK]
        qb = qh.astype(BF16)
        vb = v_ref[:, h * M_DV:(h + 1) * M_DV]
        c_old = c_ref[h]
        n_old = n_ref[h:h + 1, :]

        s = _dot_nt(qb, kh.astype(BF16)) * w_intra
        num = _dot(s.astype(BF16), vb) + w_inter * _dot(qb, c_old.astype(BF16))
        den = (jnp.sum(s, axis=1, keepdims=True)
               + w_inter * jnp.sum(qh * n_old, axis=1, keepdims=True))
        hh = num * (1.0 / jnp.maximum(jnp.abs(den), jnp.exp(-m_t)))

        kw = kh * w_s
        c_ref[h] = decay * c_old + _dot(kw.T.astype(BF16), vb)
        n_ref[h:h + 1, :] = decay * n_old + jnp.sum(kw, axis=0, keepdims=True)
        m_ref[h:h + 1, :] = jnp.broadcast_to(m_new, (1, LANES))

        sl = slice(h * M_DV, (h + 1) * M_DV)
        y = _head_norm(hh, ng_ref[:, sl]) * _sigmoid(o_ref[:, sl].astype(F32))
        out_ref[:, sl] = y.astype(out_ref.dtype)


def _mlstm(u, gates, conv_w, conv_b, norm_g, *, batch, n_chunks):
    L = CHUNK
    rows = lambda b, c: b * n_chunks + c
    return pl.pallas_call(
        _mlstm_kernel,
        out_shape=jax.ShapeDtypeStruct((batch * n_chunks * L, M_V_W), BF16),
        grid=(batch, n_chunks),
        in_specs=[
            pl.BlockSpec((L, M_QK_W), lambda b, c: (rows(b, c), 0)),
            pl.BlockSpec((L, M_QK_W), lambda b, c: (rows(b, c), 1)),
            pl.BlockSpec((L, M_V_W), lambda b, c: (rows(b, c), 1)),
            pl.BlockSpec((L, M_V_W), lambda b, c: (rows(b, c), 2)),
            pl.BlockSpec((L, GATE_PAD), lambda b, c: (rows(b, c), 0)),
            pl.BlockSpec((CONV_K, 2 * M_QK_W), lambda b, c: (0, 0)),
            pl.BlockSpec((1, 2 * M_QK_W), lambda b, c: (0, 0)),
            pl.BlockSpec((1, M_V_W), lambda b, c: (0, 0)),
        ],
        out_specs=pl.BlockSpec((L, M_V_W), lambda b, c: (rows(b, c), 0)),
        scratch_shapes=[
            pltpu.VMEM((M_HEADS, M_DQK, M_DV), F32),
            pltpu.VMEM((SUBLANES, M_DQK), F32),
            pltpu.VMEM((SUBLANES, LANES), F32),
            pltpu.VMEM((SUBLANES + L, 2 * M_QK_W), F32),
        ],
        compiler_params=pltpu.CompilerParams(
            dimension_semantics=("arbitrary", "arbitrary"), vmem_limit_bytes=40 * MIB),
        name="mlstm",
    )(u, u, u, u, gates, conv_w, conv_b, norm_g)


def _retention_kernel(q_ref, k_ref, v_ref, sg_ref, pos_ref, invf_ref, sign_ref, ng_ref, out_ref, r_ref):
    L = CHUNK

    @pl.when(pl.program_id(1) == 0)
    def _():
        r_ref[...] = jnp.zeros_like(r_ref)

    ang = pos_ref[...] * invf_ref[...]
    cos = jnp.cos(ang)
    sin = jnp.sin(ang) * sign_ref[...]

    t_col = lax.broadcasted_iota(jnp.int32, (L, 1), 0).astype(F32)
    rel = (lax.broadcasted_iota(jnp.int32, (L, L), 0)
           - lax.broadcasted_iota(jnp.int32, (L, L), 1)).astype(F32)
    scale = R_DQK ** -0.5

    for h in range(R_HEADS):
        log_gamma = math.log1p(-(2.0 ** (-5.0 - h)))
        decay = jnp.where(rel >= 0.0, jnp.exp(jnp.maximum(rel, 0.0) * log_gamma), 0.0) * scale
        xi = jnp.exp((t_col + 1.0) * log_gamma) * scale
        zeta = jnp.exp((L - 1.0 - t_col) * log_gamma)
        g_chunk = math.exp(L * log_gamma)

        qs = slice(h * R_DQK, (h + 1) * R_DQK)
        vs = slice(h * R_DV, (h + 1) * R_DV)
        xq = q_ref[:, qs].astype(F32)
        xk = k_ref[:, qs].astype(F32)
        qr = xq * cos + pltpu.roll(xq, R_DQK // 2, 1) * sin
        kr = xk * cos + pltpu.roll(xk, R_DQK // 2, 1) * sin
        qb = qr.astype(BF16)
        vb = v_ref[:, vs]
        r_old = r_ref[h]

        p = _dot_nt(qb, kr.astype(BF16)) * decay
        o = _dot(p.astype(BF16), vb) + _dot(qb, r_old.astype(BF16)) * xi
        r_ref[h] = g_chunk * r_old + _dot((kr * zeta).T.astype(BF16), vb)

        sg = sg_ref[:, vs].astype(F32)
        y = _head_norm(o, ng_ref[:, vs]) * (sg * _sigmoid(sg))
        out_ref[:, vs] = y.astype(out_ref.dtype)


def _retention(u, pos_b, inv_freq, sign, norm_g, *, batch, n_chunks):
    L = CHUNK
    rows = lambda b, c: b * n_chunks + c
    q_blk = (M_QK_W * 2 + M_V_W * 2) // R_QK_W
    v_blk = (M_QK_W * 2 + M_V_W * 2 + 2 * R_QK_W) // R_V_W
    return pl.pallas_call(
        _retention_kernel,
        out_shape=jax.ShapeDtypeStruct((batch * n_chunks * L, R_V_W), BF16),
        grid=(batch, n_chunks),
        in_specs=[
            pl.BlockSpec((L, R_QK_W), lambda b, c: (rows(b, c), q_blk)),
            pl.BlockSpec((L, R_QK_W), lambda b, c: (rows(b, c), q_blk + 1)),
            pl.BlockSpec((L, R_V_W), lambda b, c: (rows(b, c), v_blk)),
            pl.BlockSpec((L, R_V_W), lambda b, c: (rows(b, c), v_blk + 1)),
            pl.BlockSpec((L, R_DQK), lambda b, c: (rows(b, c), 0)),
            pl.BlockSpec((1, R_DQK), lambda b, c: (0, 0)),
            pl.BlockSpec((1, R_DQK), lambda b, c: (0, 0)),
            pl.BlockSpec((1, R_V_W), lambda b, c: (0, 0)),
        ],
        out_specs=pl.BlockSpec((L, R_V_W), lambda b, c: (rows(b, c), 0)),
        scratch_shapes=[pltpu.VMEM((R_HEADS, R_DQK, R_DV), F32)],
        compiler_params=pltpu.CompilerParams(
            dimension_semantics=("arbitrary", "arbitrary"), vmem_limit_bytes=40 * MIB),
        name="retention",
    )(u, u, u, u, pos_b, inv_freq, sign, norm_g)


def _merge_kernel(alpha, hm_ref, hr_ref, ga_ref, gb_ref, x_ref, wm_ref, wr_ref, wo_ref,
                  bo_ref, lg_ref, lb_ref, out_ref):
    ya = _dot(hm_ref[...], wm_ref[...])
    yb = _dot(hr_ref[...], wr_ref[...])
    merged = (_sigmoid(ga_ref[...].astype(F32)) * ya + _sigmoid(gb_ref[...].astype(F32)) * yb)
    y = _dot(merged.astype(BF16), wo_ref[...]) + bo_ref[...]
    out_ref[...] = _layer_norm(alpha * x_ref[...] + y, lg_ref[...], lb_ref[...])


def _merge(hm, hr, u, x2d, wm, wr, wo, bo, ln_g, ln_b, *, alpha, bm=256):
    m, d = x2d.shape
    ga_blk = (2 * M_QK_W + 2 * M_V_W + 2 * R_QK_W + 2 * R_V_W) // d
    resident = lambda shape: pl.BlockSpec(shape, lambda i: (0, 0), pipeline_mode=pl.Buffered(1))
    return pl.pallas_call(
        functools.partial(_merge_kernel, alpha),
        out_shape=jax.ShapeDtypeStruct((m, d), F32),
        grid=(m // bm,),
        in_specs=[
            pl.BlockSpec((bm, M_V_W), lambda i: (i, 0)),
            pl.BlockSpec((bm, R_V_W), lambda i: (i, 0)),
            pl.BlockSpec((bm, d), lambda i: (i, ga_blk)),
            pl.BlockSpec((bm, d), lambda i: (i, ga_blk + 1)),
            pl.BlockSpec((bm, d), lambda i: (i, 0)),
            resident((M_V_W, d)),
            resident((R_V_W, d)),
            resident((d, d)),
            resident((1, d)),
            resident((1, d)),
            resident((1, d)),
        ],
        out_specs=pl.BlockSpec((bm, d), lambda i: (i, 0)),
        compiler_params=pltpu.CompilerParams(
            dimension_semantics=("arbitrary",), vmem_limit_bytes=52 * MIB),
        name="merge",
    )(hm, hr, u, u, x2d, wm, wr, wo, bo, ln_g, ln_b)


def _mlp_kernel(alpha, x_ref, w1_ref, b1_ref, w2_ref, b2_ref, lg_ref, lb_ref, out_ref, xb_ref):
    j = pl.program_id(1)

    @pl.when(j == 0)
    def _():
        xb_ref[...] = x_ref[...].astype(BF16)
        out_ref[...] = jnp.zeros_like(out_ref)

    hid = jnp.maximum(_dot(xb_ref[...], w1_ref[...]) + b1_ref[...], 0.0)
    out_ref[...] += _dot((hid * hid).astype(BF16), w2_ref[...])

    @pl.when(j == pl.num_programs(1) - 1)
    def _():
        z = alpha * x_ref[...] + (out_ref[...] + b2_ref[...])
        out_ref[...] = _layer_norm(z, lg_ref[...], lb_ref[...])


def _mlp(x2d, w1, b1, w2, b2, ln_g, ln_b, *, alpha, bm=512, bf=1024):
    m, d = x2d.shape
    f = w1.shape[1]
    return pl.pallas_call(
        functools.partial(_mlp_kernel, alpha),
        out_shape=jax.ShapeDtypeStruct((m, d), F32),
        grid=(m // bm, f // bf),
        in_specs=[
            pl.BlockSpec((bm, d), lambda i, j: (i, 0)),
            pl.BlockSpec((d, bf), lambda i, j: (0, j)),
            pl.BlockSpec((1, bf), lambda i, j: (0, j)),
            pl.BlockSpec((bf, d), lambda i, j: (j, 0)),
            pl.BlockSpec((1, d), lambda i, j: (0, 0)),
            pl.BlockSpec((1, d), lambda i, j: (0, 0)),
            pl.BlockSpec((1, d), lambda i, j: (0, 0)),
        ],
        out_specs=pl.BlockSpec((bm, d), lambda i, j: (i, 0)),
        scratch_shapes=[pltpu.VMEM((bm, d), BF16)],
        compiler_params=pltpu.CompilerParams(
            dimension_semantics=("arbitrary", "arbitrary"), vmem_limit_bytes=48 * MIB),
        name="mlp",
    )(x2d, w1, b1, w2, b2, ln_g, ln_b)


def kernel(x, positions, w_in, b_in, m_conv_w, m_conv_b, m_norm_g, r_norm_g, w_branch_m, w_branch_r,
           w_out, b_out, ln1_g, ln1_b, w_ff1, b_ff1, w_ff2, b_ff2, ln2_g, ln2_b):
    batch, seq, d = x.shape
    depth = w_in.shape[0]
    n_chunks = seq // CHUNK
    alpha = (2.0 * depth) ** 0.25
    gate_lo = 2 * M_QK_W + 2 * M_V_W
    gate_hi = gate_lo + 2 * M_HEADS

    half = R_DQK // 2
    inv_freq = ROPE_BASE ** (-jnp.arange(half, dtype=F32) / half)
    inv_freq = jnp.concatenate([inv_freq, inv_freq]).reshape(1, R_DQK)
    sign = jnp.concatenate([-jnp.ones((half,), F32), jnp.ones((half,), F32)]).reshape(1, R_DQK)
    pos_b = jnp.broadcast_to(positions.astype(F32).reshape(batch * seq, 1), (batch * seq, R_DQK))

    h = x.reshape(batch * seq, d)
    for l in range(depth):
        w_l = w_in[l]
        w_main = jnp.concatenate([w_l[:, :gate_lo], w_l[:, gate_hi:]], axis=1).astype(BF16)
        b_main = jnp.concatenate([b_in[l, :gate_lo], b_in[l, gate_hi:]]).reshape(1, -1)
        w_gate = jnp.pad(w_l[:, gate_lo:gate_hi], ((0, 0), (0, GATE_PAD - 2 * M_HEADS))).astype(BF16)
        b_gate = jnp.pad(b_in[l, gate_lo:gate_hi], (0, GATE_PAD - 2 * M_HEADS)).reshape(1, GATE_PAD)

        u, gates = _proj(h, w_main, b_main, w_gate, b_gate)
        hm = _mlstm(u, gates, m_conv_w[l], m_conv_b[l].reshape(1, -1), m_norm_g[l].reshape(1, -1),
                    batch=batch, n_chunks=n_chunks)
        hr = _retention(u, pos_b, inv_freq, sign, r_norm_g[l].reshape(1, -1),
                        batch=batch, n_chunks=n_chunks)
        h1 = _merge(hm, hr, u, h, w_branch_m[l].astype(BF16), w_branch_r[l].astype(BF16),
                    w_out[l].astype(BF16), b_out[l].reshape(1, -1), ln1_g[l].reshape(1, -1),
                    ln1_b[l].reshape(1, -1), alpha=alpha)
        h = _mlp(h1, w_ff1[l].astype(BF16), b_ff1[l].reshape(1, -1), w_ff2[l].astype(BF16),
                 b_ff2[l].reshape(1, -1), ln2_g[l].reshape(1, -1), ln2_b[l].reshape(1, -1), alpha=alpha)
    return h.reshape(batch, seq, d)
```

```python
import functools
import math

import jax
import jax.numpy as jnp
from jax import lax
from jax.experimental import pallas as pl
from jax.experimental.pallas import tpu as pltpu

M_HEADS, M_DQK, M_DV = 4, 256, 512
R_HEADS, R_DQK, R_DV = 8, 128, 256
CONV_K = 4
CHUNK = 128
ROPE_BASE = 10000.0
LN_EPS = 1e-5
M_QK_W = M_HEADS * M_DQK
M_V_W = M_HEADS * M_DV
R_QK_W = R_HEADS * R_DQK
R_V_W = R_HEADS * R_DV

LANES = 128
SUBLANES = 8
GATE_PAD = LANES
MIB = 1024 * 1024

F32 = jnp.float32
BF16 = jnp.bfloat16


def _sigmoid(x):
    return 1.0 / (1.0 + jnp.exp(-x))


def _log_sigmoid(x):
    return jnp.minimum(x, 0.0) - jnp.log1p(jnp.exp(-jnp.abs(x)))


def _dot(a, b):
    return jnp.dot(a, b, preferred_element_type=F32)


def _dot_nt(a, b):
    return lax.dot_general(a, b, (((1,), (1,)), ((), ())), preferred_element_type=F32)


def _layer_norm(z, g, b):
    mu = jnp.mean(z, axis=-1, keepdims=True)
    d = z - mu
    var = jnp.mean(d * d, axis=-1, keepdims=True)
    return d * lax.rsqrt(var + LN_EPS) * g + b


def _head_norm(h, g):
    mu = jnp.mean(h, axis=-1, keepdims=True)
    d = h - mu
    var = jnp.mean(d * d, axis=-1, keepdims=True)
    return d * lax.rsqrt(var + LN_EPS) * g


def _col_tile(step, gate_blk, n_tiles):
    return lax.rem(step + gate_blk, n_tiles)


def _proj_kernel(gate_blk, n_gate, x_ref, wa_ref, wb_ref, ba_ref, bb_ref, u_ref, g_ref, w_scr, b_scr):
    j = _col_tile(pl.program_id(0), gate_blk, pl.num_programs(0))
    d, bn = wa_ref.shape
    rows = 128

    @pl.when(pl.program_id(1) == 0)
    def _():
        @pl.when(j < gate_blk)
        def _():
            def body(r, carry):
                rs = pl.ds(pl.multiple_of(r * rows, rows), rows)
                w_scr[rs, :] = wa_ref[rs, :].astype(BF16)
                return carry
            lax.fori_loop(0, d // rows, body, 0)
            b_scr[...] = ba_ref[...]

        @pl.when(j >= gate_blk)
        def _():
            def body(r, carry):
                rs = pl.ds(pl.multiple_of(r * rows, rows), rows)
                cat = jnp.concatenate([wa_ref[rs, :], wb_ref[rs, :]], axis=1)
                w_scr[rs, :] = cat[:, n_gate:n_gate + bn].astype(BF16)
                return carry
            lax.fori_loop(0, d // rows, body, 0)
            b_scr[...] = jnp.concatenate([ba_ref[...], bb_ref[...]], axis=1)[:, n_gate:n_gate + bn]

    xb = x_ref[...].astype(BF16)
    u_ref[...] = (_dot(xb, w_scr[...]) + b_scr[...]).astype(u_ref.dtype)

    @pl.when(j == gate_blk)
    def _():
        g_ref[...] = _dot(xb, wa_ref[:, 0:GATE_PAD].astype(BF16)) + ba_ref[:, 0:GATE_PAD]

    @pl.when(j != gate_blk)
    def _():
        g_ref[...] = jnp.zeros_like(g_ref)


def _proj(x2d, w, b, *, gate_lo, n_gate, bm=1024, bn=1024):
    m, d = x2d.shape
    n = w.shape[1] - n_gate
    assert gate_lo % bn == 0 and n % bn == 0 and m % bm == 0 and n_gate <= LANES
    gate_blk = gate_lo // bn
    n_i = m // bm
    nxt = bn // LANES
    n_j = n // bn
    col = lambda s: _col_tile(s, gate_blk, n_j)
    return pl.pallas_call(
        functools.partial(_proj_kernel, gate_blk, n_gate),
        out_shape=(jax.ShapeDtypeStruct((m, n), BF16),
                   jax.ShapeDtypeStruct(((n_i + 1) * bm, GATE_PAD), F32)),
        grid=(n_j, n_i),
        in_specs=[
            pl.BlockSpec((bm, d), lambda s, i: (i, 0)),
            pl.BlockSpec((d, bn), lambda s, i: (0, col(s))),
            pl.BlockSpec((d, LANES), lambda s, i: (0, (col(s) + 1) * nxt)),
            pl.BlockSpec((1, bn), lambda s, i: (0, col(s))),
            pl.BlockSpec((1, LANES), lambda s, i: (0, (col(s) + 1) * nxt)),
        ],
        out_specs=(
            pl.BlockSpec((bm, bn), lambda s, i: (i, col(s))),
            pl.BlockSpec((bm, GATE_PAD), lambda s, i: (jnp.where(s == 0, i, n_i), 0)),
        ),
        scratch_shapes=[pltpu.VMEM((d, bn), BF16), pltpu.VMEM((1, bn), F32)],
        compiler_params=pltpu.CompilerParams(
            dimension_semantics=("arbitrary", "arbitrary"), vmem_limit_bytes=48 * MIB),
        name="proj",
    )(x2d, w, w, b, b)


def _mlstm_kernel(q_ref, k_ref, v_ref, o_ref, g_ref, cw_ref, cb_ref, ng_ref, out_ref,
                  c_ref, n_ref, m_ref, xbuf_ref):
    L = CHUNK
    hist = SUBLANES

    @pl.when(pl.program_id(1) == 0)
    def _():
        c_ref[...] = jnp.zeros_like(c_ref)
        n_ref[...] = jnp.zeros_like(n_ref)
        m_ref[...] = jnp.zeros_like(m_ref)
        xbuf_ref[0:hist, :] = jnp.zeros((hist, 2 * M_QK_W), F32)

    xbuf_ref[hist:hist + L, 0:M_QK_W] = q_ref[...].astype(F32)
    xbuf_ref[hist:hist + L, M_QK_W:2 * M_QK_W] = k_ref[...].astype(F32)
    conv = cb_ref[...]
    for j in range(CONV_K):
        start = hist - (CONV_K - 1) + j
        conv = conv + cw_ref[j:j + 1, :] * xbuf_ref[start:start + L, :]
    xbuf_ref[0:hist, :] = xbuf_ref[L:L + hist, :]
    qk = conv * _sigmoid(conv)

    g = g_ref[...]
    lfg = _log_sigmoid(g)
    g_rows = g.T[0:2 * M_HEADS, :]
    lf_rows = _log_sigmoid(g_rows)

    row_t = lax.broadcasted_iota(jnp.int32, (L, L), 0)
    col_s = lax.broadcasted_iota(jnp.int32, (L, L), 1)
    causal = col_s <= row_t
    lane = lax.broadcasted_iota(jnp.int32, (L, GATE_PAD), 1)

    for h in range(M_HEADS):
        li_row = g_rows[h:h + 1, :]
        lf_row = lf_rows[M_HEADS + h:M_HEADS + h + 1, :]
        li_col = jnp.sum(jnp.where(lane == h, g, 0.0), axis=1, keepdims=True)
        lf_col = jnp.sum(jnp.where(lane == M_HEADS + h, lfg, 0.0), axis=1, keepdims=True)
        b_col = jnp.sum(jnp.where(causal, lf_row, 0.0), axis=1, keepdims=True)
        b_row = jnp.sum(jnp.where(row_t <= col_s, lf_col, 0.0), axis=0, keepdims=True)
        g_tot = jnp.sum(lf_row, axis=1, keepdims=True)
        m_prev = m_ref[h:h + 1, 0:1]

        d_log = jnp.where(causal, b_col - b_row + li_row, -jnp.inf)
        m_inter = b_col + m_prev
        m_t = jnp.maximum(m_inter, jnp.max(d_log, axis=1, keepdims=True))
        w_intra = jnp.exp(d_log - m_t)
        w_inter = jnp.exp(m_inter - m_t)

        w_log_row = g_tot - b_row + li_row
        w_log_col = g_tot - b_col + li_col
        m_new = jnp.maximum(g_tot + m_prev, jnp.max(w_log_row, axis=1, keepdims=True))
        w_s = jnp.exp(w_log_col - m_new)
        decay = jnp.exp(g_tot + m_prev - m_new)

        qh = qk[:, h * M_DQK:(h + 1) * M_DQK] * (M_DQK ** -0.5)
        kh = qk[:, M_QK_W + h * M_DQK:M_QK_W + (h + 1) * M_DQK]
        qb = qh.astype(BF16)
        vb = v_ref[:, h * M_DV:(h + 1) * M_DV]
        c_old = c_ref[h]
        n_old = n_ref[h:h + 1, :]

        s = _dot_nt(qb, kh.astype(BF16)) * w_intra
        num = _dot(s.astype(BF16), vb) + w_inter * _dot(qb, c_old.astype(BF16))
        den = (jnp.sum(s, axis=1, keepdims=True)
               + w_inter * jnp.sum(qh * n_old, axis=1, keepdims=True))
        hh = num * (1.0 / jnp.maximum(jnp.abs(den), jnp.exp(-m_t)))

        kw = kh * w_s
        c_ref[h] = decay * c_old + _dot(kw.T.astype(BF16), vb)
        n_ref[h:h + 1, :] = decay * n_old + jnp.sum(kw, axis=0, keepdims=True)
        m_ref[h:h + 1, :] = jnp.broadcast_to(m_new, (1, LANES))

        sl = slice(h * M_DV, (h + 1) * M_DV)
        y = _head_norm(hh, ng_ref[:, sl]) * _sigmoid(o_ref[:, sl].astype(F32))
        out_ref[:, sl] = y.astype(out_ref.dtype)


def _mlstm(u, gates, conv_w, conv_b, norm_g, *, batch, n_chunks):
    L = CHUNK
    rows = lambda b, c: b * n_chunks + c
    return pl.pallas_call(
        _mlstm_kernel,
        out_shape=jax.ShapeDtypeStruct((batch * n_chunks * L, M_V_W), BF16),
        grid=(batch, n_chunks),
        in_specs=[
            pl.BlockSpec((L, M_QK_W), lambda b, c: (rows(b, c), 0)),
            pl.BlockSpec((L, M_QK_W), lambda b, c: (rows(b, c), 1)),
            pl.BlockSpec((L, M_V_W), lambda b, c: (rows(b, c), 1)),
            pl.BlockSpec((L, M_V_W), lambda b, c: (rows(b, c), 2)),
            pl.BlockSpec((L, GATE_PAD), lambda b, c: (rows(b, c), 0)),
            pl.BlockSpec((CONV_K, 2 * M_QK_W), lambda b, c: (0, 0)),
            pl.BlockSpec((1, 2 * M_QK_W), lambda b, c: (0, 0)),
            pl.BlockSpec((1, M_V_W), lambda b, c: (0, 0)),
        ],
        out_specs=pl.BlockSpec((L, M_V_W), lambda b, c: (rows(b, c), 0)),
        scratch_shapes=[
            pltpu.VMEM((M_HEADS, M_DQK, M_DV), F32),
            pltpu.VMEM((SUBLANES, M_DQK), F32),
            pltpu.VMEM((SUBLANES, LANES), F32),
            pltpu.VMEM((SUBLANES + L, 2 * M_QK_W), F32),
        ],
        compiler_params=pltpu.CompilerParams(
            dimension_semantics=("arbitrary", "arbitrary"), vmem_limit_bytes=40 * MIB),
        name="mlstm",
    )(u, u, u, u, gates, conv_w, conv_b, norm_g)


def _retention_kernel(q_ref, k_ref, v_ref, sg_ref, pos_ref, invf_ref, sign_ref, ng_ref, out_ref, r_ref):
    L = CHUNK

    @pl.when(pl.program_id(1) == 0)
    def _():
        r_ref[...] = jnp.zeros_like(r_ref)

    ang = pos_ref[...] * invf_ref[...]
    cos = jnp.cos(ang)
    sin = jnp.sin(ang) * sign_ref[...]

    t_col = lax.broadcasted_iota(jnp.int32, (L, 1), 0).astype(F32)
    rel = (lax.broadcasted_iota(jnp.int32, (L, L), 0)
           - lax.broadcasted_iota(jnp.int32, (L, L), 1)).astype(F32)
    scale = R_DQK ** -0.5

    for h in range(R_HEADS):
        log_gamma = math.log1p(-(2.0 ** (-5.0 - h)))
        decay = jnp.where(rel >= 0.0, jnp.exp(jnp.maximum(rel, 0.0) * log_gamma), 0.0) * scale
        xi = jnp.exp((t_col + 1.0) * log_gamma) * scale
        zeta = jnp.exp((L - 1.0 - t_col) * log_gamma)
        g_chunk = math.exp(L * log_gamma)

        qs = slice(h * R_DQK, (h + 1) * R_DQK)
        vs = slice(h * R_DV, (h + 1) * R_DV)
        xq = q_ref[:, qs].astype(F32)
        xk = k_ref[:, qs].astype(F32)
        qr = xq * cos + pltpu.roll(xq, R_DQK // 2, 1) * sin
        kr = xk * cos + pltpu.roll(xk, R_DQK // 2, 1) * sin
        qb = qr.astype(BF16)
        vb = v_ref[:, vs]
        r_old = r_ref[h]

        p = _dot_nt(qb, kr.astype(BF16)) * decay
        o = _dot(p.astype(BF16), vb) + _dot(qb, r_old.astype(BF16)) * xi
        r_ref[h] = g_chunk * r_old + _dot((kr * zeta).T.astype(BF16), vb)

        sg = sg_ref[:, vs].astype(F32)
        y = _head_norm(o, ng_ref[:, vs]) * (sg * _sigmoid(sg))
        out_ref[:, vs] = y.astype(out_ref.dtype)


def _retention(u, pos_b, inv_freq, sign, norm_g, *, batch, n_chunks):
    L = CHUNK
    rows = lambda b, c: b * n_chunks + c
    q_blk = (M_QK_W * 2 + M_V_W * 2) // R_QK_W
    v_blk = (M_QK_W * 2 + M_V_W * 2 + 2 * R_QK_W) // R_V_W
    return pl.pallas_call(
        _retention_kernel,
        out_shape=jax.ShapeDtypeStruct((batch * n_chunks * L, R_V_W), BF16),
        grid=(batch, n_chunks),
        in_specs=[
            pl.BlockSpec((L, R_QK_W), lambda b, c: (rows(b, c), q_blk)),
            pl.BlockSpec((L, R_QK_W), lambda b, c: (rows(b, c), q_blk + 1)),
            pl.BlockSpec((L, R_V_W), lambda b, c: (rows(b, c), v_blk)),
            pl.BlockSpec((L, R_V_W), lambda b, c: (rows(b, c), v_blk + 1)),
            pl.BlockSpec((L, R_DQK), lambda b, c: (rows(b, c), 0)),
            pl.BlockSpec((1, R_DQK), lambda b, c: (0, 0)),
            pl.BlockSpec((1, R_DQK), lambda b, c: (0, 0)),
            pl.BlockSpec((1, R_V_W), lambda b, c: (0, 0)),
        ],
        out_specs=pl.BlockSpec((L, R_V_W), lambda b, c: (rows(b, c), 0)),
        scratch_shapes=[pltpu.VMEM((R_HEADS, R_DQK, R_DV), F32)],
        compiler_params=pltpu.CompilerParams(
            dimension_semantics=("arbitrary", "arbitrary"), vmem_limit_bytes=40 * MIB),
        name="retention",
    )(u, u, u, u, pos_b, inv_freq, sign, norm_g)


def _merge_kernel(alpha, hm_ref, hr_ref, ga_ref, gb_ref, x_ref, wm_ref, wr_ref, wo_ref,
                  bo_ref, lg_ref, lb_ref, out_ref):
    ya = _dot(hm_ref[...], wm_ref[...])
    yb = _dot(hr_ref[...], wr_ref[...])
    merged = (_sigmoid(ga_ref[...].astype(F32)) * ya + _sigmoid(gb_ref[...].astype(F32)) * yb)
    y = _dot(merged.astype(BF16), wo_ref[...]) + bo_ref[...]
    out_ref[...] = _layer_norm(alpha * x_ref[...] + y, lg_ref[...], lb_ref[...])


def _merge(hm, hr, u, x2d, wm, wr, wo, bo, ln_g, ln_b, *, alpha, bm=256):
    m, d = x2d.shape
    ga_blk = (2 * M_QK_W + 2 * M_V_W + 2 * R_QK_W + 2 * R_V_W) // d
    resident = lambda shape: pl.BlockSpec(shape, lambda i: (0, 0), pipeline_mode=pl.Buffered(1))
    return pl.pallas_call(
        functools.partial(_merge_kernel, alpha),
        out_shape=jax.ShapeDtypeStruct((m, d), F32),
        grid=(m // bm,),
        in_specs=[
            pl.BlockSpec((bm, M_V_W), lambda i: (i, 0)),
            pl.BlockSpec((bm, R_V_W), lambda i: (i, 0)),
            pl.BlockSpec((bm, d), lambda i: (i, ga_blk)),
            pl.BlockSpec((bm, d), lambda i: (i, ga_blk + 1)),
            pl.BlockSpec((bm, d), lambda i: (i, 0)),
            resident((M_V_W, d)),
            resident((R_V_W, d)),
            resident((d, d)),
            resident((1, d)),
            resident((1, d)),
            resident((1, d)),
        ],
        out_specs=pl.BlockSpec((bm, d), lambda i: (i, 0)),
        compiler_params=pltpu.CompilerParams(
            dimension_semantics=("arbitrary",), vmem_limit_bytes=52 * MIB),
        name="merge",
    )(hm, hr, u, u, x2d, wm, wr, wo, bo, ln_g, ln_b)


def _mlp_kernel(alpha, x_ref, w1_ref, b1_ref, w2_ref, b2_ref, lg_ref, lb_ref, out_ref, xb_ref):
    j = pl.program_id(1)

    @pl.when(j == 0)
    def _():
        xb_ref[...] = x_ref[...].astype(BF16)
        out_ref[...] = jnp.zeros_like(out_ref)

    hid = jnp.maximum(_dot(xb_ref[...], w1_ref[...]) + b1_ref[...], 0.0)
    out_ref[...] += _dot((hid * hid).astype(BF16), w2_ref[...])

    @pl.when(j == pl.num_programs(1) - 1)
    def _():
        z = alpha * x_ref[...] + (out_ref[...] + b2_ref[...])
        out_ref[...] = _layer_norm(z, lg_ref[...], lb_ref[...])


def _mlp(x2d, w1, b1, w2, b2, ln_g, ln_b, *, alpha, bm=512, bf=1024):
    m, d = x2d.shape
    f = w1.shape[1]
    return pl.pallas_call(
        functools.partial(_mlp_kernel, alpha),
        out_shape=jax.ShapeDtypeStruct((m, d), F32),
        grid=(m // bm, f // bf),
        in_specs=[
            pl.BlockSpec((bm, d), lambda i, j: (i, 0)),
            pl.BlockSpec((d, bf), lambda i, j: (0, j)),
            pl.BlockSpec((1, bf), lambda i, j: (0, j)),
            pl.BlockSpec((bf, d), lambda i, j: (j, 0)),
            pl.BlockSpec((1, d), lambda i, j: (0, 0)),
            pl.BlockSpec((1, d), lambda i, j: (0, 0)),
            pl.BlockSpec((1, d), lambda i, j: (0, 0)),
        ],
        out_specs=pl.BlockSpec((bm, d), lambda i, j: (i, 0)),
        scratch_shapes=[pltpu.VMEM((bm, d), BF16)],
        compiler_params=pltpu.CompilerParams(
            dimension_semantics=("arbitrary", "arbitrary"), vmem_limit_bytes=48 * MIB),
        name="mlp",
    )(x2d, w1, b1, w2, b2, ln_g, ln_b)


def kernel(x, positions, w_in, b_in, m_conv_w, m_conv_b, m_norm_g, r_norm_g, w_branch_m, w_branch_r,
           w_out, b_out, ln1_g, ln1_b, w_ff1, b_ff1, w_ff2, b_ff2, ln2_g, ln2_b):
    batch, seq, d = x.shape
    depth = w_in.shape[0]
    n_chunks = seq // CHUNK
    alpha = (2.0 * depth) ** 0.25
    gate_lo = 2 * M_QK_W + 2 * M_V_W
    gate_hi = gate_lo + 2 * M_HEADS

    half = R_DQK // 2
    inv_freq = ROPE_BASE ** (-jnp.arange(half, dtype=F32) / half)
    inv_freq = jnp.concatenate([inv_freq, inv_freq]).reshape(1, R_DQK)
    sign = jnp.concatenate([-jnp.ones((half,), F32), jnp.ones((half,), F32)]).reshape(1, R_DQK)
    pos_b = jnp.broadcast_to(positions.astype(F32).reshape(batch * seq, 1), (batch * seq, R_DQK))

    h = x.reshape(batch * seq, d)
    for l in range(depth):
        u, gates = _proj(h, w_in[l], b_in[l].reshape(1, -1), gate_lo=gate_lo, n_gate=2 * M_HEADS)
        hm = _mlstm(u, gates, m_conv_w[l], m_conv_b[l].reshape(1, -1), m_norm_g[l].reshape(1, -1),
                    batch=batch, n_chunks=n_chunks)
        hr = _retention(u, pos_b, inv_freq, sign, r_norm_g[l].reshape(1, -1),
                        batch=batch, n_chunks=n_chunks)
        h1 = _merge(hm, hr, u, h, w_branch_m[l].astype(BF16), w_branch_r[l].astype(BF16),
                    w_out[l].astype(BF16), b_out[l].reshape(1, -1), ln1_g[l].reshape(1, -1),
                    ln1_b[l].reshape(1, -1), alpha=alpha)
        h = _mlp(h1, w_ff1[l].astype(BF16), b_ff1[l].reshape(1, -1), w_ff2[l].astype(BF16),
                 b_ff2[l].reshape(1, -1), ln2_g[l].reshape(1, -1), ln2_b[l].reshape(1, -1), alpha=alpha)
    return h.reshape(batch, seq, d)
```

```python
import functools
import math

import jax
import jax.numpy as jnp
from jax import lax
from jax.experimental import pallas as pl
from jax.experimental.pallas import tpu as pltpu

M_HEADS, M_DQK, M_DV = 4, 256, 512
R_HEADS, R_DQK, R_DV = 8, 128, 256
CONV_K = 4
CHUNK = 128
ROPE_BASE = 10000.0
LN_EPS = 1e-5
M_QK_W = M_HEADS * M_DQK
M_V_W = M_HEADS * M_DV
R_QK_W = R_HEADS * R_DQK
R_V_W = R_HEADS * R_DV

LANES = 128
SUBLANES = 8
GATE_PAD = LANES
MIB = 1024 * 1024

F32 = jnp.float32
BF16 = jnp.bfloat16


def _sigmoid(x):
    return 1.0 / (1.0 + jnp.exp(-x))


def _log_sigmoid(x):
    return jnp.minimum(x, 0.0) - jnp.log1p(jnp.exp(-jnp.abs(x)))


def _dot(a, b):
    return jnp.dot(a, b, preferred_element_type=F32)


def _dot_nt(a, b):
    return lax.dot_general(a, b, (((1,), (1,)), ((), ())), preferred_element_type=F32)


def _layer_norm(z, g, b):
    mu = jnp.mean(z, axis=-1, keepdims=True)
    d = z - mu
    var = jnp.mean(d * d, axis=-1, keepdims=True)
    return d * lax.rsqrt(var + LN_EPS) * g + b


def _head_norm(h, g):
    mu = jnp.mean(h, axis=-1, keepdims=True)
    d = h - mu
    var = jnp.mean(d * d, axis=-1, keepdims=True)
    return d * lax.rsqrt(var + LN_EPS) * g


def _col_tile(step, gate_blk, n_tiles):
    return lax.rem(step + gate_blk, n_tiles)


def _proj_kernel(gate_blk, n_gate, x_ref, wa_ref, wb_ref, ba_ref, bb_ref, u_ref, g_ref, w_scr, b_scr):
    j = _col_tile(pl.program_id(0), gate_blk, pl.num_programs(0))
    bn, d = wa_ref.shape
    rows = 128

    @pl.when(pl.program_id(1) == 0)
    def _():
        @pl.when(j < gate_blk)
        def _():
            for r in range(0, bn, rows):
                w_scr[r:r + rows, :] = wa_ref[r:r + rows, :].astype(BF16)
            b_scr[...] = ba_ref[...]

        @pl.when(j >= gate_blk)
        def _():
            for r in range(0, bn, rows):
                lo, hi = r + n_gate, r + n_gate + rows
                if hi <= bn:
                    blk = wa_ref[lo:hi, :]
                else:
                    blk = jnp.concatenate([wa_ref[lo:bn, :], wb_ref[0:hi - bn, :]], axis=0)
                w_scr[r:r + rows, :] = blk.astype(BF16)
            b_scr[...] = jnp.concatenate([ba_ref[...], bb_ref[...]], axis=1)[:, n_gate:n_gate + bn]

    xb = x_ref[...].astype(BF16)
    u_ref[...] = (_dot_nt(xb, w_scr[...]) + b_scr[...]).astype(u_ref.dtype)

    @pl.when(j == gate_blk)
    def _():
        g_ref[...] = _dot_nt(xb, wa_ref[0:GATE_PAD, :].astype(BF16)) + ba_ref[:, 0:GATE_PAD]

    @pl.when(j != gate_blk)
    def _():
        g_ref[...] = jnp.zeros_like(g_ref)


def _proj(x2d, w_t, b, *, gate_lo, n_gate, bm=1024, bn=1024):
    m, d = x2d.shape
    n = w_t.shape[0] - n_gate
    assert gate_lo % bn == 0 and n % bn == 0 and m % bm == 0
    assert n_gate % SUBLANES == 0 and n_gate <= LANES
    gate_blk = gate_lo // bn
    n_i = m // bm
    n_j = n // bn
    col = lambda s: _col_tile(s, gate_blk, n_j)
    return pl.pallas_call(
        functools.partial(_proj_kernel, gate_blk, n_gate),
        out_shape=(jax.ShapeDtypeStruct((m, n), BF16),
                   jax.ShapeDtypeStruct(((n_i + 1) * bm, GATE_PAD), F32)),
        grid=(n_j, n_i),
        in_specs=[
            pl.BlockSpec((bm, d), lambda s, i: (i, 0)),
            pl.BlockSpec((bn, d), lambda s, i: (col(s), 0)),
            pl.BlockSpec((n_gate, d), lambda s, i: ((col(s) + 1) * (bn // n_gate), 0)),
            pl.BlockSpec((1, bn), lambda s, i: (0, col(s))),
            pl.BlockSpec((1, LANES), lambda s, i: (0, (col(s) + 1) * (bn // LANES))),
        ],
        out_specs=(
            pl.BlockSpec((bm, bn), lambda s, i: (i, col(s))),
            pl.BlockSpec((bm, GATE_PAD), lambda s, i: (jnp.where(s == 0, i, n_i), 0)),
        ),
        scratch_shapes=[pltpu.VMEM((bn, d), BF16), pltpu.VMEM((1, bn), F32)],
        compiler_params=pltpu.CompilerParams(
            dimension_semantics=("arbitrary", "arbitrary"), vmem_limit_bytes=48 * MIB),
        name="proj",
    )(x2d, w_t, w_t, b, b)


def _mlstm_kernel(q_ref, k_ref, v_ref, o_ref, g_ref, cw_ref, cb_ref, ng_ref, out_ref,
                  c_ref, n_ref, m_ref, xbuf_ref):
    L = CHUNK
    hist = SUBLANES

    @pl.when(pl.program_id(1) == 0)
    def _():
        c_ref[...] = jnp.zeros_like(c_ref)
        n_ref[...] = jnp.zeros_like(n_ref)
        m_ref[...] = jnp.zeros_like(m_ref)
        xbuf_ref[0:hist, :] = jnp.zeros((hist, 2 * M_QK_W), F32)

    xbuf_ref[hist:hist + L, 0:M_QK_W] = q_ref[...].astype(F32)
    xbuf_ref[hist:hist + L, M_QK_W:2 * M_QK_W] = k_ref[...].astype(F32)
    conv = cb_ref[...]
    for j in range(CONV_K):
        start = hist - (CONV_K - 1) + j
        conv = conv + cw_ref[j:j + 1, :] * xbuf_ref[start:start + L, :]
    xbuf_ref[0:hist, :] = xbuf_ref[L:L + hist, :]
    qk = conv * _sigmoid(conv)

    g = g_ref[...]
    lfg = _log_sigmoid(g)
    g_rows = g.T[0:2 * M_HEADS, :]
    lf_rows = _log_sigmoid(g_rows)

    row_t = lax.broadcasted_iota(jnp.int32, (L, L), 0)
    col_s = lax.broadcasted_iota(jnp.int32, (L, L), 1)
    causal = col_s <= row_t
    lane = lax.broadcasted_iota(jnp.int32, (L, GATE_PAD), 1)

    for h in range(M_HEADS):
        li_row = g_rows[h:h + 1, :]
        lf_row = lf_rows[M_HEADS + h:M_HEADS + h + 1, :]
        li_col = jnp.sum(jnp.where(lane == h, g, 0.0), axis=1, keepdims=True)
        lf_col = jnp.sum(jnp.where(lane == M_HEADS + h, lfg, 0.0), axis=1, keepdims=True)
        b_col = jnp.sum(jnp.where(causal, lf_row, 0.0), axis=1, keepdims=True)
        b_row = jnp.sum(jnp.where(row_t <= col_s, lf_col, 0.0), axis=0, keepdims=True)
        g_tot = jnp.sum(lf_row, axis=1, keepdims=True)
        m_prev = m_ref[h:h + 1, 0:1]

        d_log = jnp.where(causal, b_col - b_row + li_row, -jnp.inf)
        m_inter = b_col + m_prev
        m_t = jnp.maximum(m_inter, jnp.max(d_log, axis=1, keepdims=True))
        w_intra = jnp.exp(d_log - m_t)
        w_inter = jnp.exp(m_inter - m_t)

        w_log_row = g_tot - b_row + li_row
        w_log_col = g_tot - b_col + li_col
        m_new = jnp.maximum(g_tot + m_prev, jnp.max(w_log_row, axis=1, keepdims=True))
        w_s = jnp.exp(w_log_col - m_new)
        decay = jnp.exp(g_tot + m_prev - m_new)

        qh = qk[:, h * M_DQK:(h + 1) * M_DQK] * (M_DQK ** -0.5)
        kh = qk[:, M_QK_W + h * M_DQK:M_QK_W + (h + 1) * M_DQK]
        qb = qh.astype(BF16)
        vb = v_ref[:, h * M_DV:(h + 1) * M_DV]
        c_old = c_ref[h]
        n_old = n_ref[h:h + 1, :]

        s = _dot_nt(qb, kh.astype(BF16)) * w_intra
        num = _dot(s.astype(BF16), vb) + w_inter * _dot(qb, c_old.astype(BF16))
        den = (jnp.sum(s, axis=1, keepdims=True)
               + w_inter * jnp.sum(qh * n_old, axis=1, keepdims=True))
        hh = num * (1.0 / jnp.maximum(jnp.abs(den), jnp.exp(-m_t)))

        kw = kh * w_s
        c_ref[h] = decay * c_old + _dot(kw.T.astype(BF16), vb)
        n_ref[h:h + 1, :] = decay * n_old + jnp.sum(kw, axis=0, keepdims=True)
        m_ref[h:h + 1, :] = jnp.broadcast_to(m_new, (1, LANES))

        sl = slice(h * M_DV, (h + 1) * M_DV)
        y = _head_norm(hh, ng_ref[:, sl]) * _sigmoid(o_ref[:, sl].astype(F32))
        out_ref[:, sl] = y.astype(out_ref.dtype)


def _mlstm(u, gates, conv_w, conv_b, norm_g, *, batch, n_chunks):
    L = CHUNK
    rows = lambda b, c: b * n_chunks + c
    return pl.pallas_call(
        _mlstm_kernel,
        out_shape=jax.ShapeDtypeStruct((batch * n_chunks * L, M_V_W), BF16),
        grid=(batch, n_chunks),
        in_specs=[
            pl.BlockSpec((L, M_QK_W), lambda b, c: (rows(b, c), 0)),
            pl.BlockSpec((L, M_QK_W), lambda b, c: (rows(b, c), 1)),
            pl.BlockSpec((L, M_V_W), lambda b, c: (rows(b, c), 1)),
            pl.BlockSpec((L, M_V_W), lambda b, c: (rows(b, c), 2)),
            pl.BlockSpec((L, GATE_PAD), lambda b, c: (rows(b, c), 0)),
            pl.BlockSpec((CONV_K, 2 * M_QK_W), lambda b, c: (0, 0)),
            pl.BlockSpec((1, 2 * M_QK_W), lambda b, c: (0, 0)),
            pl.BlockSpec((1, M_V_W), lambda b, c: (0, 0)),
        ],
        out_specs=pl.BlockSpec((L, M_V_W), lambda b, c: (rows(b, c), 0)),
        scratch_shapes=[
            pltpu.VMEM((M_HEADS, M_DQK, M_DV), F32),
            pltpu.VMEM((SUBLANES, M_DQK), F32),
            pltpu.VMEM((SUBLANES, LANES), F32),
            pltpu.VMEM((SUBLANES + L, 2 * M_QK_W), F32),
        ],
        compiler_params=pltpu.CompilerParams(
            dimension_semantics=("arbitrary", "arbitrary"), vmem_limit_bytes=40 * MIB),
        name="mlstm",
    )(u, u, u, u, gates, conv_w, conv_b, norm_g)


def _retention_kernel(q_ref, k_ref, v_ref, sg_ref, pos_ref, invf_ref, sign_ref, ng_ref, out_ref, r_ref):
    L = CHUNK

    @pl.when(pl.program_id(1) == 0)
    def _():
        r_ref[...] = jnp.zeros_like(r_ref)

    ang = pos_ref[...] * invf_ref[...]
    cos = jnp.cos(ang)
    sin = jnp.sin(ang) * sign_ref[...]

    t_col = lax.broadcasted_iota(jnp.int32, (L, 1), 0).astype(F32)
    rel = (lax.broadcasted_iota(jnp.int32, (L, L), 0)
           - lax.broadcasted_iota(jnp.int32, (L, L), 1)).astype(F32)
    scale = R_DQK ** -0.5

    for h in range(R_HEADS):
        log_gamma = math.log1p(-(2.0 ** (-5.0 - h)))
        decay = jnp.where(rel >= 0.0, jnp.exp(jnp.maximum(rel, 0.0) * log_gamma), 0.0) * scale
        xi = jnp.exp((t_col + 1.0) * log_gamma) * scale
        zeta = jnp.exp((L - 1.0 - t_col) * log_gamma)
        g_chunk = math.exp(L * log_gamma)

        qs = slice(h * R_DQK, (h + 1) * R_DQK)
        vs = slice(h * R_DV, (h + 1) * R_DV)
        xq = q_ref[:, qs].astype(F32)
        xk = k_ref[:, qs].astype(F32)
        qr = xq * cos + pltpu.roll(xq, R_DQK // 2, 1) * sin
        kr = xk * cos + pltpu.roll(xk, R_DQK // 2, 1) * sin
        qb = qr.astype(BF16)
        vb = v_ref[:, vs]
        r_old = r_ref[h]

        p = _dot_nt(qb, kr.astype(BF16)) * decay
        o = _dot(p.astype(BF16), vb) + _dot(qb, r_old.astype(BF16)) * xi
        r_ref[h] = g_chunk * r_old + _dot((kr * zeta).T.astype(BF16), vb)

        sg = sg_ref[:, vs].astype(F32)
        y = _head_norm(o, ng_ref[:, vs]) * (sg * _sigmoid(sg))
        out_ref[:, vs] = y.astype(out_ref.dtype)


def _retention(u, pos_b, inv_freq, sign, norm_g, *, batch, n_chunks):
    L = CHUNK
    rows = lambda b, c: b * n_chunks + c
    q_blk = (M_QK_W * 2 + M_V_W * 2) // R_QK_W
    v_blk = (M_QK_W * 2 + M_V_W * 2 + 2 * R_QK_W) // R_V_W
    return pl.pallas_call(
        _retention_kernel,
        out_shape=jax.ShapeDtypeStruct((batch * n_chunks * L, R_V_W), BF16),
        grid=(batch, n_chunks),
        in_specs=[
            pl.BlockSpec((L, R_QK_W), lambda b, c: (rows(b, c), q_blk)),
            pl.BlockSpec((L, R_QK_W), lambda b, c: (rows(b, c), q_blk + 1)),
            pl.BlockSpec((L, R_V_W), lambda b, c: (rows(b, c), v_blk)),
            pl.BlockSpec((L, R_V_W), lambda b, c: (rows(b, c), v_blk + 1)),
            pl.BlockSpec((L, R_DQK), lambda b, c: (rows(b, c), 0)),
            pl.BlockSpec((1, R_DQK), lambda b, c: (0, 0)),
            pl.BlockSpec((1, R_DQK), lambda b, c: (0, 0)),
            pl.BlockSpec((1, R_V_W), lambda b, c: (0, 0)),
        ],
        out_specs=pl.BlockSpec((L, R_V_W), lambda b, c: (rows(b, c), 0)),
        scratch_shapes=[pltpu.VMEM((R_HEADS, R_DQK, R_DV), F32)],
        compiler_params=pltpu.CompilerParams(
            dimension_semantics=("arbitrary", "arbitrary"), vmem_limit_bytes=40 * MIB),
        name="retention",
    )(u, u, u, u, pos_b, inv_freq, sign, norm_g)


def _merge_kernel(alpha, hm_ref, hr_ref, ga_ref, gb_ref, x_ref, wm_ref, wr_ref, wo_ref,
                  bo_ref, lg_ref, lb_ref, out_ref):
    ya = _dot(hm_ref[...], wm_ref[...])
    yb = _dot(hr_ref[...], wr_ref[...])
    merged = (_sigmoid(ga_ref[...].astype(F32)) * ya + _sigmoid(gb_ref[...].astype(F32)) * yb)
    y = _dot(merged.astype(BF16), wo_ref[...]) + bo_ref[...]
    out_ref[...] = _layer_norm(alpha * x_ref[...] + y, lg_ref[...], lb_ref[...])


def _merge(hm, hr, u, x2d, wm, wr, wo, bo, ln_g, ln_b, *, alpha, bm=256):
    m, d = x2d.shape
    ga_blk = (2 * M_QK_W + 2 * M_V_W + 2 * R_QK_W + 2 * R_V_W) // d
    resident = lambda shape: pl.BlockSpec(shape, lambda i: (0, 0), pipeline_mode=pl.Buffered(1))
    return pl.pallas_call(
        functools.partial(_merge_kernel, alpha),
        out_shape=jax.ShapeDtypeStruct((m, d), F32),
        grid=(m // bm,),
        in_specs=[
            pl.BlockSpec((bm, M_V_W), lambda i: (i, 0)),
            pl.BlockSpec((bm, R_V_W), lambda i: (i, 0)),
            pl.BlockSpec((bm, d), lambda i: (i, ga_blk)),
            pl.BlockSpec((bm, d), lambda i: (i, ga_blk + 1)),
            pl.BlockSpec((bm, d), lambda i: (i, 0)),
            resident((M_V_W, d)),
            resident((R_V_W, d)),
            resident((d, d)),
            resident((1, d)),
            resident((1, d)),
            resident((1, d)),
        ],
        out_specs=pl.BlockSpec((bm, d), lambda i: (i, 0)),
        compiler_params=pltpu.CompilerParams(
            dimension_semantics=("arbitrary",), vmem_limit_bytes=52 * MIB),
        name="merge",
    )(hm, hr, u, u, x2d, wm, wr, wo, bo, ln_g, ln_b)


def _mlp_kernel(alpha, x_ref, w1_ref, b1_ref, w2_ref, b2_ref, lg_ref, lb_ref, out_ref, xb_ref):
    j = pl.program_id(1)

    @pl.when(j == 0)
    def _():
        xb_ref[...] = x_ref[...].astype(BF16)
        out_ref[...] = jnp.zeros_like(out_ref)

    hid = jnp.maximum(_dot(xb_ref[...], w1_ref[...]) + b1_ref[...], 0.0)
    out_ref[...] += _dot((hid * hid).astype(BF16), w2_ref[...])

    @pl.when(j == pl.num_programs(1) - 1)
    def _():
        z = alpha * x_ref[...] + (out_ref[...] + b2_ref[...])
        out_ref[...] = _layer_norm(z, lg_ref[...], lb_ref[...])


def _mlp(x2d, w1, b1, w2, b2, ln_g, ln_b, *, alpha, bm=512, bf=1024):
    m, d = x2d.shape
    f = w1.shape[1]
    return pl.pallas_call(
        functools.partial(_mlp_kernel, alpha),
        out_shape=jax.ShapeDtypeStruct((m, d), F32),
        grid=(m // bm, f // bf),
        in_specs=[
            pl.BlockSpec((bm, d), lambda i, j: (i, 0)),
            pl.BlockSpec((d, bf), lambda i, j: (0, j)),
            pl.BlockSpec((1, bf), lambda i, j: (0, j)),
            pl.BlockSpec((bf, d), lambda i, j: (j, 0)),
            pl.BlockSpec((1, d), lambda i, j: (0, 0)),
            pl.BlockSpec((1, d), lambda i, j: (0, 0)),
            pl.BlockSpec((1, d), lambda i, j: (0, 0)),
        ],
        out_specs=pl.BlockSpec((bm, d), lambda i, j: (i, 0)),
        scratch_shapes=[pltpu.VMEM((bm, d), BF16)],
        compiler_params=pltpu.CompilerParams(
            dimension_semantics=("arbitrary", "arbitrary"), vmem_limit_bytes=48 * MIB),
        name="mlp",
    )(x2d, w1, b1, w2, b2, ln_g, ln_b)


def kernel(x, positions, w_in, b_in, m_conv_w, m_conv_b, m_norm_g, r_norm_g, w_branch_m, w_branch_r,
           w_out, b_out, ln1_g, ln1_b, w_ff1, b_ff1, w_ff2, b_ff2, ln2_g, ln2_b):
    batch, seq, d = x.shape
    depth = w_in.shape[0]
    n_chunks = seq // CHUNK
    alpha = (2.0 * depth) ** 0.25
    gate_lo = 2 * M_QK_W + 2 * M_V_W
    gate_hi = gate_lo + 2 * M_HEADS

    half = R_DQK // 2
    inv_freq = ROPE_BASE ** (-jnp.arange(half, dtype=F32) / half)
    inv_freq = jnp.concatenate([inv_freq, inv_freq]).reshape(1, R_DQK)
    sign = jnp.concatenate([-jnp.ones((half,), F32), jnp.ones((half,), F32)]).reshape(1, R_DQK)
    pos_b = jnp.broadcast_to(positions.astype(F32).reshape(batch * seq, 1), (batch * seq, R_DQK))

    h = x.reshape(batch * seq, d)
    for l in range(depth):
        u, gates = _proj(h, w_in[l].T, b_in[l].reshape(1, -1), gate_lo=gate_lo, n_gate=2 * M_HEADS)
        hm = _mlstm(u, gates, m_conv_w[l], m_conv_b[l].reshape(1, -1), m_norm_g[l].reshape(1, -1),
                    batch=batch, n_chunks=n_chunks)
        hr = _retention(u, pos_b, inv_freq, sign, r_norm_g[l].reshape(1, -1),
                        batch=batch, n_chunks=n_chunks)
        h1 = _merge(hm, hr, u, h, w_branch_m[l].astype(BF16), w_branch_r[l].astype(BF16),
                    w_out[l].astype(BF16), b_out[l].reshape(1, -1), ln1_g[l].reshape(1, -1),
                    ln1_b[l].reshape(1, -1), alpha=alpha)
        h = _mlp(h1, w_ff1[l].astype(BF16), b_ff1[l].reshape(1, -1), w_ff2[l].astype(BF16),
                 b_ff2[l].reshape(1, -1), ln2_g[l].reshape(1, -1), ln2_b[l].reshape(1, -1), alpha=alpha)
    return h.reshape(batch, seq, d)
```

```python
import functools
import math

import jax
import jax.numpy as jnp
from jax import lax
from jax.experimental import pallas as pl
from jax.experimental.pallas import tpu as pltpu

M_HEADS, M_DQK, M_DV = 4, 256, 512
R_HEADS, R_DQK, R_DV = 8, 128, 256
CONV_K = 4
CHUNK = 128
ROPE_BASE = 10000.0
LN_EPS = 1e-5
M_QK_W = M_HEADS * M_DQK
M_V_W = M_HEADS * M_DV
R_QK_W = R_HEADS * R_DQK
R_V_W = R_HEADS * R_DV

LANES = 128
SUBLANES = 8
GATE_PAD = LANES
MIB = 1024 * 1024

F32 = jnp.float32
BF16 = jnp.bfloat16


def _sigmoid(x):
    return 1.0 / (1.0 + jnp.exp(-x))


def _log_sigmoid(x):
    return jnp.minimum(x, 0.0) - jnp.log1p(jnp.exp(-jnp.abs(x)))


def _dot(a, b):
    return jnp.dot(a, b, preferred_element_type=F32)


def _dot_nt(a, b):
    return lax.dot_general(a, b, (((1,), (1,)), ((), ())), preferred_element_type=F32)


def _layer_norm(z, g, b):
    mu = jnp.mean(z, axis=-1, keepdims=True)
    d = z - mu
    var = jnp.mean(d * d, axis=-1, keepdims=True)
    return d * lax.rsqrt(var + LN_EPS) * g + b


def _head_norm(h, g):
    mu = jnp.mean(h, axis=-1, keepdims=True)
    d = h - mu
    var = jnp.mean(d * d, axis=-1, keepdims=True)
    return d * lax.rsqrt(var + LN_EPS) * g


def _col_tile(step, gate_blk, n_tiles):
    return lax.rem(step + gate_blk, n_tiles)


def _proj_kernel(gate_blk, n_gate, x_ref, wa_ref, wb_ref, ba_ref, bb_ref, u_ref, g_ref, w_scr, b_scr):
    j = _col_tile(pl.program_id(0), gate_blk, pl.num_programs(0))
    bn, d = wa_ref.shape
    rows = 128

    @pl.when(pl.program_id(1) == 0)
    def _():
        @pl.when(j < gate_blk)
        def _():
            for r in range(0, bn, rows):
                w_scr[r:r + rows, :] = wa_ref[r:r + rows, :].astype(BF16)
            b_scr[...] = ba_ref[...]

        @pl.when(j >= gate_blk)
        def _():
            for r in range(0, bn, rows):
                lo, hi = r + n_gate, r + n_gate + rows
                if hi <= bn:
                    blk = wa_ref[lo:hi, :]
                else:
                    blk = jnp.concatenate([wa_ref[lo:bn, :], wb_ref[0:hi - bn, :]], axis=0)
                w_scr[r:r + rows, :] = blk.astype(BF16)
            b_scr[...] = jnp.concatenate([ba_ref[...], bb_ref[...]], axis=1)[:, n_gate:n_gate + bn]

    xb = x_ref[...].astype(BF16)
    u_ref[...] = (_dot_nt(xb, w_scr[...]) + b_scr[...]).astype(u_ref.dtype)

    @pl.when(j == gate_blk)
    def _():
        g_ref[...] = _dot_nt(xb, wa_ref[0:GATE_PAD, :].astype(BF16)) + ba_ref[:, 0:GATE_PAD]

    @pl.when(j != gate_blk)
    def _():
        g_ref[...] = jnp.zeros_like(g_ref)


def _proj(x2d, w_t, b, *, gate_lo, n_gate, bm=1024, bn=1024):
    m, d = x2d.shape
    n = w_t.shape[0] - n_gate
    assert gate_lo % bn == 0 and n % bn == 0 and m % bm == 0
    assert n_gate % SUBLANES == 0 and n_gate <= LANES
    gate_blk = gate_lo // bn
    n_i = m // bm
    n_j = n // bn
    col = lambda s: _col_tile(s, gate_blk, n_j)
    return pl.pallas_call(
        functools.partial(_proj_kernel, gate_blk, n_gate),
        out_shape=(jax.ShapeDtypeStruct((m, n), BF16),
                   jax.ShapeDtypeStruct(((n_i + 1) * bm, GATE_PAD), F32)),
        grid=(n_j, n_i),
        in_specs=[
            pl.BlockSpec((bm, d), lambda s, i: (i, 0)),
            pl.BlockSpec((bn, d), lambda s, i: (col(s), 0)),
            pl.BlockSpec((n_gate, d), lambda s, i: ((col(s) + 1) * (bn // n_gate), 0)),
            pl.BlockSpec((1, bn), lambda s, i: (0, col(s))),
            pl.BlockSpec((1, LANES), lambda s, i: (0, (col(s) + 1) * (bn // LANES))),
        ],
        out_specs=(
            pl.BlockSpec((bm, bn), lambda s, i: (i, col(s))),
            pl.BlockSpec((bm, GATE_PAD), lambda s, i: (jnp.where(s == 0, i, n_i), 0)),
        ),
        scratch_shapes=[pltpu.VMEM((bn, d), BF16), pltpu.VMEM((1, bn), F32)],
        compiler_params=pltpu.CompilerParams(
            dimension_semantics=("arbitrary", "arbitrary"), vmem_limit_bytes=48 * MIB),
        name="proj",
    )(x2d, w_t, w_t, b, b)


def _mlstm_chunk(q_ref, k_ref, v_ref, o_ref, g_ref, cw_ref, cb_ref, ng_ref, out_ref,
                 c_ref, n_ref, m_ref, xbuf_ref):
    L = CHUNK
    hist = SUBLANES

    xbuf_ref[hist:hist + L, 0:M_QK_W] = q_ref[...].astype(F32)
    xbuf_ref[hist:hist + L, M_QK_W:2 * M_QK_W] = k_ref[...].astype(F32)
    conv = cb_ref[...]
    for j in range(CONV_K):
        start = hist - (CONV_K - 1) + j
        conv = conv + cw_ref[j:j + 1, :] * xbuf_ref[start:start + L, :]
    xbuf_ref[0:hist, :] = xbuf_ref[L:L + hist, :]
    qk = conv * _sigmoid(conv)

    g = g_ref[...]
    lfg = _log_sigmoid(g)
    g_rows = g.T[0:2 * M_HEADS, :]
    lf_rows = _log_sigmoid(g_rows)

    row_t = lax.broadcasted_iota(jnp.int32, (L, L), 0)
    col_s = lax.broadcasted_iota(jnp.int32, (L, L), 1)
    causal = col_s <= row_t
    lane = lax.broadcasted_iota(jnp.int32, (L, GATE_PAD), 1)

    for h in range(M_HEADS):
        li_row = g_rows[h:h + 1, :]
        lf_row = lf_rows[M_HEADS + h:M_HEADS + h + 1, :]
        li_col = jnp.sum(jnp.where(lane == h, g, 0.0), axis=1, keepdims=True)
        lf_col = jnp.sum(jnp.where(lane == M_HEADS + h, lfg, 0.0), axis=1, keepdims=True)
        b_col = jnp.sum(jnp.where(causal, lf_row, 0.0), axis=1, keepdims=True)
        b_row = jnp.sum(jnp.where(row_t <= col_s, lf_col, 0.0), axis=0, keepdims=True)
        g_tot = jnp.sum(lf_row, axis=1, keepdims=True)
        m_prev = m_ref[h:h + 1, 0:1]

        d_log = jnp.where(causal, b_col - b_row + li_row, -jnp.inf)
        m_inter = b_col + m_prev
        m_t = jnp.maximum(m_inter, jnp.max(d_log, axis=1, keepdims=True))
        w_intra = jnp.exp(d_log - m_t)
        w_inter = jnp.exp(m_inter - m_t)

        w_log_row = g_tot - b_row + li_row
        w_log_col = g_tot - b_col + li_col
        m_new = jnp.maximum(g_tot + m_prev, jnp.max(w_log_row, axis=1, keepdims=True))
        w_s = jnp.exp(w_log_col - m_new)
        decay = jnp.exp(g_tot + m_prev - m_new)

        qh = qk[:, h * M_DQK:(h + 1) * M_DQK] * (M_DQK ** -0.5)
        kh = qk[:, M_QK_W + h * M_DQK:M_QK_W + (h + 1) * M_DQK]
        qb = qh.astype(BF16)
        vb = v_ref[:, h * M_DV:(h + 1) * M_DV]
        c_old = c_ref[h]
        n_old = n_ref[h:h + 1, :]

        s = _dot_nt(qb, kh.astype(BF16)) * w_intra
        num = _dot(s.astype(BF16), vb) + w_inter * _dot(qb, c_old.astype(BF16))
        den = (jnp.sum(s, axis=1, keepdims=True)
               + w_inter * jnp.sum(qh * n_old, axis=1, keepdims=True))
        hh = num * (1.0 / jnp.maximum(jnp.abs(den), jnp.exp(-m_t)))

        kw = kh * w_s
        c_ref[h] = decay * c_old + _dot(kw.T.astype(BF16), vb)
        n_ref[h:h + 1, :] = decay * n_old + jnp.sum(kw, axis=0, keepdims=True)
        m_ref[h:h + 1, :] = jnp.broadcast_to(m_new, (1, LANES))

        sl = slice(h * M_DV, (h + 1) * M_DV)
        y = _head_norm(hh, ng_ref[:, sl]) * _sigmoid(o_ref[:, sl].astype(F32))
        out_ref[:, sl] = y.astype(out_ref.dtype)


def _retention_chunk(q_ref, k_ref, v_ref, sg_ref, pos_ref, invf_ref, sign_ref, ng_ref, out_ref, r_ref):
    L = CHUNK
    ang = pos_ref[...] * invf_ref[...]
    cos = jnp.cos(ang)
    sin = jnp.sin(ang) * sign_ref[...]

    t_col = lax.broadcasted_iota(jnp.int32, (L, 1), 0).astype(F32)
    rel = (lax.broadcasted_iota(jnp.int32, (L, L), 0)
           - lax.broadcasted_iota(jnp.int32, (L, L), 1)).astype(F32)
    scale = R_DQK ** -0.5

    for h in range(R_HEADS):
        log_gamma = math.log1p(-(2.0 ** (-5.0 - h)))
        decay = jnp.where(rel >= 0.0, jnp.exp(jnp.maximum(rel, 0.0) * log_gamma), 0.0) * scale
        xi = jnp.exp((t_col + 1.0) * log_gamma) * scale
        zeta = jnp.exp((L - 1.0 - t_col) * log_gamma)
        g_chunk = math.exp(L * log_gamma)

        qs = slice(h * R_DQK, (h + 1) * R_DQK)
        vs = slice(h * R_DV, (h + 1) * R_DV)
        xq = q_ref[:, qs].astype(F32)
        xk = k_ref[:, qs].astype(F32)
        qr = xq * cos + pltpu.roll(xq, R_DQK // 2, 1) * sin
        kr = xk * cos + pltpu.roll(xk, R_DQK // 2, 1) * sin
        qb = qr.astype(BF16)
        vb = v_ref[:, vs]
        r_old = r_ref[h]

        p = _dot_nt(qb, kr.astype(BF16)) * decay
        o = _dot(p.astype(BF16), vb) + _dot(qb, r_old.astype(BF16)) * xi
        r_ref[h] = g_chunk * r_old + _dot((kr * zeta).T.astype(BF16), vb)

        sg = sg_ref[:, vs].astype(F32)
        y = _head_norm(o, ng_ref[:, vs]) * (sg * _sigmoid(sg))
        out_ref[:, vs] = y.astype(out_ref.dtype)


def _mixer_kernel(alpha, n_chunks,
                  mq_ref, mk_ref, mv_ref, mo_ref, g_ref, cw_ref, cb_ref, mng_ref,
                  rq_ref, rk_ref, rv_ref, rg_ref, pos_ref, invf_ref, sign_ref, rng_ref,
                  ga_ref, gb_ref, x_ref, wm_ref, wr_ref, wo_ref, bo_ref, lg_ref, lb_ref,
                  out_ref,
                  c_ref, n_ref, m_ref, xbuf_ref, r_ref, hm_ref, hr_ref):
    t = pl.program_id(0)

    @pl.when(t == 0)
    def _():
        hm_ref[...] = jnp.zeros_like(hm_ref)
        hr_ref[...] = jnp.zeros_like(hr_ref)

    @pl.when(lax.rem(t, n_chunks) == 0)
    def _():
        c_ref[...] = jnp.zeros_like(c_ref)
        n_ref[...] = jnp.zeros_like(n_ref)
        m_ref[...] = jnp.zeros_like(m_ref)
        r_ref[...] = jnp.zeros_like(r_ref)
        xbuf_ref[0:SUBLANES, :] = jnp.zeros((SUBLANES, 2 * M_QK_W), F32)

    ya = _dot(hm_ref[...], wm_ref[...])
    yb = _dot(hr_ref[...], wr_ref[...])
    merged = _sigmoid(ga_ref[...].astype(F32)) * ya + _sigmoid(gb_ref[...].astype(F32)) * yb
    y = _dot(merged.astype(BF16), wo_ref[...]) + bo_ref[...]
    out_ref[...] = _layer_norm(alpha * x_ref[...] + y, lg_ref[...], lb_ref[...])

    _mlstm_chunk(mq_ref, mk_ref, mv_ref, mo_ref, g_ref, cw_ref, cb_ref, mng_ref, hm_ref,
                 c_ref, n_ref, m_ref, xbuf_ref)
    _retention_chunk(rq_ref, rk_ref, rv_ref, rg_ref, pos_ref, invf_ref, sign_ref, rng_ref, hr_ref, r_ref)


def _mixer(u, gates, pos_b, x2d, conv_w, conv_b, m_norm_g, inv_freq, sign, r_norm_g,
           wm, wr, wo, bo, ln_g, ln_b, *, alpha, batch, n_chunks):
    L = CHUNK
    m, d = x2d.shape
    total = batch * n_chunks
    rec = lambda t: jnp.minimum(t, total - 1)
    prj = lambda t: jnp.maximum(t - 1, 0)
    r_q = (2 * M_QK_W + 2 * M_V_W) // R_QK_W
    r_v = (2 * M_QK_W + 2 * M_V_W + 2 * R_QK_W) // R_V_W
    g_a = (2 * M_QK_W + 2 * M_V_W + 2 * R_QK_W + 2 * R_V_W) // d
    const = lambda shape: pl.BlockSpec(shape, lambda t: (0, 0))
    resident = lambda shape: pl.BlockSpec(shape, lambda t: (0, 0), pipeline_mode=pl.Buffered(1))
    return pl.pallas_call(
        functools.partial(_mixer_kernel, alpha, n_chunks),
        out_shape=jax.ShapeDtypeStruct((m, d), F32),
        grid=(total + 1,),
        in_specs=[
            pl.BlockSpec((L, M_QK_W), lambda t: (rec(t), 0)),
            pl.BlockSpec((L, M_QK_W), lambda t: (rec(t), 1)),
            pl.BlockSpec((L, M_V_W), lambda t: (rec(t), 1)),
            pl.BlockSpec((L, M_V_W), lambda t: (rec(t), 2)),
            pl.BlockSpec((L, GATE_PAD), lambda t: (rec(t), 0)),
            const((CONV_K, 2 * M_QK_W)),
            const((1, 2 * M_QK_W)),
            const((1, M_V_W)),
            pl.BlockSpec((L, R_QK_W), lambda t: (rec(t), r_q)),
            pl.BlockSpec((L, R_QK_W), lambda t: (rec(t), r_q + 1)),
            pl.BlockSpec((L, R_V_W), lambda t: (rec(t), r_v)),
            pl.BlockSpec((L, R_V_W), lambda t: (rec(t), r_v + 1)),
            pl.BlockSpec((L, R_DQK), lambda t: (rec(t), 0)),
            const((1, R_DQK)),
            const((1, R_DQK)),
            const((1, R_V_W)),
            pl.BlockSpec((L, d), lambda t: (prj(t), g_a)),
            pl.BlockSpec((L, d), lambda t: (prj(t), g_a + 1)),
            pl.BlockSpec((L, d), lambda t: (prj(t), 0)),
            resident((M_V_W, d)),
            resident((R_V_W, d)),
            resident((d, d)),
            const((1, d)),
            const((1, d)),
            const((1, d)),
        ],
        out_specs=pl.BlockSpec((L, d), lambda t: (prj(t), 0)),
        scratch_shapes=[
            pltpu.VMEM((M_HEADS, M_DQK, M_DV), F32),
            pltpu.VMEM((SUBLANES, M_DQK), F32),
            pltpu.VMEM((SUBLANES, LANES), F32),
            pltpu.VMEM((SUBLANES + L, 2 * M_QK_W), F32),
            pltpu.VMEM((R_HEADS, R_DQK, R_DV), F32),
            pltpu.VMEM((L, M_V_W), BF16),
            pltpu.VMEM((L, R_V_W), BF16),
        ],
        compiler_params=pltpu.CompilerParams(
            dimension_semantics=("arbitrary",), vmem_limit_bytes=56 * MIB),
        name="mixer",
    )(u, u, u, u, gates, conv_w, conv_b, m_norm_g,
      u, u, u, u, pos_b, inv_freq, sign, r_norm_g,
      u, u, x2d, wm, wr, wo, bo, ln_g, ln_b)


def _mlp_kernel(alpha, x_ref, w1_ref, b1_ref, w2_ref, b2_ref, lg_ref, lb_ref, out_ref, xb_ref):
    j = pl.program_id(1)

    @pl.when(j == 0)
    def _():
        xb_ref[...] = x_ref[...].astype(BF16)
        out_ref[...] = jnp.zeros_like(out_ref)

    hid = jnp.maximum(_dot(xb_ref[...], w1_ref[...]) + b1_ref[...], 0.0)
    out_ref[...] += _dot((hid * hid).astype(BF16), w2_ref[...])

    @pl.when(j == pl.num_programs(1) - 1)
    def _():
        z = alpha * x_ref[...] + (out_ref[...] + b2_ref[...])
        out_ref[...] = _layer_norm(z, lg_ref[...], lb_ref[...])


def _mlp(x2d, w1, b1, w2, b2, ln_g, ln_b, *, alpha, bm=512, bf=1024):
    m, d = x2d.shape
    f = w1.shape[1]
    return pl.pallas_call(
        functools.partial(_mlp_kernel, alpha),
        out_shape=jax.ShapeDtypeStruct((m, d), F32),
        grid=(m // bm, f // bf),
        in_specs=[
            pl.BlockSpec((bm, d), lambda i, j: (i, 0)),
            pl.BlockSpec((d, bf), lambda i, j: (0, j)),
            pl.BlockSpec((1, bf), lambda i, j: (0, j)),
            pl.BlockSpec((bf, d), lambda i, j: (j, 0)),
            pl.BlockSpec((1, d), lambda i, j: (0, 0)),
            pl.BlockSpec((1, d), lambda i, j: (0, 0)),
            pl.BlockSpec((1, d), lambda i, j: (0, 0)),
        ],
        out_specs=pl.BlockSpec((bm, d), lambda i, j: (i, 0)),
        scratch_shapes=[pltpu.VMEM((bm, d), BF16)],
        compiler_params=pltpu.CompilerParams(
            dimension_semantics=("arbitrary", "arbitrary"), vmem_limit_bytes=48 * MIB),
        name="mlp",
    )(x2d, w1, b1, w2, b2, ln_g, ln_b)


def kernel(x, positions, w_in, b_in, m_conv_w, m_conv_b, m_norm_g, r_norm_g, w_branch_m, w_branch_r,
           w_out, b_out, ln1_g, ln1_b, w_ff1, b_ff1, w_ff2, b_ff2, ln2_g, ln2_b):
    batch, seq, d = x.shape
    depth = w_in.shape[0]
    n_chunks = seq // CHUNK
    alpha = (2.0 * depth) ** 0.25
    gate_lo = 2 * M_QK_W + 2 * M_V_W
    gate_hi = gate_lo + 2 * M_HEADS

    half = R_DQK // 2
    inv_freq = ROPE_BASE ** (-jnp.arange(half, dtype=F32) / half)
    inv_freq = jnp.concatenate([inv_freq, inv_freq]).reshape(1, R_DQK)
    sign = jnp.concatenate([-jnp.ones((half,), F32), jnp.ones((half,), F32)]).reshape(1, R_DQK)
    pos_b = jnp.broadcast_to(positions.astype(F32).reshape(batch * seq, 1), (batch * seq, R_DQK))

    h = x.reshape(batch * seq, d)
    for l in range(depth):
        u, gates = _proj(h, w_in[l].T, b_in[l].reshape(1, -1), gate_lo=gate_lo, n_gate=2 * M_HEADS)
        h1 = _mixer(u, gates, pos_b, h, m_conv_w[l], m_conv_b[l].reshape(1, -1),
                    m_norm_g[l].reshape(1, -1), inv_freq, sign, r_norm_g[l].reshape(1, -1),
                    w_branch_m[l].astype(BF16), w_branch_r[l].astype(BF16), w_out[l].astype(BF16),
                    b_out[l].reshape(1, -1), ln1_g[l].reshape(1, -1), ln1_b[l].reshape(1, -1),
                    alpha=alpha, batch=batch, n_chunks=n_chunks)
        h = _mlp(h1, w_ff1[l].astype(BF16), b_ff1[l].reshape(1, -1), w_ff2[l].astype(BF16),
                 b_ff2[l].reshape(1, -1), ln2_g[l].reshape(1, -1), ln2_b[l].reshape(1, -1), alpha=alpha)
    return h.reshape(batch, seq, d)
```

```python
import functools
import math

import jax
import jax.numpy as jnp
from jax import lax
from jax.experimental import pallas as pl
from jax.experimental.pallas import tpu as pltpu

M_HEADS, M_DQK, M_DV = 4, 256, 512
R_HEADS, R_DQK, R_DV = 8, 128, 256
CONV_K = 4
CHUNK = 128
ROPE_BASE = 10000.0
LN_EPS = 1e-5
M_QK_W = M_HEADS * M_DQK
M_V_W = M_HEADS * M_DV
R_QK_W = R_HEADS * R_DQK
R_V_W = R_HEADS * R_DV

LANES = 128
SUBLANES = 8
GATE_PAD = LANES
MIB = 1024 * 1024

F32 = jnp.float32
BF16 = jnp.bfloat16

def _sigmoid(x):
    return 1.0 / (1.0 + jnp.exp(-x))


def _log_sigmoid(x):
    return jnp.minimum(x, 0.0) - jnp.log1p(jnp.exp(-jnp.abs(x)))


def _dot(a, b):
    return jnp.dot(a, b, preferred_element_type=F32)


def _dot_nt(a, b):
    return lax.dot_general(a, b, (((1,), (1,)), ((), ())), preferred_element_type=F32)


def _layer_norm(z, g, b):
    mu = jnp.mean(z, axis=-1, keepdims=True)
    d = z - mu
    var = jnp.mean(d * d, axis=-1, keepdims=True)
    return d * lax.rsqrt(var + LN_EPS) * g + b


def _head_norm(h, g):
    mu = jnp.mean(h, axis=-1, keepdims=True)
    d = h - mu
    var = jnp.mean(d * d, axis=-1, keepdims=True)
    return d * lax.rsqrt(var + LN_EPS) * g


def _col_tile(step, gate_blk, n_tiles):
    return lax.rem(step + gate_blk, n_tiles)


def _proj_kernel(gate_blk, n_gate, x_ref, wa_ref, wb_ref, ba_ref, bb_ref, u_ref, g_ref, w_scr, b_scr):
    j = _col_tile(pl.program_id(0), gate_blk, pl.num_programs(0))
    bn, d = wa_ref.shape
    rows = 128

    @pl.when(pl.program_id(1) == 0)
    def _():
        @pl.when(j < gate_blk)
        def _():
            for r in range(0, bn, rows):
                w_scr[r:r + rows, :] = wa_ref[r:r + rows, :].astype(BF16)
            b_scr[...] = ba_ref[...]

        @pl.when(j >= gate_blk)
        def _():
            for r in range(0, bn, rows):
                lo, hi = r + n_gate, r + n_gate + rows
                if hi <= bn:
                    blk = wa_ref[lo:hi, :]
                else:
                    blk = jnp.concatenate([wa_ref[lo:bn, :], wb_ref[0:hi - bn, :]], axis=0)
                w_scr[r:r + rows, :] = blk.astype(BF16)
            b_scr[...] = jnp.concatenate([ba_ref[...], bb_ref[...]], axis=1)[:, n_gate:n_gate + bn]

    xb = x_ref[...].astype(BF16)
    u_ref[...] = (_dot_nt(xb, w_scr[...]) + b_scr[...]).astype(u_ref.dtype)

    @pl.when(j == gate_blk)
    def _():
        g_ref[...] = _dot_nt(xb, wa_ref[0:GATE_PAD, :].astype(BF16)) + ba_ref[:, 0:GATE_PAD]

    @pl.when(j != gate_blk)
    def _():
        g_ref[...] = jnp.zeros_like(g_ref)


def _proj(x2d, w_t, b, *, gate_lo, n_gate, bm=1024, bn=1024):
    m, d = x2d.shape
    n = w_t.shape[0] - n_gate
    assert gate_lo % bn == 0 and n % bn == 0 and m % bm == 0
    assert n_gate % SUBLANES == 0 and n_gate <= LANES
    gate_blk = gate_lo // bn
    n_i = m // bm
    n_j = n // bn
    col = lambda s: _col_tile(s, gate_blk, n_j)
    return pl.pallas_call(
        functools.partial(_proj_kernel, gate_blk, n_gate),
        out_shape=(jax.ShapeDtypeStruct((m, n), BF16),
                   jax.ShapeDtypeStruct(((n_i + 1) * bm, GATE_PAD), F32)),
        grid=(n_j, n_i),
        in_specs=[
            pl.BlockSpec((bm, d), lambda s, i: (i, 0)),
            pl.BlockSpec((bn, d), lambda s, i: (col(s), 0)),
            pl.BlockSpec((n_gate, d), lambda s, i: ((col(s) + 1) * (bn // n_gate), 0)),
            pl.BlockSpec((1, bn), lambda s, i: (0, col(s))),
            pl.BlockSpec((1, LANES), lambda s, i: (0, (col(s) + 1) * (bn // LANES))),
        ],
        out_specs=(
            pl.BlockSpec((bm, bn), lambda s, i: (i, col(s))),
            pl.BlockSpec((bm, GATE_PAD), lambda s, i: (jnp.where(s == 0, i, n_i), 0)),
        ),
        scratch_shapes=[pltpu.VMEM((bn, d), BF16), pltpu.VMEM((1, bn), F32)],
        compiler_params=pltpu.CompilerParams(
            dimension_semantics=("arbitrary", "arbitrary"), vmem_limit_bytes=48 * MIB),
        name="proj",
    )(x2d, w_t, w_t, b, b)


def _mlstm_chunk(q_ref, k_ref, v_ref, o_ref, g_ref, cw_ref, cb_ref, ng_ref, out_ref,
                 c_ref, n_ref, m_ref, xbuf_ref):
    L = CHUNK
    hist = SUBLANES

    xbuf_ref[hist:hist + L, 0:M_QK_W] = q_ref[...].astype(F32)
    xbuf_ref[hist:hist + L, M_QK_W:2 * M_QK_W] = k_ref[...].astype(F32)
    conv = cb_ref[...]
    for j in range(CONV_K):
        start = hist - (CONV_K - 1) + j
        conv = conv + cw_ref[j:j + 1, :] * xbuf_ref[start:start + L, :]
    xbuf_ref[0:hist, :] = xbuf_ref[L:L + hist, :]
    qk = conv * _sigmoid(conv)

    g = g_ref[...]
    lfg = _log_sigmoid(g)
    g_rows = g.T[0:2 * M_HEADS, :]
    lf_rows = _log_sigmoid(g_rows)

    row_t = lax.broadcasted_iota(jnp.int32, (L, L), 0)
    col_s = lax.broadcasted_iota(jnp.int32, (L, L), 1)
    causal = col_s <= row_t
    lane = lax.broadcasted_iota(jnp.int32, (L, GATE_PAD), 1)

    for h in range(M_HEADS):
        li_row = g_rows[h:h + 1, :]
        lf_row = lf_rows[M_HEADS + h:M_HEADS + h + 1, :]
        li_col = jnp.sum(jnp.where(lane == h, g, 0.0), axis=1, keepdims=True)
        lf_col = jnp.sum(jnp.where(lane == M_HEADS + h, lfg, 0.0), axis=1, keepdims=True)
        b_col = jnp.sum(jnp.where(causal, lf_row, 0.0), axis=1, keepdims=True)
        b_row = jnp.sum(jnp.where(row_t <= col_s, lf_col, 0.0), axis=0, keepdims=True)
        g_tot = jnp.sum(lf_row, axis=1, keepdims=True)
        m_prev = m_ref[h:h + 1, 0:1]

        d_log = jnp.where(causal, b_col - b_row + li_row, -jnp.inf)
        m_inter = b_col + m_prev
        m_t = jnp.maximum(m_inter, jnp.max(d_log, axis=1, keepdims=True))
        w_intra = jnp.exp(d_log - m_t)
        w_inter = jnp.exp(m_inter - m_t)

        w_log_row = g_tot - b_row + li_row
        w_log_col = g_tot - b_col + li_col
        m_new = jnp.maximum(g_tot + m_prev, jnp.max(w_log_row, axis=1, keepdims=True))
        w_s = jnp.exp(w_log_col - m_new)
        decay = jnp.exp(g_tot + m_prev - m_new)

        qh = qk[:, h * M_DQK:(h + 1) * M_DQK] * (M_DQK ** -0.5)
        kh = qk[:, M_QK_W + h * M_DQK:M_QK_W + (h + 1) * M_DQK]
        qb = qh.astype(BF16)
        vb = v_ref[:, h * M_DV:(h + 1) * M_DV]
        c_old = c_ref[h]
        n_old = n_ref[h:h + 1, :]

        s = _dot_nt(qb, kh.astype(BF16)) * w_intra
        num = _dot(s.astype(BF16), vb) + w_inter * _dot(qb, c_old.astype(BF16))
        den = (jnp.sum(s, axis=1, keepdims=True)
               + w_inter * jnp.sum(qh * n_old, axis=1, keepdims=True))
        hh = num * (1.0 / jnp.maximum(jnp.abs(den), jnp.exp(-m_t)))

        kw = kh * w_s
        c_ref[h] = decay * c_old + _dot(kw.T.astype(BF16), vb)
        n_ref[h:h + 1, :] = decay * n_old + jnp.sum(kw, axis=0, keepdims=True)
        m_ref[h:h + 1, :] = jnp.broadcast_to(m_new, (1, LANES))

        sl = slice(h * M_DV, (h + 1) * M_DV)
        y = _head_norm(hh, ng_ref[:, sl]) * _sigmoid(o_ref[:, sl].astype(F32))
        out_ref[:, sl] = y.astype(out_ref.dtype)


def _retention_chunk(q_ref, k_ref, v_ref, sg_ref, pos_ref, invf_ref, sign_ref, ng_ref, out_ref, r_ref):
    L = CHUNK
    half = R_DQK // 2
    ang = pos_ref[...] * invf_ref[...]
    lane = lax.broadcasted_iota(jnp.int32, (L // 2, R_DQK), 1)

    def unpack(packed):
        swapped = pltpu.roll(packed, half, 1)
        top = jnp.where(lane < half, packed, swapped)
        bottom = jnp.where(lane < half, swapped, packed)
        return jnp.concatenate([top, bottom], axis=0)

    cos = unpack(jnp.cos(ang))
    sin = unpack(jnp.sin(ang)) * sign_ref[...]

    t_col = lax.broadcasted_iota(jnp.int32, (L, 1), 0).astype(F32)
    rel = (lax.broadcasted_iota(jnp.int32, (L, L), 0)
           - lax.broadcasted_iota(jnp.int32, (L, L), 1)).astype(F32)
    scale = R_DQK ** -0.5

    for h in range(R_HEADS):
        log_gamma = math.log1p(-(2.0 ** (-5.0 - h)))
        decay = jnp.where(rel >= 0.0, jnp.exp(jnp.maximum(rel, 0.0) * log_gamma), 0.0) * scale
        xi = jnp.exp((t_col + 1.0) * log_gamma) * scale
        zeta = jnp.exp((L - 1.0 - t_col) * log_gamma)
        g_chunk = math.exp(L * log_gamma)

        qs = slice(h * R_DQK, (h + 1) * R_DQK)
        vs = slice(h * R_DV, (h + 1) * R_DV)
        xq = q_ref[:, qs].astype(F32)
        xk = k_ref[:, qs].astype(F32)
        qr = xq * cos + pltpu.roll(xq, half, 1) * sin
        kr = xk * cos + pltpu.roll(xk, half, 1) * sin
        qb = qr.astype(BF16)
        vb = v_ref[:, vs]
        r_old = r_ref[h]

        p = _dot_nt(qb, kr.astype(BF16)) * decay
        o = _dot(p.astype(BF16), vb) + _dot(qb, r_old.astype(BF16)) * xi
        r_ref[h] = g_chunk * r_old + _dot((kr * zeta).T.astype(BF16), vb)

        sg = sg_ref[:, vs].astype(F32)
        y = _head_norm(o, ng_ref[:, vs]) * (sg * _sigmoid(sg))
        out_ref[:, vs] = y.astype(out_ref.dtype)


def _load_weight_bf16(w_hbm, w_scr, stage, sem):
    n_slots, rows, _ = stage.shape
    n = w_hbm.shape[0] // rows

    def copy(c, slot):
        src = w_hbm.at[pl.ds(pl.multiple_of(c * rows, rows), rows), :]
        return pltpu.make_async_copy(src, stage.at[slot], sem.at[slot])

    copy(0, 0).start()

    def body(c, carry):
        slot = lax.rem(c, n_slots)

        @pl.when(c + 1 < n)
        def _():
            copy(c + 1, lax.rem(c + 1, n_slots)).start()

        copy(c, slot).wait()
        w_scr[pl.ds(pl.multiple_of(c * rows, rows), rows), :] = stage[slot].astype(BF16)
        return carry

    lax.fori_loop(0, n, body, 0)


def _mixer_kernel(alpha, n_chunks,
                  mq_ref, mk_ref, mv_ref, mo_ref, g_ref, cw_ref, cb_ref, mng_ref,
                  rq_ref, rk_ref, rv_ref, rg_ref, pos_ref, invf_ref, sign_ref, rng_ref,
                  ga_ref, gb_ref, x_ref, wm_hbm, wr_hbm, wo_hbm, bo_ref, lg_ref, lb_ref,
                  out_ref,
                  c_ref, n_ref, m_ref, xbuf_ref, r_ref, hm_ref, hr_ref,
                  wm_ref, wr_ref, wo_ref, stage_ref, sem_ref):
    t = pl.program_id(0)

    @pl.when(t == 0)
    def _():
        hm_ref[...] = jnp.zeros_like(hm_ref)
        hr_ref[...] = jnp.zeros_like(hr_ref)
        _load_weight_bf16(wm_hbm, wm_ref, stage_ref, sem_ref)
        _load_weight_bf16(wr_hbm, wr_ref, stage_ref, sem_ref)
        _load_weight_bf16(wo_hbm, wo_ref, stage_ref, sem_ref)

    @pl.when(lax.rem(t, n_chunks) == 0)
    def _():
        c_ref[...] = jnp.zeros_like(c_ref)
        n_ref[...] = jnp.zeros_like(n_ref)
        m_ref[...] = jnp.zeros_like(m_ref)
        r_ref[...] = jnp.zeros_like(r_ref)
        xbuf_ref[0:SUBLANES, :] = jnp.zeros((SUBLANES, 2 * M_QK_W), F32)

    ya = _dot(hm_ref[...], wm_ref[...])
    yb = _dot(hr_ref[...], wr_ref[...])
    merged = _sigmoid(ga_ref[...].astype(F32)) * ya + _sigmoid(gb_ref[...].astype(F32)) * yb
    y = _dot(merged.astype(BF16), wo_ref[...]) + bo_ref[...]
    out_ref[...] = _layer_norm(alpha * x_ref[...] + y, lg_ref[...], lb_ref[...])

    _mlstm_chunk(mq_ref, mk_ref, mv_ref, mo_ref, g_ref, cw_ref, cb_ref, mng_ref, hm_ref,
                 c_ref, n_ref, m_ref, xbuf_ref)
    _retention_chunk(rq_ref, rk_ref, rv_ref, rg_ref, pos_ref, invf_ref, sign_ref, rng_ref, hr_ref, r_ref)


def _mixer(u, gates, pos_p, x2d, conv_w, conv_b, m_norm_g, inv_freq, sign, r_norm_g,
           wm, wr, wo, bo, ln_g, ln_b, *, alpha, batch, n_chunks):
    L = CHUNK
    m, d = x2d.shape
    total = batch * n_chunks
    rec = lambda t: jnp.minimum(t, total - 1)
    prj = lambda t: jnp.maximum(t - 1, 0)
    r_q = (2 * M_QK_W + 2 * M_V_W) // R_QK_W
    r_v = (2 * M_QK_W + 2 * M_V_W + 2 * R_QK_W) // R_V_W
    g_a = (2 * M_QK_W + 2 * M_V_W + 2 * R_QK_W + 2 * R_V_W) // d
    const = lambda shape: pl.BlockSpec(shape, lambda t: (0, 0))
    in_hbm = pl.BlockSpec(memory_space=pl.ANY)
    stage_rows = 256
    assert wm.shape == (M_V_W, d) and wr.shape == (R_V_W, d) and wo.shape == (d, d)
    assert M_V_W % stage_rows == 0 and R_V_W % stage_rows == 0 and d % stage_rows == 0
    return pl.pallas_call(
        functools.partial(_mixer_kernel, alpha, n_chunks),
        out_shape=jax.ShapeDtypeStruct((m, d), F32),
        grid=(total + 1,),
        in_specs=[
            pl.BlockSpec((L, M_QK_W), lambda t: (rec(t), 0)),
            pl.BlockSpec((L, M_QK_W), lambda t: (rec(t), 1)),
            pl.BlockSpec((L, M_V_W), lambda t: (rec(t), 1)),
            pl.BlockSpec((L, M_V_W), lambda t: (rec(t), 2)),
            pl.BlockSpec((L, GATE_PAD), lambda t: (rec(t), 0)),
            const((CONV_K, 2 * M_QK_W)),
            const((1, 2 * M_QK_W)),
            const((1, M_V_W)),
            pl.BlockSpec((L, R_QK_W), lambda t: (rec(t), r_q)),
            pl.BlockSpec((L, R_QK_W), lambda t: (rec(t), r_q + 1)),
            pl.BlockSpec((L, R_V_W), lambda t: (rec(t), r_v)),
            pl.BlockSpec((L, R_V_W), lambda t: (rec(t), r_v + 1)),
            pl.BlockSpec((L // 2, R_DQK), lambda t: (rec(t), 0)),
            const((1, R_DQK)),
            const((1, R_DQK)),
            const((1, R_V_W)),
            pl.BlockSpec((L, d), lambda t: (prj(t), g_a)),
            pl.BlockSpec((L, d), lambda t: (prj(t), g_a + 1)),
            pl.BlockSpec((L, d), lambda t: (prj(t), 0)),
            in_hbm,
            in_hbm,
            in_hbm,
            const((1, d)),
            const((1, d)),
            const((1, d)),
        ],
        out_specs=pl.BlockSpec((L, d), lambda t: (prj(t), 0)),
        scratch_shapes=[
            pltpu.VMEM((M_HEADS, M_DQK, M_DV), F32),
            pltpu.VMEM((SUBLANES, M_DQK), F32),
            pltpu.VMEM((SUBLANES, LANES), F32),
            pltpu.VMEM((SUBLANES + L, 2 * M_QK_W), F32),
            pltpu.VMEM((R_HEADS, R_DQK, R_DV), F32),
            pltpu.VMEM((L, M_V_W), BF16),
            pltpu.VMEM((L, R_V_W), BF16),
            pltpu.VMEM((M_V_W, d), BF16),
            pltpu.VMEM((R_V_W, d), BF16),
            pltpu.VMEM((d, d), BF16),
            pltpu.VMEM((2, stage_rows, d), F32),
            pltpu.SemaphoreType.DMA((2,)),
        ],
        compiler_params=pltpu.CompilerParams(
            dimension_semantics=("arbitrary",), vmem_limit_bytes=56 * MIB),
        name="mixer",
    )(u, u, u, u, gates, conv_w, conv_b, m_norm_g,
      u, u, u, u, pos_p, inv_freq, sign, r_norm_g,
      u, u, x2d, wm, wr, wo, bo, ln_g, ln_b)


def _mlp_kernel(alpha, x_ref, w1_ref, b1_ref, w2_ref, b2_ref, lg_ref, lb_ref, out_ref, xb_ref):
    j = pl.program_id(1)

    @pl.when(j == 0)
    def _():
        xb_ref[...] = x_ref[...].astype(BF16)
        out_ref[...] = jnp.zeros_like(out_ref)

    hid = jnp.maximum(_dot(xb_ref[...], w1_ref[...]) + b1_ref[...], 0.0)
    out_ref[...] += _dot((hid * hid).astype(BF16), w2_ref[...])

    @pl.when(j == pl.num_programs(1) - 1)
    def _():
        z = alpha * x_ref[...] + (out_ref[...] + b2_ref[...])
        out_ref[...] = _layer_norm(z, lg_ref[...], lb_ref[...])


def _mlp(x2d, w1, b1, w2, b2, ln_g, ln_b, *, alpha, bm=512, bf=1024):
    m, d = x2d.shape
    f = w1.shape[1]
    return pl.pallas_call(
        functools.partial(_mlp_kernel, alpha),
        out_shape=jax.ShapeDtypeStruct((m, d), F32),
        grid=(m // bm, f // bf),
        in_specs=[
            pl.BlockSpec((bm, d), lambda i, j: (i, 0)),
            pl.BlockSpec((d, bf), lambda i, j: (0, j)),
            pl.BlockSpec((1, bf), lambda i, j: (0, j)),
            pl.BlockSpec((bf, d), lambda i, j: (j, 0)),
            pl.BlockSpec((1, d), lambda i, j: (0, 0)),
            pl.BlockSpec((1, d), lambda i, j: (0, 0)),
            pl.BlockSpec((1, d), lambda i, j: (0, 0)),
        ],
        out_specs=pl.BlockSpec((bm, d), lambda i, j: (i, 0)),
        scratch_shapes=[pltpu.VMEM((bm, d), BF16)],
        compiler_params=pltpu.CompilerParams(
            dimension_semantics=("arbitrary", "arbitrary"), vmem_limit_bytes=48 * MIB),
        name="mlp",
    )(x2d, w1, b1, w2, b2, ln_g, ln_b)


def kernel(x, positions, w_in, b_in, m_conv_w, m_conv_b, m_norm_g, r_norm_g, w_branch_m, w_branch_r,
           w_out, b_out, ln1_g, ln1_b, w_ff1, b_ff1, w_ff2, b_ff2, ln2_g, ln2_b):
    batch, seq, d = x.shape
    depth = w_in.shape[0]
    n_chunks = seq // CHUNK
    alpha = (2.0 * depth) ** 0.25
    gate_lo = 2 * M_QK_W + 2 * M_V_W

    half = R_DQK // 2
    inv_freq = ROPE_BASE ** (-jnp.arange(half, dtype=F32) / half)
    inv_freq = jnp.concatenate([inv_freq, inv_freq]).reshape(1, R_DQK)
    sign = jnp.concatenate([-jnp.ones((half,), F32), jnp.ones((half,), F32)]).reshape(1, R_DQK)
    pos_p = positions.astype(F32).reshape(batch * n_chunks, 2, CHUNK // 2).transpose(0, 2, 1)
    pos_p = jnp.repeat(pos_p, half, axis=2).reshape(batch * n_chunks * (CHUNK // 2), R_DQK)

    h = x.reshape(batch * seq, d)
    for l in range(depth):
        u, gates = _proj(h, w_in[l].T, b_in[l].reshape(1, -1),
                         gate_lo=gate_lo, n_gate=2 * M_HEADS)
        h1 = _mixer(u, gates, pos_p, h, m_conv_w[l], m_conv_b[l].reshape(1, -1),
                    m_norm_g[l].reshape(1, -1), inv_freq, sign,
                    r_norm_g[l].reshape(1, -1),
                    w_branch_m[l], w_branch_r[l], w_out[l],
                    b_out[l].reshape(1, -1), ln1_g[l].reshape(1, -1), ln1_b[l].reshape(1, -1),
                    alpha=alpha, batch=batch, n_chunks=n_chunks)
        h = _mlp(h1, w_ff1[l].astype(BF16), b_ff1[l].reshape(1, -1), w_ff2[l].astype(BF16),
                 b_ff2[l].reshape(1, -1), ln2_g[l].reshape(1, -1), ln2_b[l].reshape(1, -1), alpha=alpha)
    return h.reshape(batch, seq, d)
```

```python
import functools
import math

import jax
import jax.numpy as jnp
from jax import lax
from jax.experimental import pallas as pl
from jax.experimental.pallas import tpu as pltpu

M_HEADS, M_DQK, M_DV = 4, 256, 512
R_HEADS, R_DQK, R_DV = 8, 128, 256
CONV_K = 4
CHUNK = 128
ROPE_BASE = 10000.0
LN_EPS = 1e-5
M_QK_W = M_HEADS * M_DQK
M_V_W = M_HEADS * M_DV
R_QK_W = R_HEADS * R_DQK
R_V_W = R_HEADS * R_DV

LANES = 128
SUBLANES = 8
GATE_PAD = LANES
MIB = 1024 * 1024

F32 = jnp.float32
BF16 = jnp.bfloat16

def _sigmoid(x):
    return 1.0 / (1.0 + jnp.exp(-x))


def _log_sigmoid(x):
    return jnp.minimum(x, 0.0) - jnp.log1p(jnp.exp(-jnp.abs(x)))


def _dot(a, b):
    return jnp.dot(a, b, preferred_element_type=F32)


def _dot_nt(a, b):
    return lax.dot_general(a, b, (((1,), (1,)), ((), ())), preferred_element_type=F32)


def _layer_norm(z, g, b):
    mu = jnp.mean(z, axis=-1, keepdims=True)
    d = z - mu
    var = jnp.mean(d * d, axis=-1, keepdims=True)
    return d * lax.rsqrt(var + LN_EPS) * g + b


def _head_norm(h, g):
    mu = jnp.mean(h, axis=-1, keepdims=True)
    d = h - mu
    var = jnp.mean(d * d, axis=-1, keepdims=True)
    return d * lax.rsqrt(var + LN_EPS) * g


def _col_tile(step, gate_blk, n_tiles):
    return lax.rem(step + gate_blk, n_tiles)


def _proj_kernel(gate_blk, n_gate, x_ref, wa_ref, wb_ref, ba_ref, bb_ref, u_ref, g_ref, w_scr, b_scr):
    j = _col_tile(pl.program_id(0), gate_blk, pl.num_programs(0))
    bn, d = wa_ref.shape
    rows = 128

    @pl.when(pl.program_id(1) == 0)
    def _():
        @pl.when(j < gate_blk)
        def _():
            for r in range(0, bn, rows):
                w_scr[r:r + rows, :] = wa_ref[r:r + rows, :].astype(BF16)
            b_scr[...] = ba_ref[...]

        @pl.when(j >= gate_blk)
        def _():
            for r in range(0, bn, rows):
                lo, hi = r + n_gate, r + n_gate + rows
                if hi <= bn:
                    blk = wa_ref[lo:hi, :]
                else:
                    blk = jnp.concatenate([wa_ref[lo:bn, :], wb_ref[0:hi - bn, :]], axis=0)
                w_scr[r:r + rows, :] = blk.astype(BF16)
            b_scr[...] = jnp.concatenate([ba_ref[...], bb_ref[...]], axis=1)[:, n_gate:n_gate + bn]

    xb = x_ref[...].astype(BF16)
    u_ref[...] = (_dot_nt(xb, w_scr[...]) + b_scr[...]).astype(u_ref.dtype)

    @pl.when(j == gate_blk)
    def _():
        g_ref[...] = _dot_nt(xb, wa_ref[0:GATE_PAD, :].astype(BF16)) + ba_ref[:, 0:GATE_PAD]

    @pl.when(j != gate_blk)
    def _():
        g_ref[...] = jnp.zeros_like(g_ref)


def _proj(x2d, w_t, b, *, gate_lo, n_gate, bm=1024, bn=1024):
    m, d = x2d.shape
    n = w_t.shape[0] - n_gate
    assert gate_lo % bn == 0 and n % bn == 0 and m % bm == 0
    assert n_gate % SUBLANES == 0 and n_gate <= LANES
    gate_blk = gate_lo // bn
    n_i = m // bm
    n_j = n // bn
    col = lambda s: _col_tile(s, gate_blk, n_j)
    return pl.pallas_call(
        functools.partial(_proj_kernel, gate_blk, n_gate),
        out_shape=(jax.ShapeDtypeStruct((m, n), BF16),
                   jax.ShapeDtypeStruct(((n_i + 1) * bm, GATE_PAD), F32)),
        grid=(n_j, n_i),
        in_specs=[
            pl.BlockSpec((bm, d), lambda s, i: (i, 0)),
            pl.BlockSpec((bn, d), lambda s, i: (col(s), 0)),
            pl.BlockSpec((n_gate, d), lambda s, i: ((col(s) + 1) * (bn // n_gate), 0)),
            pl.BlockSpec((1, bn), lambda s, i: (0, col(s))),
            pl.BlockSpec((1, LANES), lambda s, i: (0, (col(s) + 1) * (bn // LANES))),
        ],
        out_specs=(
            pl.BlockSpec((bm, bn), lambda s, i: (i, col(s))),
            pl.BlockSpec((bm, GATE_PAD), lambda s, i: (jnp.where(s == 0, i, n_i), 0)),
        ),
        scratch_shapes=[pltpu.VMEM((bn, d), BF16), pltpu.VMEM((1, bn), F32)],
        compiler_params=pltpu.CompilerParams(
            dimension_semantics=("arbitrary", "arbitrary"), vmem_limit_bytes=48 * MIB),
        name="proj",
    )(x2d, w_t, w_t, b, b)


def _mlstm_chunk(q_ref, k_ref, v_ref, g_ref, cw_ref, cb_ref, out_ref, c_ref, n_ref, m_ref, xbuf_ref):
    L = CHUNK
    hist = SUBLANES

    xbuf_ref[hist:hist + L, 0:M_QK_W] = q_ref[...].astype(F32)
    xbuf_ref[hist:hist + L, M_QK_W:2 * M_QK_W] = k_ref[...].astype(F32)
    conv = cb_ref[...]
    for j in range(CONV_K):
        start = hist - (CONV_K - 1) + j
        conv = conv + cw_ref[j:j + 1, :] * xbuf_ref[start:start + L, :]
    xbuf_ref[0:hist, :] = xbuf_ref[L:L + hist, :]
    qk = conv * _sigmoid(conv)

    g = g_ref[...]
    lfg = _log_sigmoid(g)
    g_rows = g.T[0:2 * M_HEADS, :]
    lf_rows = _log_sigmoid(g_rows)

    row_t = lax.broadcasted_iota(jnp.int32, (L, L), 0)
    col_s = lax.broadcasted_iota(jnp.int32, (L, L), 1)
    causal = col_s <= row_t
    lane = lax.broadcasted_iota(jnp.int32, (L, GATE_PAD), 1)

    for h in range(M_HEADS):
        li_row = g_rows[h:h + 1, :]
        lf_row = lf_rows[M_HEADS + h:M_HEADS + h + 1, :]
        li_col = jnp.sum(jnp.where(lane == h, g, 0.0), axis=1, keepdims=True)
        lf_col = jnp.sum(jnp.where(lane == M_HEADS + h, lfg, 0.0), axis=1, keepdims=True)
        b_col = jnp.sum(jnp.where(causal, lf_row, 0.0), axis=1, keepdims=True)
        b_row = jnp.sum(jnp.where(row_t <= col_s, lf_col, 0.0), axis=0, keepdims=True)
        g_tot = jnp.sum(lf_row, axis=1, keepdims=True)
        m_prev = m_ref[h:h + 1, 0:1]

        d_log = jnp.where(causal, b_col - b_row + li_row, -jnp.inf)
        m_inter = b_col + m_prev
        m_t = jnp.maximum(m_inter, jnp.max(d_log, axis=1, keepdims=True))
        w_intra = jnp.exp(d_log - m_t)
        w_inter = jnp.exp(m_inter - m_t)

        w_log_row = g_tot - b_row + li_row
        w_log_col = g_tot - b_col + li_col
        m_new = jnp.maximum(g_tot + m_prev, jnp.max(w_log_row, axis=1, keepdims=True))
        w_s = jnp.exp(w_log_col - m_new)
        decay = jnp.exp(g_tot + m_prev - m_new)

        qh = qk[:, h * M_DQK:(h + 1) * M_DQK] * (M_DQK ** -0.5)
        kh = qk[:, M_QK_W + h * M_DQK:M_QK_W + (h + 1) * M_DQK]
        qb = qh.astype(BF16)
        vb = v_ref[:, h * M_DV:(h + 1) * M_DV]
        c_old = c_ref[h]
        n_old = n_ref[h:h + 1, :]

        s = _dot_nt(qb, kh.astype(BF16)) * w_intra
        num = _dot(s.astype(BF16), vb) + w_inter * _dot(qb, c_old.astype(BF16))
        den = (jnp.sum(s, axis=1, keepdims=True)
               + w_inter * jnp.sum(qh * n_old, axis=1, keepdims=True))
        hh = num * (1.0 / jnp.maximum(jnp.abs(den), jnp.exp(-m_t)))

        kw = kh * w_s
        c_ref[h] = decay * c_old + _dot(kw.T.astype(BF16), vb)
        n_ref[h:h + 1, :] = decay * n_old + jnp.sum(kw, axis=0, keepdims=True)
        m_ref[h:h + 1, :] = jnp.broadcast_to(m_new, (1, LANES))

        out_ref[:, h * M_DV:(h + 1) * M_DV] = hh.astype(out_ref.dtype)


def _retention_chunk(q_ref, k_ref, v_ref, pos_ref, invf_ref, sign_ref, out_ref, r_ref):
    L = CHUNK
    half = R_DQK // 2
    ang = pos_ref[...] * invf_ref[...]
    lane = lax.broadcasted_iota(jnp.int32, (L // 2, R_DQK), 1)

    def unpack(packed):
        swapped = pltpu.roll(packed, half, 1)
        top = jnp.where(lane < half, packed, swapped)
        bottom = jnp.where(lane < half, swapped, packed)
        return jnp.concatenate([top, bottom], axis=0)

    cos = unpack(jnp.cos(ang))
    sin = unpack(jnp.sin(ang)) * sign_ref[...]

    t_col = lax.broadcasted_iota(jnp.int32, (L, 1), 0).astype(F32)
    rel = (lax.broadcasted_iota(jnp.int32, (L, L), 0)
           - lax.broadcasted_iota(jnp.int32, (L, L), 1)).astype(F32)
    scale = R_DQK ** -0.5

    for h in range(R_HEADS):
        log_gamma = math.log1p(-(2.0 ** (-5.0 - h)))
        decay = jnp.where(rel >= 0.0, jnp.exp(jnp.maximum(rel, 0.0) * log_gamma), 0.0) * scale
        xi = jnp.exp((t_col + 1.0) * log_gamma) * scale
        zeta = jnp.exp((L - 1.0 - t_col) * log_gamma)
        g_chunk = math.exp(L * log_gamma)

        qs = slice(h * R_DQK, (h + 1) * R_DQK)
        vs = slice(h * R_DV, (h + 1) * R_DV)
        xq = q_ref[:, qs].astype(F32)
        xk = k_ref[:, qs].astype(F32)
        qr = xq * cos + pltpu.roll(xq, half, 1) * sin
        kr = xk * cos + pltpu.roll(xk, half, 1) * sin
        qb = qr.astype(BF16)
        vb = v_ref[:, vs]
        r_old = r_ref[h]

        p = _dot_nt(qb, kr.astype(BF16)) * decay
        o = _dot(p.astype(BF16), vb) + _dot(qb, r_old.astype(BF16)) * xi
        r_ref[h] = g_chunk * r_old + _dot((kr * zeta).T.astype(BF16), vb)
        out_ref[:, vs] = o.astype(out_ref.dtype)


def _rec_kernel(n_chunks, mq_ref, mk_ref, mv_ref, g_ref, cw_ref, cb_ref,
                rq_ref, rk_ref, rv_ref, pos_ref, invf_ref, sign_ref,
                hm_ref, hr_ref, c_ref, n_ref, m_ref, xbuf_ref, r_ref):
    @pl.when(lax.rem(pl.program_id(0), n_chunks) == 0)
    def _():
        c_ref[...] = jnp.zeros_like(c_ref)
        n_ref[...] = jnp.zeros_like(n_ref)
        m_ref[...] = jnp.zeros_like(m_ref)
        r_ref[...] = jnp.zeros_like(r_ref)
        xbuf_ref[0:SUBLANES, :] = jnp.zeros((SUBLANES, 2 * M_QK_W), F32)

    _mlstm_chunk(mq_ref, mk_ref, mv_ref, g_ref, cw_ref, cb_ref, hm_ref, c_ref, n_ref, m_ref, xbuf_ref)
    _retention_chunk(rq_ref, rk_ref, rv_ref, pos_ref, invf_ref, sign_ref, hr_ref, r_ref)


def _rec(u, gates, pos_p, conv_w, conv_b, inv_freq, sign, *, batch, n_chunks):
    L = CHUNK
    total = batch * n_chunks
    r_q = (2 * M_QK_W + 2 * M_V_W) // R_QK_W
    r_v = (2 * M_QK_W + 2 * M_V_W + 2 * R_QK_W) // R_V_W
    const = lambda shape: pl.BlockSpec(shape, lambda t: (0, 0))
    return pl.pallas_call(
        functools.partial(_rec_kernel, n_chunks),
        out_shape=(jax.ShapeDtypeStruct((total * L, M_V_W), BF16),
                   jax.ShapeDtypeStruct((total * L, R_V_W), BF16)),
        grid=(total,),
        in_specs=[
            pl.BlockSpec((L, M_QK_W), lambda t: (t, 0)),
            pl.BlockSpec((L, M_QK_W), lambda t: (t, 1)),
            pl.BlockSpec((L, M_V_W), lambda t: (t, 1)),
            pl.BlockSpec((L, GATE_PAD), lambda t: (t, 0)),
            const((CONV_K, 2 * M_QK_W)),
            const((1, 2 * M_QK_W)),
            pl.BlockSpec((L, R_QK_W), lambda t: (t, r_q)),
            pl.BlockSpec((L, R_QK_W), lambda t: (t, r_q + 1)),
            pl.BlockSpec((L, R_V_W), lambda t: (t, r_v)),
            pl.BlockSpec((L // 2, R_DQK), lambda t: (t, 0)),
            const((1, R_DQK)),
            const((1, R_DQK)),
        ],
        out_specs=(pl.BlockSpec((L, M_V_W), lambda t: (t, 0)),
                   pl.BlockSpec((L, R_V_W), lambda t: (t, 0))),
        scratch_shapes=[
            pltpu.VMEM((M_HEADS, M_DQK, M_DV), F32),
            pltpu.VMEM((SUBLANES, M_DQK), F32),
            pltpu.VMEM((SUBLANES, LANES), F32),
            pltpu.VMEM((SUBLANES + L, 2 * M_QK_W), F32),
            pltpu.VMEM((R_HEADS, R_DQK, R_DV), F32),
        ],
        compiler_params=pltpu.CompilerParams(
            dimension_semantics=("arbitrary",), vmem_limit_bytes=40 * MIB),
        name="rec",
    )(u, u, u, gates, conv_w, conv_b, u, u, u, pos_p, inv_freq, sign)


def _load_weight_bf16(w_hbm, w_scr, stage, sem):
    n_slots, rows, _ = stage.shape
    n = w_hbm.shape[0] // rows

    def copy(c, slot):
        src = w_hbm.at[pl.ds(pl.multiple_of(c * rows, rows), rows), :]
        return pltpu.make_async_copy(src, stage.at[slot], sem.at[slot])

    copy(0, 0).start()

    def body(c, carry):
        slot = lax.rem(c, n_slots)

        @pl.when(c + 1 < n)
        def _():
            copy(c + 1, lax.rem(c + 1, n_slots)).start()

        copy(c, slot).wait()
        w_scr[pl.ds(pl.multiple_of(c * rows, rows), rows), :] = stage[slot].astype(BF16)
        return carry

    lax.fori_loop(0, n, body, 0)


def _branch_kernel(hh_ref, ho_ref, mo_ref, rg_ref, ga_ref, gb_ref, mng_ref, rng_ref, wm_hbm, wr_hbm,
                   out_ref, hm_scr, hr_scr, wm_ref, wr_ref, stage_ref, sem_ref):
    @pl.when(pl.program_id(0) == 0)
    def _():
        _load_weight_bf16(wm_hbm, wm_ref, stage_ref, sem_ref)
        _load_weight_bf16(wr_hbm, wr_ref, stage_ref, sem_ref)

    for h in range(M_HEADS):
        sl = slice(h * M_DV, (h + 1) * M_DV)
        y = _head_norm(hh_ref[:, sl].astype(F32), mng_ref[:, sl]) * _sigmoid(mo_ref[:, sl].astype(F32))
        hm_scr[:, sl] = y.astype(BF16)
    for h in range(R_HEADS):
        sl = slice(h * R_DV, (h + 1) * R_DV)
        sg = rg_ref[:, sl].astype(F32)
        y = _head_norm(ho_ref[:, sl].astype(F32), rng_ref[:, sl]) * (sg * _sigmoid(sg))
        hr_scr[:, sl] = y.astype(BF16)

    ya = _dot(hm_scr[...], wm_ref[...])
    yb = _dot(hr_scr[...], wr_ref[...])
    merged = _sigmoid(ga_ref[...].astype(F32)) * ya + _sigmoid(gb_ref[...].astype(F32)) * yb
    out_ref[...] = merged.astype(out_ref.dtype)


def _branch(hh, ho, u, m_norm_g, r_norm_g, wm, wr, *, bm=256, stage_rows=256):
    m, d = hh.shape[0], wm.shape[1]
    o_blk = (2 * M_QK_W + M_V_W) // M_V_W
    s_blk = (2 * M_QK_W + 2 * M_V_W + 2 * R_QK_W + R_V_W) // R_V_W
    g_a = (2 * M_QK_W + 2 * M_V_W + 2 * R_QK_W + 2 * R_V_W) // d
    assert wm.shape == (M_V_W, d) and wr.shape == (R_V_W, d) and M_V_W == R_V_W == d
    assert m % bm == 0 and d % stage_rows == 0
    const = lambda shape: pl.BlockSpec(shape, lambda i: (0, 0))
    in_hbm = pl.BlockSpec(memory_space=pl.ANY)
    return pl.pallas_call(
        _branch_kernel,
        out_shape=jax.ShapeDtypeStruct((m, d), BF16),
        grid=(m // bm,),
        in_specs=[
            pl.BlockSpec((bm, M_V_W), lambda i: (i, 0)),
            pl.BlockSpec((bm, R_V_W), lambda i: (i, 0)),
            pl.BlockSpec((bm, M_V_W), lambda i: (i, o_blk)),
            pl.BlockSpec((bm, R_V_W), lambda i: (i, s_blk)),
            pl.BlockSpec((bm, d), lambda i: (i, g_a)),
            pl.BlockSpec((bm, d), lambda i: (i, g_a + 1)),
            const((1, M_V_W)),
            const((1, R_V_W)),
            in_hbm,
            in_hbm,
        ],
        out_specs=pl.BlockSpec((bm, d), lambda i: (i, 0)),
        scratch_shapes=[
            pltpu.VMEM((bm, M_V_W), BF16),
            pltpu.VMEM((bm, R_V_W), BF16),
            pltpu.VMEM((M_V_W, d), BF16),
            pltpu.VMEM((R_V_W, d), BF16),
            pltpu.VMEM((2, stage_rows, d), F32),
            pltpu.SemaphoreType.DMA((2,)),
        ],
        compiler_params=pltpu.CompilerParams(
            dimension_semantics=("arbitrary",), vmem_limit_bytes=52 * MIB),
        name="branch",
    )(hh, ho, u, u, u, u, m_norm_g, r_norm_g, wm, wr)


def _outproj_kernel(alpha, mg_ref, x_ref, wo_hbm, bo_ref, lg_ref, lb_ref, out_ref, wo_ref, stage_ref, sem_ref):
    @pl.when(pl.program_id(0) == 0)
    def _():
        _load_weight_bf16(wo_hbm, wo_ref, stage_ref, sem_ref)

    y = _dot(mg_ref[...], wo_ref[...]) + bo_ref[...]
    out_ref[...] = _layer_norm(alpha * x_ref[...] + y, lg_ref[...], lb_ref[...])


def _outproj(merged, x2d, wo, bo, ln_g, ln_b, *, alpha, bm=512, stage_rows=256):
    m, d = x2d.shape
    assert wo.shape == (d, d) and m % bm == 0 and d % stage_rows == 0
    const = lambda shape: pl.BlockSpec(shape, lambda i: (0, 0))
    return pl.pallas_call(
        functools.partial(_outproj_kernel, alpha),
        out_shape=jax.ShapeDtypeStruct((m, d), F32),
        grid=(m // bm,),
        in_specs=[
            pl.BlockSpec((bm, d), lambda i: (i, 0)),
            pl.BlockSpec((bm, d), lambda i: (i, 0)),
            pl.BlockSpec(memory_space=pl.ANY),
            const((1, d)),
            const((1, d)),
            const((1, d)),
        ],
        out_specs=pl.BlockSpec((bm, d), lambda i: (i, 0)),
        scratch_shapes=[
            pltpu.VMEM((d, d), BF16),
            pltpu.VMEM((2, stage_rows, d), F32),
            pltpu.SemaphoreType.DMA((2,)),
        ],
        compiler_params=pltpu.CompilerParams(
            dimension_semantics=("arbitrary",), vmem_limit_bytes=48 * MIB),
        name="outproj",
    )(merged, x2d, wo, bo, ln_g, ln_b)


def _mlp_kernel(alpha, x_ref, w1_ref, b1_ref, w2_ref, b2_ref, lg_ref, lb_ref, out_ref, xb_ref):
    j = pl.program_id(1)

    @pl.when(j == 0)
    def _():
        xb_ref[...] = x_ref[...].astype(BF16)
        out_ref[...] = jnp.zeros_like(out_ref)

    hid = jnp.maximum(_dot(xb_ref[...], w1_ref[...]) + b1_ref[...], 0.0)
    out_ref[...] += _dot((hid * hid).astype(BF16), w2_ref[...])

    @pl.when(j == pl.num_programs(1) - 1)
    def _():
        z = alpha * x_ref[...] + (out_ref[...] + b2_ref[...])
        out_ref[...] = _layer_norm(z, lg_ref[...], lb_ref[...])


def _mlp(x2d, w1, b1, w2, b2, ln_g, ln_b, *, alpha, bm=512, bf=1024):
    m, d = x2d.shape
    f = w1.shape[1]
    return pl.pallas_call(
        functools.partial(_mlp_kernel, alpha),
        out_shape=jax.ShapeDtypeStruct((m, d), F32),
        grid=(m // bm, f // bf),
        in_specs=[
            pl.BlockSpec((bm, d), lambda i, j: (i, 0)),
            pl.BlockSpec((d, bf), lambda i, j: (0, j)),
            pl.BlockSpec((1, bf), lambda i, j: (0, j)),
            pl.BlockSpec((bf, d), lambda i, j: (j, 0)),
            pl.BlockSpec((1, d), lambda i, j: (0, 0)),
            pl.BlockSpec((1, d), lambda i, j: (0, 0)),
            pl.BlockSpec((1, d), lambda i, j: (0, 0)),
        ],
        out_specs=pl.BlockSpec((bm, d), lambda i, j: (i, 0)),
        scratch_shapes=[pltpu.VMEM((bm, d), BF16)],
        compiler_params=pltpu.CompilerParams(
            dimension_semantics=("arbitrary", "arbitrary"), vmem_limit_bytes=48 * MIB),
        name="mlp",
    )(x2d, w1, b1, w2, b2, ln_g, ln_b)


def kernel(x, positions, w_in, b_in, m_conv_w, m_conv_b, m_norm_g, r_norm_g, w_branch_m, w_branch_r,
           w_out, b_out, ln1_g, ln1_b, w_ff1, b_ff1, w_ff2, b_ff2, ln2_g, ln2_b):
    batch, seq, d = x.shape
    depth = w_in.shape[0]
    n_chunks = seq // CHUNK
    alpha = (2.0 * depth) ** 0.25
    gate_lo = 2 * M_QK_W + 2 * M_V_W

    half = R_DQK // 2
    inv_freq = ROPE_BASE ** (-jnp.arange(half, dtype=F32) / half)
    inv_freq = jnp.concatenate([inv_freq, inv_freq]).reshape(1, R_DQK)
    sign = jnp.concatenate([-jnp.ones((half,), F32), jnp.ones((half,), F32)]).reshape(1, R_DQK)
    pos_p = positions.astype(F32).reshape(batch * n_chunks, 2, CHUNK // 2).transpose(0, 2, 1)
    pos_p = jnp.repeat(pos_p, half, axis=2).reshape(batch * n_chunks * (CHUNK // 2), R_DQK)

    h = x.reshape(batch * seq, d)
    for l in range(depth):
        u, gates = _proj(h, w_in[l].T, b_in[l].reshape(1, -1),
                         gate_lo=gate_lo, n_gate=2 * M_HEADS)
        hh, ho = _rec(u, gates, pos_p, m_conv_w[l], m_conv_b[l].reshape(1, -1), inv_freq, sign,
                      batch=batch, n_chunks=n_chunks)
        merged = _branch(hh, ho, u, m_norm_g[l].reshape(1, -1), r_norm_g[l].reshape(1, -1),
                         w_branch_m[l], w_branch_r[l])
        h1 = _outproj(merged, h, w_out[l], b_out[l].reshape(1, -1), ln1_g[l].reshape(1, -1),
                      ln1_b[l].reshape(1, -1), alpha=alpha)
        h = _mlp(h1, w_ff1[l].astype(BF16), b_ff1[l].reshape(1, -1), w_ff2[l].astype(BF16),
                 b_ff2[l].reshape(1, -1), ln2_g[l].reshape(1, -1), ln2_b[l].reshape(1, -1), alpha=alpha)
    return h.reshape(batch, seq, d)
```

```python
import functools
import math

import jax
import jax.numpy as jnp
from jax import lax
from jax.experimental import pallas as pl
from jax.experimental.pallas import tpu as pltpu

M_HEADS, M_DQK, M_DV = 4, 256, 512
R_HEADS, R_DQK, R_DV = 8, 128, 256
CONV_K = 4
CHUNK = 128
ROPE_BASE = 10000.0
LN_EPS = 1e-5
M_QK_W = M_HEADS * M_DQK
M_V_W = M_HEADS * M_DV
R_QK_W = R_HEADS * R_DQK
R_V_W = R_HEADS * R_DV

LANES = 128
SUBLANES = 8
GATE_PAD = LANES
MIB = 1024 * 1024

F32 = jnp.float32
BF16 = jnp.bfloat16

def _sigmoid(x):
    return 1.0 / (1.0 + jnp.exp(-x))


def _log_sigmoid(x):
    return jnp.minimum(x, 0.0) - jnp.log1p(jnp.exp(-jnp.abs(x)))


def _dot(a, b):
    return jnp.dot(a, b, preferred_element_type=F32)


def _dot_nt(a, b):
    return lax.dot_general(a, b, (((1,), (1,)), ((), ())), preferred_element_type=F32)


def _layer_norm(z, g, b):
    mu = jnp.mean(z, axis=-1, keepdims=True)
    d = z - mu
    var = jnp.mean(d * d, axis=-1, keepdims=True)
    return d * lax.rsqrt(var + LN_EPS) * g + b


def _head_norm(h, g):
    mu = jnp.mean(h, axis=-1, keepdims=True)
    d = h - mu
    var = jnp.mean(d * d, axis=-1, keepdims=True)
    return d * lax.rsqrt(var + LN_EPS) * g


def _col_tile(step, gate_blk, n_tiles):
    return lax.rem(step + gate_blk, n_tiles)


def _proj_kernel(gate_blk, n_gate, x_ref, wa_ref, wb_ref, ba_ref, bb_ref, u_ref, g_ref, w_scr, b_scr):
    j = _col_tile(pl.program_id(0), gate_blk, pl.num_programs(0))
    bn, d = wa_ref.shape
    rows = 128

    @pl.when(pl.program_id(1) == 0)
    def _():
        @pl.when(j < gate_blk)
        def _():
            for r in range(0, bn, rows):
                w_scr[r:r + rows, :] = wa_ref[r:r + rows, :].astype(BF16)
            b_scr[...] = ba_ref[...]

        @pl.when(j >= gate_blk)
        def _():
            for r in range(0, bn, rows):
                lo, hi = r + n_gate, r + n_gate + rows
                if hi <= bn:
                    blk = wa_ref[lo:hi, :]
                else:
                    blk = jnp.concatenate([wa_ref[lo:bn, :], wb_ref[0:hi - bn, :]], axis=0)
                w_scr[r:r + rows, :] = blk.astype(BF16)
            b_scr[...] = jnp.concatenate([ba_ref[...], bb_ref[...]], axis=1)[:, n_gate:n_gate + bn]

    xb = x_ref[...].astype(BF16)
    u_ref[...] = (_dot_nt(xb, w_scr[...]) + b_scr[...]).astype(u_ref.dtype)

    @pl.when(j == gate_blk)
    def _():
        g_ref[...] = _dot_nt(xb, wa_ref[0:GATE_PAD, :].astype(BF16)) + ba_ref[:, 0:GATE_PAD]

    @pl.when(j != gate_blk)
    def _():
        g_ref[...] = jnp.zeros_like(g_ref)


def _proj(x2d, w_t, b, *, gate_lo, n_gate, bm=1024, bn=1024):
    m, d = x2d.shape
    n = w_t.shape[0] - n_gate
    assert gate_lo % bn == 0 and n % bn == 0 and m % bm == 0
    assert n_gate % SUBLANES == 0 and n_gate <= LANES
    gate_blk = gate_lo // bn
    n_i = m // bm
    n_j = n // bn
    col = lambda s: _col_tile(s, gate_blk, n_j)
    return pl.pallas_call(
        functools.partial(_proj_kernel, gate_blk, n_gate),
        out_shape=(jax.ShapeDtypeStruct((m, n), BF16),
                   jax.ShapeDtypeStruct(((n_i + 1) * bm, GATE_PAD), F32)),
        grid=(n_j, n_i),
        in_specs=[
            pl.BlockSpec((bm, d), lambda s, i: (i, 0)),
            pl.BlockSpec((bn, d), lambda s, i: (col(s), 0)),
            pl.BlockSpec((n_gate, d), lambda s, i: ((col(s) + 1) * (bn // n_gate), 0)),
            pl.BlockSpec((1, bn), lambda s, i: (0, col(s))),
            pl.BlockSpec((1, LANES), lambda s, i: (0, (col(s) + 1) * (bn // LANES))),
        ],
        out_specs=(
            pl.BlockSpec((bm, bn), lambda s, i: (i, col(s))),
            pl.BlockSpec((bm, GATE_PAD), lambda s, i: (jnp.where(s == 0, i, n_i), 0)),
        ),
        scratch_shapes=[pltpu.VMEM((bn, d), BF16), pltpu.VMEM((1, bn), F32)],
        compiler_params=pltpu.CompilerParams(
            dimension_semantics=("arbitrary", "arbitrary"), vmem_limit_bytes=48 * MIB),
        name="proj",
    )(x2d, w_t, w_t, b, b)


def _conv_shift_matrix():
    L = CHUNK
    row = lax.broadcasted_iota(jnp.int32, (L, 2 * L), 0)
    col = lax.broadcasted_iota(jnp.int32, (L, 2 * L), 1)
    blocks = [jnp.where(col == row + (L - (CONV_K - 1) + j), 1.0, 0.0) for j in range(CONV_K)]
    return jnp.concatenate(blocks, axis=0).astype(BF16)


def _mlstm_chunk(q_ref, k_ref, v_ref, g_ref, cw_ref, cb_ref, out_ref, c_ref, n_ref, m_ref,
                 x2_ref, shift_ref):
    L = CHUNK

    x2_ref[L:2 * L, 0:M_QK_W] = q_ref[...]
    x2_ref[L:2 * L, M_QK_W:2 * M_QK_W] = k_ref[...]
    taps = _dot(shift_ref[...], x2_ref[...])
    conv = cb_ref[...]
    for j in range(CONV_K):
        conv = conv + cw_ref[j:j + 1, :] * taps[j * L:(j + 1) * L, :]
    x2_ref[0:L, :] = x2_ref[L:2 * L, :]
    qk = conv * _sigmoid(conv)

    g = g_ref[...]
    lfg = _log_sigmoid(g)
    g_rows = g.T[0:2 * M_HEADS, :]
    lf_rows = _log_sigmoid(g_rows)

    row_t = lax.broadcasted_iota(jnp.int32, (L, L), 0)
    col_s = lax.broadcasted_iota(jnp.int32, (L, L), 1)
    causal = col_s <= row_t
    lane = lax.broadcasted_iota(jnp.int32, (L, GATE_PAD), 1)

    heads = range(M_HEADS)
    qh = [qk[:, h * M_DQK:(h + 1) * M_DQK] * (M_DQK ** -0.5) for h in heads]
    kh = [qk[:, M_QK_W + h * M_DQK:M_QK_W + (h + 1) * M_DQK] for h in heads]
    qb = [q.astype(BF16) for q in qh]
    vb = [v_ref[:, h * M_DV:(h + 1) * M_DV] for h in heads]
    scores = [_dot_nt(qb[h], kh[h].astype(BF16)) for h in heads]
    inter = [_dot(qb[h], c_ref[h].astype(BF16)) for h in heads]

    gate = []
    for h in heads:
        li_row = g_rows[h:h + 1, :]
        lf_row = lf_rows[M_HEADS + h:M_HEADS + h + 1, :]
        li_col = jnp.sum(jnp.where(lane == h, g, 0.0), axis=1, keepdims=True)
        lf_col = jnp.sum(jnp.where(lane == M_HEADS + h, lfg, 0.0), axis=1, keepdims=True)
        b_col = jnp.sum(jnp.where(causal, lf_row, 0.0), axis=1, keepdims=True)
        b_row = jnp.sum(jnp.where(row_t <= col_s, lf_col, 0.0), axis=0, keepdims=True)
        g_tot = jnp.sum(lf_row, axis=1, keepdims=True)
        m_prev = m_ref[h:h + 1, 0:1]

        d_log = jnp.where(causal, b_col - b_row + li_row, -jnp.inf)
        m_inter = b_col + m_prev
        m_t = jnp.maximum(m_inter, jnp.max(d_log, axis=1, keepdims=True))
        w_intra = jnp.exp(d_log - m_t)
        w_inter = jnp.exp(m_inter - m_t)

        w_log_row = g_tot - b_row + li_row
        w_log_col = g_tot - b_col + li_col
        m_new = jnp.maximum(g_tot + m_prev, jnp.max(w_log_row, axis=1, keepdims=True))
        w_s = jnp.exp(w_log_col - m_new)
        decay = jnp.exp(g_tot + m_prev - m_new)
        gate.append((m_t, w_intra, w_inter, m_new, w_s, decay))

    for h in heads:
        m_t, w_intra, w_inter, m_new, w_s, decay = gate[h]
        n_old = n_ref[h:h + 1, :]
        s = scores[h] * w_intra
        num = _dot(s.astype(BF16), vb[h]) + w_inter * inter[h]
        den = (jnp.sum(s, axis=1, keepdims=True)
               + w_inter * jnp.sum(qh[h] * n_old, axis=1, keepdims=True))
        hh = num * (1.0 / jnp.maximum(jnp.abs(den), jnp.exp(-m_t)))

        kw = kh[h] * w_s
        c_ref[h] = decay * c_ref[h] + _dot(kw.T.astype(BF16), vb[h])
        n_ref[h:h + 1, :] = decay * n_old + jnp.sum(kw, axis=0, keepdims=True)
        m_ref[h:h + 1, :] = jnp.broadcast_to(m_new, (1, LANES))

        out_ref[:, h * M_DV:(h + 1) * M_DV] = hh.astype(out_ref.dtype)


def _retention_constants(decay_ref, xi_ref, zeta_ref):
    L = CHUNK
    t_col = lax.broadcasted_iota(jnp.int32, (L, 1), 0).astype(F32)
    rel = (lax.broadcasted_iota(jnp.int32, (L, L), 0)
           - lax.broadcasted_iota(jnp.int32, (L, L), 1)).astype(F32)
    scale = R_DQK ** -0.5
    for h in range(R_HEADS):
        log_gamma = math.log1p(-(2.0 ** (-5.0 - h)))
        decay_ref[h] = jnp.where(rel >= 0.0, jnp.exp(jnp.maximum(rel, 0.0) * log_gamma), 0.0) * scale
        xi_ref[h] = jnp.exp((t_col + 1.0) * log_gamma) * scale
        zeta_ref[h] = jnp.exp((L - 1.0 - t_col) * log_gamma)


def _retention_chunk(q_ref, k_ref, v_ref, pos_ref, invf_ref, sign_ref, out_ref, r_ref,
                     decay_ref, xi_ref, zeta_ref):
    L = CHUNK
    half = R_DQK // 2
    ang = pos_ref[...] * invf_ref[...]
    lane = lax.broadcasted_iota(jnp.int32, (L // 2, R_DQK), 1)

    def unpack(packed):
        swapped = pltpu.roll(packed, half, 1)
        top = jnp.where(lane < half, packed, swapped)
        bottom = jnp.where(lane < half, swapped, packed)
        return jnp.concatenate([top, bottom], axis=0)

    cos = unpack(jnp.cos(ang))
    sin = unpack(jnp.sin(ang)) * sign_ref[...]

    heads = range(R_HEADS)
    qr, kr = [], []
    for h in heads:
        qs = slice(h * R_DQK, (h + 1) * R_DQK)
        xq = q_ref[:, qs].astype(F32)
        xk = k_ref[:, qs].astype(F32)
        qr.append((xq * cos + pltpu.roll(xq, half, 1) * sin).astype(BF16))
        kr.append(xk * cos + pltpu.roll(xk, half, 1) * sin)
    vb = [v_ref[:, h * R_DV:(h + 1) * R_DV] for h in heads]
    scores = [_dot_nt(qr[h], kr[h].astype(BF16)) for h in heads]
    cross = [_dot(qr[h], r_ref[h].astype(BF16)) for h in heads]

    for h in heads:
        g_chunk = math.exp(L * math.log1p(-(2.0 ** (-5.0 - h))))
        p = scores[h] * decay_ref[h]
        o = _dot(p.astype(BF16), vb[h]) + cross[h] * xi_ref[h]
        r_ref[h] = g_chunk * r_ref[h] + _dot((kr[h] * zeta_ref[h]).T.astype(BF16), vb[h])
        out_ref[:, h * R_DV:(h + 1) * R_DV] = o.astype(out_ref.dtype)


def _rec_kernel(n_chunks, mq_ref, mk_ref, mv_ref, g_ref, cw_ref, cb_ref,
                rq_ref, rk_ref, rv_ref, pos_ref, invf_ref, sign_ref,
                hm_ref, hr_ref, c_ref, n_ref, m_ref, x2_ref, shift_ref, r_ref, decay_ref, xi_ref, zeta_ref):
    @pl.when(pl.program_id(0) == 0)
    def _():
        shift_ref[...] = _conv_shift_matrix()
        _retention_constants(decay_ref, xi_ref, zeta_ref)

    @pl.when(lax.rem(pl.program_id(0), n_chunks) == 0)
    def _():
        c_ref[...] = jnp.zeros_like(c_ref)
        n_ref[...] = jnp.zeros_like(n_ref)
        m_ref[...] = jnp.zeros_like(m_ref)
        r_ref[...] = jnp.zeros_like(r_ref)
        x2_ref[0:CHUNK, :] = jnp.zeros((CHUNK, 2 * M_QK_W), BF16)

    _mlstm_chunk(mq_ref, mk_ref, mv_ref, g_ref, cw_ref, cb_ref, hm_ref, c_ref, n_ref, m_ref,
                 x2_ref, shift_ref)
    _retention_chunk(rq_ref, rk_ref, rv_ref, pos_ref, invf_ref, sign_ref, hr_ref, r_ref,
                     decay_ref, xi_ref, zeta_ref)


def _rec(u, gates, pos_p, conv_w, conv_b, inv_freq, sign, *, batch, n_chunks):
    L = CHUNK
    total = batch * n_chunks
    r_q = (2 * M_QK_W + 2 * M_V_W) // R_QK_W
    r_v = (2 * M_QK_W + 2 * M_V_W + 2 * R_QK_W) // R_V_W
    const = lambda shape: pl.BlockSpec(shape, lambda t: (0, 0))
    return pl.pallas_call(
        functools.partial(_rec_kernel, n_chunks),
        out_shape=(jax.ShapeDtypeStruct((total * L, M_V_W), BF16),
                   jax.ShapeDtypeStruct((total * L, R_V_W), BF16)),
        grid=(total,),
        in_specs=[
            pl.BlockSpec((L, M_QK_W), lambda t: (t, 0)),
            pl.BlockSpec((L, M_QK_W), lambda t: (t, 1)),
            pl.BlockSpec((L, M_V_W), lambda t: (t, 1)),
            pl.BlockSpec((L, GATE_PAD), lambda t: (t, 0)),
            const((CONV_K, 2 * M_QK_W)),
            const((1, 2 * M_QK_W)),
            pl.BlockSpec((L, R_QK_W), lambda t: (t, r_q)),
            pl.BlockSpec((L, R_QK_W), lambda t: (t, r_q + 1)),
            pl.BlockSpec((L, R_V_W), lambda t: (t, r_v)),
            pl.BlockSpec((L // 2, R_DQK), lambda t: (t, 0)),
            const((1, R_DQK)),
            const((1, R_DQK)),
        ],
        out_specs=(pl.BlockSpec((L, M_V_W), lambda t: (t, 0)),
                   pl.BlockSpec((L, R_V_W), lambda t: (t, 0))),
        scratch_shapes=[
            pltpu.VMEM((M_HEADS, M_DQK, M_DV), F32),
            pltpu.VMEM((SUBLANES, M_DQK), F32),
            pltpu.VMEM((SUBLANES, LANES), F32),
            pltpu.VMEM((2 * L, 2 * M_QK_W), BF16),
            pltpu.VMEM((CONV_K * L, 2 * L), BF16),
            pltpu.VMEM((R_HEADS, R_DQK, R_DV), F32),
            pltpu.VMEM((R_HEADS, L, L), F32),
            pltpu.VMEM((R_HEADS, L, 1), F32),
            pltpu.VMEM((R_HEADS, L, 1), F32),
        ],
        compiler_params=pltpu.CompilerParams(
            dimension_semantics=("arbitrary",), vmem_limit_bytes=40 * MIB),
        name="rec",
    )(u, u, u, gates, conv_w, conv_b, u, u, u, pos_p, inv_freq, sign)


def _load_weight_bf16(w_hbm, w_scr, stage, sem):
    n_slots, rows, _ = stage.shape
    n = w_hbm.shape[0] // rows

    def copy(c, slot):
        src = w_hbm.at[pl.ds(pl.multiple_of(c * rows, rows), rows), :]
        return pltpu.make_async_copy(src, stage.at[slot], sem.at[slot])

    copy(0, 0).start()

    def body(c, carry):
        slot = lax.rem(c, n_slots)

        @pl.when(c + 1 < n)
        def _():
            copy(c + 1, lax.rem(c + 1, n_slots)).start()

        copy(c, slot).wait()
        w_scr[pl.ds(pl.multiple_of(c * rows, rows), rows), :] = stage[slot].astype(BF16)
        return carry

    lax.fori_loop(0, n, body, 0)


def _branch_kernel(hh_ref, ho_ref, mo_ref, rg_ref, ga_ref, gb_ref, mng_ref, rng_ref, wm_hbm, wr_hbm,
                   out_ref, hm_scr, hr_scr, wm_ref, wr_ref, stage_ref, sem_ref):
    @pl.when(pl.program_id(0) == 0)
    def _():
        _load_weight_bf16(wm_hbm, wm_ref, stage_ref, sem_ref)
        _load_weight_bf16(wr_hbm, wr_ref, stage_ref, sem_ref)

    for h in range(M_HEADS):
        sl = slice(h * M_DV, (h + 1) * M_DV)
        y = _head_norm(hh_ref[:, sl].astype(F32), mng_ref[:, sl]) * _sigmoid(mo_ref[:, sl].astype(F32))
        hm_scr[:, sl] = y.astype(BF16)
    for h in range(R_HEADS):
        sl = slice(h * R_DV, (h + 1) * R_DV)
        sg = rg_ref[:, sl].astype(F32)
        y = _head_norm(ho_ref[:, sl].astype(F32), rng_ref[:, sl]) * (sg * _sigmoid(sg))
        hr_scr[:, sl] = y.astype(BF16)

    ya = _dot(hm_scr[...], wm_ref[...])
    yb = _dot(hr_scr[...], wr_ref[...])
    merged = _sigmoid(ga_ref[...].astype(F32)) * ya + _sigmoid(gb_ref[...].astype(F32)) * yb
    out_ref[...] = merged.astype(out_ref.dtype)


def _branch(hh, ho, u, m_norm_g, r_norm_g, wm, wr, *, bm=256, stage_rows=256):
    m, d = hh.shape[0], wm.shape[1]
    o_blk = (2 * M_QK_W + M_V_W) // M_V_W
    s_blk = (2 * M_QK_W + 2 * M_V_W + 2 * R_QK_W + R_V_W) // R_V_W
    g_a = (2 * M_QK_W + 2 * M_V_W + 2 * R_QK_W + 2 * R_V_W) // d
    assert wm.shape == (M_V_W, d) and wr.shape == (R_V_W, d) and M_V_W == R_V_W == d
    assert m % bm == 0 and d % stage_rows == 0
    const = lambda shape: pl.BlockSpec(shape, lambda i: (0, 0))
    in_hbm = pl.BlockSpec(memory_space=pl.ANY)
    return pl.pallas_call(
        _branch_kernel,
        out_shape=jax.ShapeDtypeStruct((m, d), BF16),
        grid=(m // bm,),
        in_specs=[
            pl.BlockSpec((bm, M_V_W), lambda i: (i, 0)),
            pl.BlockSpec((bm, R_V_W), lambda i: (i, 0)),
            pl.BlockSpec((bm, M_V_W), lambda i: (i, o_blk)),
            pl.BlockSpec((bm, R_V_W), lambda i: (i, s_blk)),
            pl.BlockSpec((bm, d), lambda i: (i, g_a)),
            pl.BlockSpec((bm, d), lambda i: (i, g_a + 1)),
            const((1, M_V_W)),
            const((1, R_V_W)),
            in_hbm,
            in_hbm,
        ],
        out_specs=pl.BlockSpec((bm, d), lambda i: (i, 0)),
        scratch_shapes=[
            pltpu.VMEM((bm, M_V_W), BF16),
            pltpu.VMEM((bm, R_V_W), BF16),
            pltpu.VMEM((M_V_W, d), BF16),
            pltpu.VMEM((R_V_W, d), BF16),
            pltpu.VMEM((2, stage_rows, d), F32),
            pltpu.SemaphoreType.DMA((2,)),
        ],
        compiler_params=pltpu.CompilerParams(
            dimension_semantics=("arbitrary",), vmem_limit_bytes=52 * MIB),
        name="branch",
    )(hh, ho, u, u, u, u, m_norm_g, r_norm_g, wm, wr)


def _outproj_kernel(alpha, mg_ref, x_ref, wo_hbm, bo_ref, lg_ref, lb_ref, out_ref, wo_ref, stage_ref, sem_ref):
    @pl.when(pl.program_id(0) == 0)
    def _():
        _load_weight_bf16(wo_hbm, wo_ref, stage_ref, sem_ref)

    y = _dot(mg_ref[...], wo_ref[...]) + bo_ref[...]
    out_ref[...] = _layer_norm(alpha * x_ref[...] + y, lg_ref[...], lb_ref[...])


def _outproj(merged, x2d, wo, bo, ln_g, ln_b, *, alpha, bm=512, stage_rows=256):
    m, d = x2d.shape
    assert wo.shape == (d, d) and m % bm == 0 and d % stage_rows == 0
    const = lambda shape: pl.BlockSpec(shape, lambda i: (0, 0))
    return pl.pallas_call(
        functools.partial(_outproj_kernel, alpha),
        out_shape=jax.ShapeDtypeStruct((m, d), F32),
        grid=(m // bm,),
        in_specs=[
            pl.BlockSpec((bm, d), lambda i: (i, 0)),
            pl.BlockSpec((bm, d), lambda i: (i, 0)),
            pl.BlockSpec(memory_space=pl.ANY),
            const((1, d)),
            const((1, d)),
            const((1, d)),
        ],
        out_specs=pl.BlockSpec((bm, d), lambda i: (i, 0)),
        scratch_shapes=[
            pltpu.VMEM((d, d), BF16),
            pltpu.VMEM((2, stage_rows, d), F32),
            pltpu.SemaphoreType.DMA((2,)),
        ],
        compiler_params=pltpu.CompilerParams(
            dimension_semantics=("arbitrary",), vmem_limit_bytes=48 * MIB),
        name="outproj",
    )(merged, x2d, wo, bo, ln_g, ln_b)


def _mlp_kernel(alpha, x_ref, w1_ref, b1_ref, w2_ref, b2_ref, lg_ref, lb_ref, out_ref, xb_ref):
    j = pl.program_id(1)

    @pl.when(j == 0)
    def _():
        xb_ref[...] = x_ref[...].astype(BF16)
        out_ref[...] = jnp.zeros_like(out_ref)

    hid = jnp.maximum(_dot(xb_ref[...], w1_ref[...]) + b1_ref[...], 0.0)
    out_ref[...] += _dot((hid * hid).astype(BF16), w2_ref[...])

    @pl.when(j == pl.num_programs(1) - 1)
    def _():
        z = alpha * x_ref[...] + (out_ref[...] + b2_ref[...])
        out_ref[...] = _layer_norm(z, lg_ref[...], lb_ref[...])


def _mlp(x2d, w1, b1, w2, b2, ln_g, ln_b, *, alpha, bm=512, bf=1024):
    m, d = x2d.shape
    f = w1.shape[1]
    return pl.pallas_call(
        functools.partial(_mlp_kernel, alpha),
        out_shape=jax.ShapeDtypeStruct((m, d), F32),
        grid=(m // bm, f // bf),
        in_specs=[
            pl.BlockSpec((bm, d), lambda i, j: (i, 0)),
            pl.BlockSpec((d, bf), lambda i, j: (0, j)),
            pl.BlockSpec((1, bf), lambda i, j: (0, j)),
            pl.BlockSpec((bf, d), lambda i, j: (j, 0)),
            pl.BlockSpec((1, d), lambda i, j: (0, 0)),
            pl.BlockSpec((1, d), lambda i, j: (0, 0)),
            pl.BlockSpec((1, d), lambda i, j: (0, 0)),
        ],
        out_specs=pl.BlockSpec((bm, d), lambda i, j: (i, 0)),
        scratch_shapes=[pltpu.VMEM((bm, d), BF16)],
        compiler_params=pltpu.CompilerParams(
            dimension_semantics=("arbitrary", "arbitrary"), vmem_limit_bytes=48 * MIB),
        name="mlp",
    )(x2d, w1, b1, w2, b2, ln_g, ln_b)


def kernel(x, positions, w_in, b_in, m_conv_w, m_conv_b, m_norm_g, r_norm_g, w_branch_m, w_branch_r,
           w_out, b_out, ln1_g, ln1_b, w_ff1, b_ff1, w_ff2, b_ff2, ln2_g, ln2_b):
    batch, seq, d = x.shape
    depth = w_in.shape[0]
    n_chunks = seq // CHUNK
    alpha = (2.0 * depth) ** 0.25
    gate_lo = 2 * M_QK_W + 2 * M_V_W

    half = R_DQK // 2
    inv_freq = ROPE_BASE ** (-jnp.arange(half, dtype=F32) / half)
    inv_freq = jnp.concatenate([inv_freq, inv_freq]).reshape(1, R_DQK)
    sign = jnp.concatenate([-jnp.ones((half,), F32), jnp.ones((half,), F32)]).reshape(1, R_DQK)
    pos_p = positions.astype(F32).reshape(batch * n_chunks, 2, CHUNK // 2).transpose(0, 2, 1)
    pos_p = jnp.repeat(pos_p, half, axis=2).reshape(batch * n_chunks * (CHUNK // 2), R_DQK)

    h = x.reshape(batch * seq, d)
    for l in range(depth):
        u, gates = _proj(h, w_in[l].T, b_in[l].reshape(1, -1),
                         gate_lo=gate_lo, n_gate=2 * M_HEADS)
        hh, ho = _rec(u, gates, pos_p, m_conv_w[l], m_conv_b[l].reshape(1, -1), inv_freq, sign,
                      batch=batch, n_chunks=n_chunks)
        merged = _branch(hh, ho, u, m_norm_g[l].reshape(1, -1), r_norm_g[l].reshape(1, -1),
                         w_branch_m[l], w_branch_r[l])
        h1 = _outproj(merged, h, w_out[l], b_out[l].reshape(1, -1), ln1_g[l].reshape(1, -1),
                      ln1_b[l].reshape(1, -1), alpha=alpha)
        h = _mlp(h1, w_ff1[l].astype(BF16), b_ff1[l].reshape(1, -1), w_ff2[l].astype(BF16),
                 b_ff2[l].reshape(1, -1), ln2_g[l].reshape(1, -1), ln2_b[l].reshape(1, -1), alpha=alpha)
    return h.reshape(batch, seq, d)
```

```python
import functools
import math

import jax
import jax.numpy as jnp
from jax import lax
from jax.experimental import pallas as pl
from jax.experimental.pallas import tpu as pltpu

M_HEADS, M_DQK, M_DV = 4, 256, 512
R_HEADS, R_DQK, R_DV = 8, 128, 256
CONV_K = 4
CHUNK = 128
ROPE_BASE = 10000.0
LN_EPS = 1e-5
M_QK_W = M_HEADS * M_DQK
M_V_W = M_HEADS * M_DV
R_QK_W = R_HEADS * R_DQK
R_V_W = R_HEADS * R_DV

LANES = 128
SUBLANES = 8
GATE_PAD = LANES
MIB = 1024 * 1024

F32 = jnp.float32
BF16 = jnp.bfloat16

def _sigmoid(x):
    return 1.0 / (1.0 + jnp.exp(-x))


def _log_sigmoid(x):
    return jnp.minimum(x, 0.0) - jnp.log1p(jnp.exp(-jnp.abs(x)))


def _dot(a, b):
    return jnp.dot(a, b, preferred_element_type=F32)


def _dot_nt(a, b):
    return lax.dot_general(a, b, (((1,), (1,)), ((), ())), preferred_element_type=F32)


def _layer_norm(z, g, b):
    mu = jnp.mean(z, axis=-1, keepdims=True)
    d = z - mu
    var = jnp.mean(d * d, axis=-1, keepdims=True)
    return d * lax.rsqrt(var + LN_EPS) * g + b


def _head_norm(h, g):
    mu = jnp.mean(h, axis=-1, keepdims=True)
    d = h - mu
    var = jnp.mean(d * d, axis=-1, keepdims=True)
    return d * lax.rsqrt(var + LN_EPS) * g


def _col_tile(step, gate_blk, n_tiles):
    return lax.rem(step + gate_blk, n_tiles)


def _proj_kernel(gate_blk, n_gate, x_ref, wg_ref, b_ref, bg_ref, w_hbm, u_ref, g_ref,
                 w_scr, stage_ref, sem_ref):
    s = pl.program_id(0)
    i = pl.program_id(1)
    n_s, n_i = pl.num_programs(0), pl.num_programs(1)
    bn = w_scr.shape[1]
    rows = stage_ref.shape[0]
    slot = lax.rem(s, 2)

    def chunk_copy(step, k):
        c = _col_tile(step, gate_blk, n_s)
        r0 = c * bn + jnp.where(c >= gate_blk, n_gate, 0) + k * rows
        return pltpu.make_async_copy(w_hbm.at[pl.ds(pl.multiple_of(r0, SUBLANES), rows), :],
                                     stage_ref, sem_ref)

    def convert_into(dst_slot, k):
        w_scr[dst_slot, pl.ds(pl.multiple_of(k * rows, rows), rows), :] = stage_ref[...].astype(BF16)

    @pl.when((s == 0) & (i == 0))
    def _():
        def body(k, carry):
            chunk_copy(0, k).start()
            chunk_copy(0, k).wait()
            convert_into(0, k)
            return carry
        lax.fori_loop(0, bn // rows, body, 0)

    @pl.when(s + 1 < n_s)
    def _():
        chunk_copy(s + 1, i).start()

    xb = x_ref[...].astype(BF16)
    for c0 in range(0, bn, bn // 2):
        cs = slice(c0, c0 + bn // 2)
        acc = _dot_nt(xb, w_scr[slot, cs, :]) + b_ref[:, cs]
        u_ref[:, cs] = acc.astype(u_ref.dtype)

    @pl.when(s == 0)
    def _():
        g_ref[...] = _dot_nt(xb, wg_ref[...].astype(BF16)) + bg_ref[...]

    @pl.when(s != 0)
    def _():
        g_ref[...] = jnp.zeros_like(g_ref)

    @pl.when(s + 1 < n_s)
    def _():
        chunk_copy(s + 1, i).wait()
        convert_into(1 - slot, i)


def _proj(x2d, w_t, b, *, gate_lo, n_gate, bm=1024, bn=2048):
    m, d = x2d.shape
    n = w_t.shape[0] - n_gate
    assert gate_lo % bn == 0 and n % bn == 0 and m % bm == 0
    assert n_gate % SUBLANES == 0 and n_gate <= GATE_PAD and gate_lo % GATE_PAD == 0
    gate_blk = gate_lo // bn
    n_i = m // bm
    n_j = n // bn
    assert bn % n_i == 0 and (bn // n_i) % SUBLANES == 0
    col = lambda s: _col_tile(s, gate_blk, n_j)
    b_main = jnp.concatenate([b[:, :gate_lo], b[:, gate_lo + n_gate:]], axis=1)
    b_gate = b[:, gate_lo:gate_lo + GATE_PAD]
    return pl.pallas_call(
        functools.partial(_proj_kernel, gate_blk, n_gate),
        out_shape=(jax.ShapeDtypeStruct((m, n), BF16),
                   jax.ShapeDtypeStruct(((n_i + 1) * bm, GATE_PAD), F32)),
        grid=(n_j, n_i),
        in_specs=[
            pl.BlockSpec((bm, d), lambda s, i: (i, 0)),
            pl.BlockSpec((GATE_PAD, d), lambda s, i: (gate_lo // GATE_PAD, 0)),
            pl.BlockSpec((1, bn), lambda s, i: (0, col(s))),
            pl.BlockSpec((1, GATE_PAD), lambda s, i: (0, 0)),
            pl.BlockSpec(memory_space=pl.ANY),
        ],
        out_specs=(
            pl.BlockSpec((bm, bn), lambda s, i: (i, col(s))),
            pl.BlockSpec((bm, GATE_PAD), lambda s, i: (jnp.where(s == 0, i, n_i), 0)),
        ),
        scratch_shapes=[
            pltpu.VMEM((2, bn, d), BF16),
            pltpu.VMEM((bn // n_i, d), F32),
            pltpu.SemaphoreType.DMA(()),
        ],
        compiler_params=pltpu.CompilerParams(
            dimension_semantics=("arbitrary", "arbitrary"), vmem_limit_bytes=58 * MIB),
        name="proj",
    )(x2d, w_t, b_main, b_gate, w_t)


def _conv_shift_matrix():
    L = CHUNK
    row = lax.broadcasted_iota(jnp.int32, (L, 2 * L), 0)
    col = lax.broadcasted_iota(jnp.int32, (L, 2 * L), 1)
    blocks = [jnp.where(col == row + (L - (CONV_K - 1) + j), 1.0, 0.0) for j in range(CONV_K)]
    return jnp.concatenate(blocks, axis=0).astype(BF16)


def _mlstm_chunk(q_ref, k_ref, v_ref, g_ref, cw_ref, cb_ref, out_ref, c_ref, n_ref, m_ref,
                 x2_ref, shift_ref):
    L = CHUNK

    x2_ref[L:2 * L, 0:M_QK_W] = q_ref[...]
    x2_ref[L:2 * L, M_QK_W:2 * M_QK_W] = k_ref[...]
    taps = _dot(shift_ref[...], x2_ref[...])
    conv = cb_ref[...]
    for j in range(CONV_K):
        conv = conv + cw_ref[j:j + 1, :] * taps[j * L:(j + 1) * L, :]
    x2_ref[0:L, :] = x2_ref[L:2 * L, :]
    qk = conv * _sigmoid(conv)

    g = g_ref[...]
    lfg = _log_sigmoid(g)
    g_rows = g.T[0:2 * M_HEADS, :]
    lf_rows = _log_sigmoid(g_rows)

    row_t = lax.broadcasted_iota(jnp.int32, (L, L), 0)
    col_s = lax.broadcasted_iota(jnp.int32, (L, L), 1)
    causal = col_s <= row_t
    lane = lax.broadcasted_iota(jnp.int32, (L, GATE_PAD), 1)

    heads = range(M_HEADS)
    qh = [qk[:, h * M_DQK:(h + 1) * M_DQK] * (M_DQK ** -0.5) for h in heads]
    kh = [qk[:, M_QK_W + h * M_DQK:M_QK_W + (h + 1) * M_DQK] for h in heads]
    qb = [q.astype(BF16) for q in qh]
    vb = [v_ref[:, h * M_DV:(h + 1) * M_DV] for h in heads]
    scores = [_dot_nt(qb[h], kh[h].astype(BF16)) for h in heads]
    inter = [_dot(qb[h], c_ref[h].astype(BF16)) for h in heads]

    gate = []
    for h in heads:
        li_row = g_rows[h:h + 1, :]
        lf_row = lf_rows[M_HEADS + h:M_HEADS + h + 1, :]
        li_col = jnp.sum(jnp.where(lane == h, g, 0.0), axis=1, keepdims=True)
        lf_col = jnp.sum(jnp.where(lane == M_HEADS + h, lfg, 0.0), axis=1, keepdims=True)
        b_col = jnp.sum(jnp.where(causal, lf_row, 0.0), axis=1, keepdims=True)
        b_row = jnp.sum(jnp.where(row_t <= col_s, lf_col, 0.0), axis=0, keepdims=True)
        g_tot = jnp.sum(lf_row, axis=1, keepdims=True)
        m_prev = m_ref[h:h + 1, 0:1]

        d_log = jnp.where(causal, b_col - b_row + li_row, -jnp.inf)
        m_inter = b_col + m_prev
        m_t = jnp.maximum(m_inter, jnp.max(d_log, axis=1, keepdims=True))
        w_intra = jnp.exp(d_log - m_t)
        w_inter = jnp.exp(m_inter - m_t)

        w_log_row = g_tot - b_row + li_row
        w_log_col = g_tot - b_col + li_col
        m_new = jnp.maximum(g_tot + m_prev, jnp.max(w_log_row, axis=1, keepdims=True))
        w_s = jnp.exp(w_log_col - m_new)
        decay = jnp.exp(g_tot + m_prev - m_new)
        gate.append((m_t, w_intra, w_inter, m_new, w_s, decay))

    for h in heads:
        m_t, w_intra, w_inter, m_new, w_s, decay = gate[h]
        n_old = n_ref[h:h + 1, :]
        s = scores[h] * w_intra
        num = _dot(s.astype(BF16), vb[h]) + w_inter * inter[h]
        den = (jnp.sum(s, axis=1, keepdims=True)
               + w_inter * jnp.sum(qh[h] * n_old, axis=1, keepdims=True))
        hh = num * (1.0 / jnp.maximum(jnp.abs(den), jnp.exp(-m_t)))

        kw = kh[h] * w_s
        c_ref[h] = decay * c_ref[h] + _dot(kw.T.astype(BF16), vb[h])
        n_ref[h:h + 1, :] = decay * n_old + jnp.sum(kw, axis=0, keepdims=True)
        m_ref[h:h + 1, :] = jnp.broadcast_to(m_new, (1, LANES))

        out_ref[:, h * M_DV:(h + 1) * M_DV] = hh.astype(out_ref.dtype)


def _retention_constants(decay_ref, xi_ref, zeta_ref):
    L = CHUNK
    t_col = lax.broadcasted_iota(jnp.int32, (L, 1), 0).astype(F32)
    rel = (lax.broadcasted_iota(jnp.int32, (L, L), 0)
           - lax.broadcasted_iota(jnp.int32, (L, L), 1)).astype(F32)
    scale = R_DQK ** -0.5
    for h in range(R_HEADS):
        log_gamma = math.log1p(-(2.0 ** (-5.0 - h)))
        decay_ref[h] = jnp.where(rel >= 0.0, jnp.exp(jnp.maximum(rel, 0.0) * log_gamma), 0.0) * scale
        xi_ref[h] = jnp.exp((t_col + 1.0) * log_gamma) * scale
        zeta_ref[h] = jnp.exp((L - 1.0 - t_col) * log_gamma)


def _retention_chunk(q_ref, k_ref, v_ref, pos_ref, invf_ref, sign_ref, out_ref, r_ref,
                     decay_ref, xi_ref, zeta_ref):
    L = CHUNK
    half = R_DQK // 2
    ang = pos_ref[...] * invf_ref[...]
    lane = lax.broadcasted_iota(jnp.int32, (L // 2, R_DQK), 1)

    def unpack(packed):
        swapped = pltpu.roll(packed, half, 1)
        top = jnp.where(lane < half, packed, swapped)
        bottom = jnp.where(lane < half, swapped, packed)
        return jnp.concatenate([top, bottom], axis=0)

    cos = unpack(jnp.cos(ang))
    sin = unpack(jnp.sin(ang)) * sign_ref[...]

    heads = range(R_HEADS)
    qr, kr = [], []
    for h in heads:
        qs = slice(h * R_DQK, (h + 1) * R_DQK)
        xq = q_ref[:, qs].astype(F32)
        xk = k_ref[:, qs].astype(F32)
        qr.append((xq * cos + pltpu.roll(xq, half, 1) * sin).astype(BF16))
        kr.append(xk * cos + pltpu.roll(xk, half, 1) * sin)
    vb = [v_ref[:, h * R_DV:(h + 1) * R_DV] for h in heads]
    scores = [_dot_nt(qr[h], kr[h].astype(BF16)) for h in heads]
    cross = [_dot(qr[h], r_ref[h].astype(BF16)) for h in heads]

    for h in heads:
        g_chunk = math.exp(L * math.log1p(-(2.0 ** (-5.0 - h))))
        p = scores[h] * decay_ref[h]
        o = _dot(p.astype(BF16), vb[h]) + cross[h] * xi_ref[h]
        r_ref[h] = g_chunk * r_ref[h] + _dot((kr[h] * zeta_ref[h]).T.astype(BF16), vb[h])
        out_ref[:, h * R_DV:(h + 1) * R_DV] = o.astype(out_ref.dtype)


def _rec_kernel(n_chunks, mq_ref, mk_ref, mv_ref, g_ref, cw_ref, cb_ref,
                rq_ref, rk_ref, rv_ref, pos_ref, invf_ref, sign_ref,
                hm_ref, hr_ref, c_ref, n_ref, m_ref, x2_ref, shift_ref, r_ref, decay_ref, xi_ref, zeta_ref):
    @pl.when(pl.program_id(0) == 0)
    def _():
        shift_ref[...] = _conv_shift_matrix()
        _retention_constants(decay_ref, xi_ref, zeta_ref)

    @pl.when(lax.rem(pl.program_id(0), n_chunks) == 0)
    def _():
        c_ref[...] = jnp.zeros_like(c_ref)
        n_ref[...] = jnp.zeros_like(n_ref)
        m_ref[...] = jnp.zeros_like(m_ref)
        r_ref[...] = jnp.zeros_like(r_ref)
        x2_ref[0:CHUNK, :] = jnp.zeros((CHUNK, 2 * M_QK_W), BF16)

    _mlstm_chunk(mq_ref, mk_ref, mv_ref, g_ref, cw_ref, cb_ref, hm_ref, c_ref, n_ref, m_ref,
                 x2_ref, shift_ref)
    _retention_chunk(rq_ref, rk_ref, rv_ref, pos_ref, invf_ref, sign_ref, hr_ref, r_ref,
                     decay_ref, xi_ref, zeta_ref)


def _rec(u, gates, pos_p, conv_w, conv_b, inv_freq, sign, *, batch, n_chunks):
    L = CHUNK
    total = batch * n_chunks
    r_q = (2 * M_QK_W + 2 * M_V_W) // R_QK_W
    r_v = (2 * M_QK_W + 2 * M_V_W + 2 * R_QK_W) // R_V_W
    const = lambda shape: pl.BlockSpec(shape, lambda t: (0, 0))
    return pl.pallas_call(
        functools.partial(_rec_kernel, n_chunks),
        out_shape=(jax.ShapeDtypeStruct((total * L, M_V_W), BF16),
                   jax.ShapeDtypeStruct((total * L, R_V_W), BF16)),
        grid=(total,),
        in_specs=[
            pl.BlockSpec((L, M_QK_W), lambda t: (t, 0)),
            pl.BlockSpec((L, M_QK_W), lambda t: (t, 1)),
            pl.BlockSpec((L, M_V_W), lambda t: (t, 1)),
            pl.BlockSpec((L, GATE_PAD), lambda t: (t, 0)),
            const((CONV_K, 2 * M_QK_W)),
            const((1, 2 * M_QK_W)),
            pl.BlockSpec((L, R_QK_W), lambda t: (t, r_q)),
            pl.BlockSpec((L, R_QK_W), lambda t: (t, r_q + 1)),
            pl.BlockSpec((L, R_V_W), lambda t: (t, r_v)),
            pl.BlockSpec((L // 2, R_DQK), lambda t: (t, 0)),
            const((1, R_DQK)),
            const((1, R_DQK)),
        ],
        out_specs=(pl.BlockSpec((L, M_V_W), lambda t: (t, 0)),
                   pl.BlockSpec((L, R_V_W), lambda t: (t, 0))),
        scratch_shapes=[
            pltpu.VMEM((M_HEADS, M_DQK, M_DV), F32),
            pltpu.VMEM((SUBLANES, M_DQK), F32),
            pltpu.VMEM((SUBLANES, LANES), F32),
            pltpu.VMEM((2 * L, 2 * M_QK_W), BF16),
            pltpu.VMEM((CONV_K * L, 2 * L), BF16),
            pltpu.VMEM((R_HEADS, R_DQK, R_DV), F32),
            pltpu.VMEM((R_HEADS, L, L), F32),
            pltpu.VMEM((R_HEADS, L, 1), F32),
            pltpu.VMEM((R_HEADS, L, 1), F32),
        ],
        compiler_params=pltpu.CompilerParams(
            dimension_semantics=("arbitrary",), vmem_limit_bytes=40 * MIB),
        name="rec",
    )(u, u, u, gates, conv_w, conv_b, u, u, u, pos_p, inv_freq, sign)


def _load_weight_bf16(w_hbm, w_scr, stage, sem):
    n_slots, rows, _ = stage.shape
    n = w_hbm.shape[0] // rows

    def copy(c, slot):
        src = w_hbm.at[pl.ds(pl.multiple_of(c * rows, rows), rows), :]
        return pltpu.make_async_copy(src, stage.at[slot], sem.at[slot])

    copy(0, 0).start()

    def body(c, carry):
        slot = lax.rem(c, n_slots)

        @pl.when(c + 1 < n)
        def _():
            copy(c + 1, lax.rem(c + 1, n_slots)).start()

        copy(c, slot).wait()
        w_scr[pl.ds(pl.multiple_of(c * rows, rows), rows), :] = stage[slot].astype(BF16)
        return carry

    lax.fori_loop(0, n, body, 0)


def _branch_kernel(hh_ref, ho_ref, mo_ref, rg_ref, ga_ref, gb_ref, mng_ref, rng_ref, wm_hbm, wr_hbm,
                   out_ref, hm_scr, hr_scr, wm_ref, wr_ref, stage_ref, sem_ref):
    @pl.when(pl.program_id(0) == 0)
    def _():
        _load_weight_bf16(wm_hbm, wm_ref, stage_ref, sem_ref)
        _load_weight_bf16(wr_hbm, wr_ref, stage_ref, sem_ref)

    for h in range(M_HEADS):
        sl = slice(h * M_DV, (h + 1) * M_DV)
        y = _head_norm(hh_ref[:, sl].astype(F32), mng_ref[:, sl]) * _sigmoid(mo_ref[:, sl].astype(F32))
        hm_scr[:, sl] = y.astype(BF16)
    for h in range(R_HEADS):
        sl = slice(h * R_DV, (h + 1) * R_DV)
        sg = rg_ref[:, sl].astype(F32)
        y = _head_norm(ho_ref[:, sl].astype(F32), rng_ref[:, sl]) * (sg * _sigmoid(sg))
        hr_scr[:, sl] = y.astype(BF16)

    ya = _dot(hm_scr[...], wm_ref[...])
    yb = _dot(hr_scr[...], wr_ref[...])
    merged = _sigmoid(ga_ref[...].astype(F32)) * ya + _sigmoid(gb_ref[...].astype(F32)) * yb
    out_ref[...] = merged.astype(out_ref.dtype)


def _branch(hh, ho, u, m_norm_g, r_norm_g, wm, wr, *, bm=256, stage_rows=256):
    m, d = hh.shape[0], wm.shape[1]
    o_blk = (2 * M_QK_W + M_V_W) // M_V_W
    s_blk = (2 * M_QK_W + 2 * M_V_W + 2 * R_QK_W + R_V_W) // R_V_W
    g_a = (2 * M_QK_W + 2 * M_V_W + 2 * R_QK_W + 2 * R_V_W) // d
    assert wm.shape == (M_V_W, d) and wr.shape == (R_V_W, d) and M_V_W == R_V_W == d
    assert m % bm == 0 and d % stage_rows == 0
    const = lambda shape: pl.BlockSpec(shape, lambda i: (0, 0))
    in_hbm = pl.BlockSpec(memory_space=pl.ANY)
    return pl.pallas_call(
        _branch_kernel,
        out_shape=jax.ShapeDtypeStruct((m, d), BF16),
        grid=(m // bm,),
        in_specs=[
            pl.BlockSpec((bm, M_V_W), lambda i: (i, 0)),
            pl.BlockSpec((bm, R_V_W), lambda i: (i, 0)),
            pl.BlockSpec((bm, M_V_W), lambda i: (i, o_blk)),
            pl.BlockSpec((bm, R_V_W), lambda i: (i, s_blk)),
            pl.BlockSpec((bm, d), lambda i: (i, g_a)),
            pl.BlockSpec((bm, d), lambda i: (i, g_a + 1)),
            const((1, M_V_W)),
            const((1, R_V_W)),
            in_hbm,
            in_hbm,
        ],
        out_specs=pl.BlockSpec((bm, d), lambda i: (i, 0)),
        scratch_shapes=[
            pltpu.VMEM((bm, M_V_W), BF16),
            pltpu.VMEM((bm, R_V_W), BF16),
            pltpu.VMEM((M_V_W, d), BF16),
            pltpu.VMEM((R_V_W, d), BF16),
            pltpu.VMEM((2, stage_rows, d), F32),
            pltpu.SemaphoreType.DMA((2,)),
        ],
        compiler_params=pltpu.CompilerParams(
            dimension_semantics=("arbitrary",), vmem_limit_bytes=52 * MIB),
        name="branch",
    )(hh, ho, u, u, u, u, m_norm_g, r_norm_g, wm, wr)


def _outproj_kernel(alpha, mg_ref, x_ref, wo_hbm, bo_ref, lg_ref, lb_ref, out_ref, wo_ref, stage_ref, sem_ref):
    @pl.when(pl.program_id(0) == 0)
    def _():
        _load_weight_bf16(wo_hbm, wo_ref, stage_ref, sem_ref)

    y = _dot(mg_ref[...], wo_ref[...]) + bo_ref[...]
    out_ref[...] = _layer_norm(alpha * x_ref[...] + y, lg_ref[...], lb_ref[...])


def _outproj(merged, x2d, wo, bo, ln_g, ln_b, *, alpha, bm=512, stage_rows=256):
    m, d = x2d.shape
    assert wo.shape == (d, d) and m % bm == 0 and d % stage_rows == 0
    const = lambda shape: pl.BlockSpec(shape, lambda i: (0, 0))
    return pl.pallas_call(
        functools.partial(_outproj_kernel, alpha),
        out_shape=jax.ShapeDtypeStruct((m, d), F32),
        grid=(m // bm,),
        in_specs=[
            pl.BlockSpec((bm, d), lambda i: (i, 0)),
            pl.BlockSpec((bm, d), lambda i: (i, 0)),
            pl.BlockSpec(memory_space=pl.ANY),
            const((1, d)),
            const((1, d)),
            const((1, d)),
        ],
        out_specs=pl.BlockSpec((bm, d), lambda i: (i, 0)),
        scratch_shapes=[
            pltpu.VMEM((d, d), BF16),
            pltpu.VMEM((2, stage_rows, d), F32),
            pltpu.SemaphoreType.DMA((2,)),
        ],
        compiler_params=pltpu.CompilerParams(
            dimension_semantics=("arbitrary",), vmem_limit_bytes=48 * MIB),
        name="outproj",
    )(merged, x2d, wo, bo, ln_g, ln_b)


def _mlp_kernel(alpha, x_ref, w1_ref, b1_ref, w2_ref, b2_ref, lg_ref, lb_ref, out_ref, xb_ref):
    j = pl.program_id(1)

    @pl.when(j == 0)
    def _():
        xb_ref[...] = x_ref[...].astype(BF16)
        out_ref[...] = jnp.zeros_like(out_ref)

    hid = jnp.maximum(_dot(xb_ref[...], w1_ref[...]) + b1_ref[...], 0.0)
    out_ref[...] += _dot((hid * hid).astype(BF16), w2_ref[...])

    @pl.when(j == pl.num_programs(1) - 1)
    def _():
        z = alpha * x_ref[...] + (out_ref[...] + b2_ref[...])
        out_ref[...] = _layer_norm(z, lg_ref[...], lb_ref[...])


def _mlp(x2d, w1, b1, w2, b2, ln_g, ln_b, *, alpha, bm=512, bf=1024):
    m, d = x2d.shape
    f = w1.shape[1]
    return pl.pallas_call(
        functools.partial(_mlp_kernel, alpha),
        out_shape=jax.ShapeDtypeStruct((m, d), F32),
        grid=(m // bm, f // bf),
        in_specs=[
            pl.BlockSpec((bm, d), lambda i, j: (i, 0)),
            pl.BlockSpec((d, bf), lambda i, j: (0, j)),
            pl.BlockSpec((1, bf), lambda i, j: (0, j)),
            pl.BlockSpec((bf, d), lambda i, j: (j, 0)),
            pl.BlockSpec((1, d), lambda i, j: (0, 0)),
            pl.BlockSpec((1, d), lambda i, j: (0, 0)),
            pl.BlockSpec((1, d), lambda i, j: (0, 0)),
        ],
        out_specs=pl.BlockSpec((bm, d), lambda i, j: (i, 0)),
        scratch_shapes=[pltpu.VMEM((bm, d), BF16)],
        compiler_params=pltpu.CompilerParams(
            dimension_semantics=("arbitrary", "arbitrary"), vmem_limit_bytes=48 * MIB),
        name="mlp",
    )(x2d, w1, b1, w2, b2, ln_g, ln_b)


def kernel(x, positions, w_in, b_in, m_conv_w, m_conv_b, m_norm_g, r_norm_g, w_branch_m, w_branch_r,
           w_out, b_out, ln1_g, ln1_b, w_ff1, b_ff1, w_ff2, b_ff2, ln2_g, ln2_b):
    batch, seq, d = x.shape
    depth = w_in.shape[0]
    n_chunks = seq // CHUNK
    alpha = (2.0 * depth) ** 0.25
    gate_lo = 2 * M_QK_W + 2 * M_V_W

    half = R_DQK // 2
    inv_freq = ROPE_BASE ** (-jnp.arange(half, dtype=F32) / half)
    inv_freq = jnp.concatenate([inv_freq, inv_freq]).reshape(1, R_DQK)
    sign = jnp.concatenate([-jnp.ones((half,), F32), jnp.ones((half,), F32)]).reshape(1, R_DQK)
    pos_p = positions.astype(F32).reshape(batch * n_chunks, 2, CHUNK // 2).transpose(0, 2, 1)
    pos_p = jnp.repeat(pos_p, half, axis=2).reshape(batch * n_chunks * (CHUNK // 2), R_DQK)

    h = x.reshape(batch * seq, d)
    for l in range(depth):
        u, gates = _proj(h, w_in[l].T, b_in[l].reshape(1, -1),
                         gate_lo=gate_lo, n_gate=2 * M_HEADS)
        hh, ho = _rec(u, gates, pos_p, m_conv_w[l], m_conv_b[l].reshape(1, -1), inv_freq, sign,
                      batch=batch, n_chunks=n_chunks)
        merged = _branch(hh, ho, u, m_norm_g[l].reshape(1, -1), r_norm_g[l].reshape(1, -1),
                         w_branch_m[l], w_branch_r[l])
        h1 = _outproj(merged, h, w_out[l], b_out[l].reshape(1, -1), ln1_g[l].reshape(1, -1),
                      ln1_b[l].reshape(1, -1), alpha=alpha)
        h = _mlp(h1, w_ff1[l].astype(BF16), b_ff1[l].reshape(1, -1), w_ff2[l].astype(BF16),
                 b_ff2[l].reshape(1, -1), ln2_g[l].reshape(1, -1), ln2_b[l].reshape(1, -1), alpha=alpha)
    return h.reshape(batch, seq, d)
```

```python
import functools
import math

import jax
import jax.numpy as jnp
from jax import lax
from jax.experimental import pallas as pl
from jax.experimental.pallas import tpu as pltpu

M_HEADS, M_DQK, M_DV = 4, 256, 512
R_HEADS, R_DQK, R_DV = 8, 128, 256
CONV_K = 4
CHUNK = 128
ROPE_BASE = 10000.0
LN_EPS = 1e-5
M_QK_W = M_HEADS * M_DQK
M_V_W = M_HEADS * M_DV
R_QK_W = R_HEADS * R_DQK
R_V_W = R_HEADS * R_DV

LANES = 128
SUBLANES = 8
GATE_PAD = LANES
MIB = 1024 * 1024

F32 = jnp.float32
BF16 = jnp.bfloat16

def _sigmoid(x):
    return 1.0 / (1.0 + jnp.exp(-x))


def _log_sigmoid(x):
    return jnp.minimum(x, 0.0) - jnp.log1p(jnp.exp(-jnp.abs(x)))


def _dot(a, b):
    return jnp.dot(a, b, preferred_element_type=F32)


def _dot_nt(a, b):
    return lax.dot_general(a, b, (((1,), (1,)), ((), ())), preferred_element_type=F32)


def _layer_norm(z, g, b):
    mu = jnp.mean(z, axis=-1, keepdims=True)
    d = z - mu
    var = jnp.mean(d * d, axis=-1, keepdims=True)
    return d * lax.rsqrt(var + LN_EPS) * g + b


def _head_norm(h, g):
    mu = jnp.mean(h, axis=-1, keepdims=True)
    d = h - mu
    var = jnp.mean(d * d, axis=-1, keepdims=True)
    return d * lax.rsqrt(var + LN_EPS) * g


def _col_tile(step, gate_blk, n_tiles):
    return lax.rem(step + gate_blk, n_tiles)


def _proj_kernel(gate_blk, n_gate, x_ref, wa_ref, wb_ref, ba_ref, bb_ref, u_ref, g_ref, w_scr, b_scr):
    j = _col_tile(pl.program_id(0), gate_blk, pl.num_programs(0))
    bn, d = wa_ref.shape
    rows = 128

    @pl.when(pl.program_id(1) == 0)
    def _():
        @pl.when(j < gate_blk)
        def _():
            for r in range(0, bn, rows):
                w_scr[r:r + rows, :] = wa_ref[r:r + rows, :].astype(BF16)
            b_scr[...] = ba_ref[...]

        @pl.when(j >= gate_blk)
        def _():
            for r in range(0, bn, rows):
                lo, hi = r + n_gate, r + n_gate + rows
                if hi <= bn:
                    blk = wa_ref[lo:hi, :]
                else:
                    blk = jnp.concatenate([wa_ref[lo:bn, :], wb_ref[0:hi - bn, :]], axis=0)
                w_scr[r:r + rows, :] = blk.astype(BF16)
            b_scr[...] = jnp.concatenate([ba_ref[...], bb_ref[...]], axis=1)[:, n_gate:n_gate + bn]

    xb = x_ref[...].astype(BF16)
    u_ref[...] = (_dot_nt(xb, w_scr[...]) + b_scr[...]).astype(u_ref.dtype)

    @pl.when(j == gate_blk)
    def _():
        g_ref[...] = _dot_nt(xb, wa_ref[0:GATE_PAD, :].astype(BF16)) + ba_ref[:, 0:GATE_PAD]

    @pl.when(j != gate_blk)
    def _():
        g_ref[...] = jnp.zeros_like(g_ref)


def _proj(x2d, w_t, b, *, gate_lo, n_gate, bm=1024, bn=1024):
    m, d = x2d.shape
    n = w_t.shape[0] - n_gate
    assert gate_lo % bn == 0 and n % bn == 0 and m % bm == 0
    assert n_gate % SUBLANES == 0 and n_gate <= LANES
    gate_blk = gate_lo // bn
    n_i = m // bm
    n_j = n // bn
    col = lambda s: _col_tile(s, gate_blk, n_j)
    return pl.pallas_call(
        functools.partial(_proj_kernel, gate_blk, n_gate),
        out_shape=(jax.ShapeDtypeStruct((m, n), BF16),
                   jax.ShapeDtypeStruct(((n_i + 1) * bm, GATE_PAD), F32)),
        grid=(n_j, n_i),
        in_specs=[
            pl.BlockSpec((bm, d), lambda s, i: (i, 0)),
            pl.BlockSpec((bn, d), lambda s, i: (col(s), 0)),
            pl.BlockSpec((n_gate, d), lambda s, i: ((col(s) + 1) * (bn // n_gate), 0)),
            pl.BlockSpec((1, bn), lambda s, i: (0, col(s))),
            pl.BlockSpec((1, LANES), lambda s, i: (0, (col(s) + 1) * (bn // LANES))),
        ],
        out_specs=(
            pl.BlockSpec((bm, bn), lambda s, i: (i, col(s))),
            pl.BlockSpec((bm, GATE_PAD), lambda s, i: (jnp.where(s == 0, i, n_i), 0)),
        ),
        scratch_shapes=[pltpu.VMEM((bn, d), BF16), pltpu.VMEM((1, bn), F32)],
        compiler_params=pltpu.CompilerParams(
            dimension_semantics=("arbitrary", "arbitrary"), vmem_limit_bytes=48 * MIB),
        name="proj",
    )(x2d, w_t, w_t, b, b)


def _conv_shift_matrix():
    L = CHUNK
    row = lax.broadcasted_iota(jnp.int32, (L, 2 * L), 0)
    col = lax.broadcasted_iota(jnp.int32, (L, 2 * L), 1)
    blocks = [jnp.where(col == row + (L - (CONV_K - 1) + j), 1.0, 0.0) for j in range(CONV_K)]
    return jnp.concatenate(blocks, axis=0).astype(BF16)


def _mlstm_chunk(q_ref, k_ref, v_ref, g_ref, cw_ref, cb_ref, out_ref, c_ref, n_ref, m_ref,
                 x2_ref, shift_ref):
    L = CHUNK

    x2_ref[L:2 * L, 0:M_QK_W] = q_ref[...]
    x2_ref[L:2 * L, M_QK_W:2 * M_QK_W] = k_ref[...]
    taps = _dot(shift_ref[...], x2_ref[...])
    conv = cb_ref[...]
    for j in range(CONV_K):
        conv = conv + cw_ref[j:j + 1, :] * taps[j * L:(j + 1) * L, :]
    x2_ref[0:L, :] = x2_ref[L:2 * L, :]
    qk = conv * _sigmoid(conv)

    g = g_ref[...]
    lfg = _log_sigmoid(g)
    g_rows = g.T[0:2 * M_HEADS, :]
    lf_rows = _log_sigmoid(g_rows)

    row_t = lax.broadcasted_iota(jnp.int32, (L, L), 0)
    col_s = lax.broadcasted_iota(jnp.int32, (L, L), 1)
    causal = col_s <= row_t
    lane = lax.broadcasted_iota(jnp.int32, (L, GATE_PAD), 1)

    heads = range(M_HEADS)
    qh = [qk[:, h * M_DQK:(h + 1) * M_DQK] * (M_DQK ** -0.5) for h in heads]
    kh = [qk[:, M_QK_W + h * M_DQK:M_QK_W + (h + 1) * M_DQK] for h in heads]
    qb = [q.astype(BF16) for q in qh]
    vb = [v_ref[:, h * M_DV:(h + 1) * M_DV] for h in heads]
    scores = [_dot_nt(qb[h], kh[h].astype(BF16)) for h in heads]
    inter = [_dot(qb[h], c_ref[h].astype(BF16)) for h in heads]

    gate = []
    for h in heads:
        li_row = g_rows[h:h + 1, :]
        lf_row = lf_rows[M_HEADS + h:M_HEADS + h + 1, :]
        li_col = jnp.sum(jnp.where(lane == h, g, 0.0), axis=1, keepdims=True)
        lf_col = jnp.sum(jnp.where(lane == M_HEADS + h, lfg, 0.0), axis=1, keepdims=True)
        b_col = jnp.sum(jnp.where(causal, lf_row, 0.0), axis=1, keepdims=True)
        b_row = jnp.sum(jnp.where(row_t <= col_s, lf_col, 0.0), axis=0, keepdims=True)
        g_tot = jnp.sum(lf_row, axis=1, keepdims=True)
        m_prev = m_ref[h:h + 1, 0:1]

        d_log = jnp.where(causal, b_col - b_row + li_row, -jnp.inf)
        m_inter = b_col + m_prev
        m_t = jnp.maximum(m_inter, jnp.max(d_log, axis=1, keepdims=True))
        w_intra = jnp.exp(d_log - m_t)
        w_inter = jnp.exp(m_inter - m_t)

        w_log_row = g_tot - b_row + li_row
        w_log_col = g_tot - b_col + li_col
        m_new = jnp.maximum(g_tot + m_prev, jnp.max(w_log_row, axis=1, keepdims=True))
        w_s = jnp.exp(w_log_col - m_new)
        decay = jnp.exp(g_tot + m_prev - m_new)
        gate.append((m_t, w_intra, w_inter, m_new, w_s, decay))

    for h in heads:
        m_t, w_intra, w_inter, m_new, w_s, decay = gate[h]
        n_old = n_ref[h:h + 1, :]
        s = scores[h] * w_intra
        num = _dot(s.astype(BF16), vb[h]) + w_inter * inter[h]
        den = (jnp.sum(s, axis=1, keepdims=True)
               + w_inter * jnp.sum(qh[h] * n_old, axis=1, keepdims=True))
        hh = num * (1.0 / jnp.maximum(jnp.abs(den), jnp.exp(-m_t)))

        kw = kh[h] * w_s
        c_ref[h] = decay * c_ref[h] + _dot(kw.T.astype(BF16), vb[h])
        n_ref[h:h + 1, :] = decay * n_old + jnp.sum(kw, axis=0, keepdims=True)
        m_ref[h:h + 1, :] = jnp.broadcast_to(m_new, (1, LANES))

        out_ref[:, h * M_DV:(h + 1) * M_DV] = hh.astype(out_ref.dtype)


def _retention_constants(decay_ref, xi_ref, zeta_ref):
    L = CHUNK
    t_col = lax.broadcasted_iota(jnp.int32, (L, 1), 0).astype(F32)
    rel = (lax.broadcasted_iota(jnp.int32, (L, L), 0)
           - lax.broadcasted_iota(jnp.int32, (L, L), 1)).astype(F32)
    scale = R_DQK ** -0.5
    for h in range(R_HEADS):
        log_gamma = math.log1p(-(2.0 ** (-5.0 - h)))
        decay_ref[h] = jnp.where(rel >= 0.0, jnp.exp(jnp.maximum(rel, 0.0) * log_gamma), 0.0) * scale
        xi_ref[h] = jnp.exp((t_col + 1.0) * log_gamma) * scale
        zeta_ref[h] = jnp.exp((L - 1.0 - t_col) * log_gamma)


def _retention_chunk(q_ref, k_ref, v_ref, pos_ref, invf_ref, sign_ref, out_ref, r_ref,
                     decay_ref, xi_ref, zeta_ref):
    L = CHUNK
    half = R_DQK // 2
    ang = pos_ref[...] * invf_ref[...]
    lane = lax.broadcasted_iota(jnp.int32, (L // 2, R_DQK), 1)

    def unpack(packed):
        swapped = pltpu.roll(packed, half, 1)
        top = jnp.where(lane < half, packed, swapped)
        bottom = jnp.where(lane < half, swapped, packed)
        return jnp.concatenate([top, bottom], axis=0)

    cos = unpack(jnp.cos(ang))
    sin = unpack(jnp.sin(ang)) * sign_ref[...]

    heads = range(R_HEADS)
    qr, kr = [], []
    for h in heads:
        qs = slice(h * R_DQK, (h + 1) * R_DQK)
        xq = q_ref[:, qs].astype(F32)
        xk = k_ref[:, qs].astype(F32)
        qr.append((xq * cos + pltpu.roll(xq, half, 1) * sin).astype(BF16))
        kr.append(xk * cos + pltpu.roll(xk, half, 1) * sin)
    vb = [v_ref[:, h * R_DV:(h + 1) * R_DV] for h in heads]
    scores = [_dot_nt(qr[h], kr[h].astype(BF16)) for h in heads]
    cross = [_dot(qr[h], r_ref[h].astype(BF16)) for h in heads]

    for h in heads:
        g_chunk = math.exp(L * math.log1p(-(2.0 ** (-5.0 - h))))
        p = scores[h] * decay_ref[h]
        o = _dot(p.astype(BF16), vb[h]) + cross[h] * xi_ref[h]
        r_ref[h] = g_chunk * r_ref[h] + _dot((kr[h] * zeta_ref[h]).T.astype(BF16), vb[h])
        out_ref[:, h * R_DV:(h + 1) * R_DV] = o.astype(out_ref.dtype)


def _rec_kernel(n_chunks, n_cast, mq_ref, mk_ref, mv_ref, g_ref, cw_ref, cb_ref,
                rq_ref, rk_ref, rv_ref, pos_ref, invf_ref, sign_ref, *refs):
    cast_in, refs = refs[:n_cast], refs[n_cast:]
    hm_ref, hr_ref = refs[:2]
    cast_out, refs = refs[2:2 + n_cast], refs[2 + n_cast:]
    c_ref, n_ref, m_ref, x2_ref, shift_ref, r_ref, decay_ref, xi_ref, zeta_ref = refs

    for src, dst in zip(cast_in, cast_out):
        dst[...] = src[...].astype(BF16)

    @pl.when(pl.program_id(0) == 0)
    def _():
        shift_ref[...] = _conv_shift_matrix()
        _retention_constants(decay_ref, xi_ref, zeta_ref)

    @pl.when(lax.rem(pl.program_id(0), n_chunks) == 0)
    def _():
        c_ref[...] = jnp.zeros_like(c_ref)
        n_ref[...] = jnp.zeros_like(n_ref)
        m_ref[...] = jnp.zeros_like(m_ref)
        r_ref[...] = jnp.zeros_like(r_ref)
        x2_ref[0:CHUNK, :] = jnp.zeros((CHUNK, 2 * M_QK_W), BF16)

    _mlstm_chunk(mq_ref, mk_ref, mv_ref, g_ref, cw_ref, cb_ref, hm_ref, c_ref, n_ref, m_ref,
                 x2_ref, shift_ref)
    _retention_chunk(rq_ref, rk_ref, rv_ref, pos_ref, invf_ref, sign_ref, hr_ref, r_ref,
                     decay_ref, xi_ref, zeta_ref)


def _rec(u, gates, pos_p, conv_w, conv_b, inv_freq, sign, *, batch, n_chunks, to_bf16=()):
    L = CHUNK
    total = batch * n_chunks
    assert all(w.ndim == 2 and w.shape[0] % (total * 2 * SUBLANES) == 0 for w in to_bf16)
    slab = lambda w: pl.BlockSpec((w.shape[0] // total, w.shape[1]), lambda t: (t, 0))
    r_q = (2 * M_QK_W + 2 * M_V_W) // R_QK_W
    r_v = (2 * M_QK_W + 2 * M_V_W + 2 * R_QK_W) // R_V_W
    const = lambda shape: pl.BlockSpec(shape, lambda t: (0, 0))
    return pl.pallas_call(
        functools.partial(_rec_kernel, n_chunks, len(to_bf16)),
        out_shape=(jax.ShapeDtypeStruct((total * L, M_V_W), BF16),
                   jax.ShapeDtypeStruct((total * L, R_V_W), BF16),
                   *[jax.ShapeDtypeStruct(w.shape, BF16) for w in to_bf16]),
        grid=(total,),
        in_specs=[
            pl.BlockSpec((L, M_QK_W), lambda t: (t, 0)),
            pl.BlockSpec((L, M_QK_W), lambda t: (t, 1)),
            pl.BlockSpec((L, M_V_W), lambda t: (t, 1)),
            pl.BlockSpec((L, GATE_PAD), lambda t: (t, 0)),
            const((CONV_K, 2 * M_QK_W)),
            const((1, 2 * M_QK_W)),
            pl.BlockSpec((L, R_QK_W), lambda t: (t, r_q)),
            pl.BlockSpec((L, R_QK_W), lambda t: (t, r_q + 1)),
            pl.BlockSpec((L, R_V_W), lambda t: (t, r_v)),
            pl.BlockSpec((L // 2, R_DQK), lambda t: (t, 0)),
            const((1, R_DQK)),
            const((1, R_DQK)),
            *[slab(w) for w in to_bf16],
        ],
        out_specs=(pl.BlockSpec((L, M_V_W), lambda t: (t, 0)),
                   pl.BlockSpec((L, R_V_W), lambda t: (t, 0)),
                   *[slab(w) for w in to_bf16]),
        scratch_shapes=[
            pltpu.VMEM((M_HEADS, M_DQK, M_DV), F32),
            pltpu.VMEM((SUBLANES, M_DQK), F32),
            pltpu.VMEM((SUBLANES, LANES), F32),
            pltpu.VMEM((2 * L, 2 * M_QK_W), BF16),
            pltpu.VMEM((CONV_K * L, 2 * L), BF16),
            pltpu.VMEM((R_HEADS, R_DQK, R_DV), F32),
            pltpu.VMEM((R_HEADS, L, L), F32),
            pltpu.VMEM((R_HEADS, L, 1), F32),
            pltpu.VMEM((R_HEADS, L, 1), F32),
        ],
        compiler_params=pltpu.CompilerParams(
            dimension_semantics=("arbitrary",), vmem_limit_bytes=40 * MIB),
        name="rec",
    )(u, u, u, gates, conv_w, conv_b, u, u, u, pos_p, inv_freq, sign, *to_bf16)


def _load_weight_bf16(w_hbm, w_scr, stage, sem):
    n_slots, rows, _ = stage.shape
    n = w_hbm.shape[0] // rows

    def copy(c, slot):
        src = w_hbm.at[pl.ds(pl.multiple_of(c * rows, rows), rows), :]
        return pltpu.make_async_copy(src, stage.at[slot], sem.at[slot])

    copy(0, 0).start()

    def body(c, carry):
        slot = lax.rem(c, n_slots)

        @pl.when(c + 1 < n)
        def _():
            copy(c + 1, lax.rem(c + 1, n_slots)).start()

        copy(c, slot).wait()
        w_scr[pl.ds(pl.multiple_of(c * rows, rows), rows), :] = stage[slot].astype(BF16)
        return carry

    lax.fori_loop(0, n, body, 0)


def _branch_kernel(hh_ref, ho_ref, mo_ref, rg_ref, ga_ref, gb_ref, mng_ref, rng_ref, wm_hbm, wr_hbm,
                   out_ref, hm_scr, hr_scr, wm_ref, wr_ref, stage_ref, sem_ref):
    @pl.when(pl.program_id(0) == 0)
    def _():
        _load_weight_bf16(wm_hbm, wm_ref, stage_ref, sem_ref)
        _load_weight_bf16(wr_hbm, wr_ref, stage_ref, sem_ref)

    for h in range(M_HEADS):
        sl = slice(h * M_DV, (h + 1) * M_DV)
        y = _head_norm(hh_ref[:, sl].astype(F32), mng_ref[:, sl]) * _sigmoid(mo_ref[:, sl].astype(F32))
        hm_scr[:, sl] = y.astype(BF16)
    for h in range(R_HEADS):
        sl = slice(h * R_DV, (h + 1) * R_DV)
        sg = rg_ref[:, sl].astype(F32)
        y = _head_norm(ho_ref[:, sl].astype(F32), rng_ref[:, sl]) * (sg * _sigmoid(sg))
        hr_scr[:, sl] = y.astype(BF16)

    ya = _dot(hm_scr[...], wm_ref[...])
    yb = _dot(hr_scr[...], wr_ref[...])
    merged = _sigmoid(ga_ref[...].astype(F32)) * ya + _sigmoid(gb_ref[...].astype(F32)) * yb
    out_ref[...] = merged.astype(out_ref.dtype)


def _branch(hh, ho, u, m_norm_g, r_norm_g, wm, wr, *, bm=256, stage_rows=256):
    m, d = hh.shape[0], wm.shape[1]
    o_blk = (2 * M_QK_W + M_V_W) // M_V_W
    s_blk = (2 * M_QK_W + 2 * M_V_W + 2 * R_QK_W + R_V_W) // R_V_W
    g_a = (2 * M_QK_W + 2 * M_V_W + 2 * R_QK_W + 2 * R_V_W) // d
    assert wm.shape == (M_V_W, d) and wr.shape == (R_V_W, d) and M_V_W == R_V_W == d
    assert m % bm == 0 and d % stage_rows == 0
    const = lambda shape: pl.BlockSpec(shape, lambda i: (0, 0))
    in_hbm = pl.BlockSpec(memory_space=pl.ANY)
    return pl.pallas_call(
        _branch_kernel,
        out_shape=jax.ShapeDtypeStruct((m, d), BF16),
        grid=(m // bm,),
        in_specs=[
            pl.BlockSpec((bm, M_V_W), lambda i: (i, 0)),
            pl.BlockSpec((bm, R_V_W), lambda i: (i, 0)),
            pl.BlockSpec((bm, M_V_W), lambda i: (i, o_blk)),
            pl.BlockSpec((bm, R_V_W), lambda i: (i, s_blk)),
            pl.BlockSpec((bm, d), lambda i: (i, g_a)),
            pl.BlockSpec((bm, d), lambda i: (i, g_a + 1)),
            const((1, M_V_W)),
            const((1, R_V_W)),
            in_hbm,
            in_hbm,
        ],
        out_specs=pl.BlockSpec((bm, d), lambda i: (i, 0)),
        scratch_shapes=[
            pltpu.VMEM((bm, M_V_W), BF16),
            pltpu.VMEM((bm, R_V_W), BF16),
            pltpu.VMEM((M_V_W, d), BF16),
            pltpu.VMEM((R_V_W, d), BF16),
            pltpu.VMEM((2, stage_rows, d), F32),
            pltpu.SemaphoreType.DMA((2,)),
        ],
        compiler_params=pltpu.CompilerParams(
            dimension_semantics=("arbitrary",), vmem_limit_bytes=52 * MIB),
        name="branch",
    )(hh, ho, u, u, u, u, m_norm_g, r_norm_g, wm, wr)


def _outproj_kernel(alpha, mg_ref, x_ref, wo_hbm, bo_ref, lg_ref, lb_ref, out_ref, wo_ref, stage_ref, sem_ref):
    @pl.when(pl.program_id(0) == 0)
    def _():
        _load_weight_bf16(wo_hbm, wo_ref, stage_ref, sem_ref)

    y = _dot(mg_ref[...], wo_ref[...]) + bo_ref[...]
    out_ref[...] = _layer_norm(alpha * x_ref[...] + y, lg_ref[...], lb_ref[...])


def _outproj(merged, x2d, wo, bo, ln_g, ln_b, *, alpha, bm=512, stage_rows=256):
    m, d = x2d.shape
    assert wo.shape == (d, d) and m % bm == 0 and d % stage_rows == 0
    const = lambda shape: pl.BlockSpec(shape, lambda i: (0, 0))
    return pl.pallas_call(
        functools.partial(_outproj_kernel, alpha),
        out_shape=jax.ShapeDtypeStruct((m, d), F32),
        grid=(m // bm,),
        in_specs=[
            pl.BlockSpec((bm, d), lambda i: (i, 0)),
            pl.BlockSpec((bm, d), lambda i: (i, 0)),
            pl.BlockSpec(memory_space=pl.ANY),
            const((1, d)),
            const((1, d)),
            const((1, d)),
        ],
        out_specs=pl.BlockSpec((bm, d), lambda i: (i, 0)),
        scratch_shapes=[
            pltpu.VMEM((d, d), BF16),
            pltpu.VMEM((2, stage_rows, d), F32),
            pltpu.SemaphoreType.DMA((2,)),
        ],
        compiler_params=pltpu.CompilerParams(
            dimension_semantics=("arbitrary",), vmem_limit_bytes=48 * MIB),
        name="outproj",
    )(merged, x2d, wo, bo, ln_g, ln_b)


def _mlp_kernel(alpha, x_ref, w1_ref, b1_ref, w2_ref, b2_ref, lg_ref, lb_ref, out_ref, xb_ref):
    j = pl.program_id(1)

    @pl.when(j == 0)
    def _():
        xb_ref[...] = x_ref[...].astype(BF16)
        out_ref[...] = jnp.zeros_like(out_ref)

    hid = jnp.maximum(_dot(xb_ref[...], w1_ref[...]) + b1_ref[...], 0.0)
    out_ref[...] += _dot((hid * hid).astype(BF16), w2_ref[...])

    @pl.when(j == pl.num_programs(1) - 1)
    def _():
        z = alpha * x_ref[...] + (out_ref[...] + b2_ref[...])
        out_ref[...] = _layer_norm(z, lg_ref[...], lb_ref[...])


def _mlp(x2d, w1, b1, w2, b2, ln_g, ln_b, *, alpha, bm=512, bf=1024):
    m, d = x2d.shape
    f = w1.shape[1]
    return pl.pallas_call(
        functools.partial(_mlp_kernel, alpha),
        out_shape=jax.ShapeDtypeStruct((m, d), F32),
        grid=(m // bm, f // bf),
        in_specs=[
            pl.BlockSpec((bm, d), lambda i, j: (i, 0)),
            pl.BlockSpec((d, bf), lambda i, j: (0, j)),
            pl.BlockSpec((1, bf), lambda i, j: (0, j)),
            pl.BlockSpec((bf, d), lambda i, j: (j, 0)),
            pl.BlockSpec((1, d), lambda i, j: (0, 0)),
            pl.BlockSpec((1, d), lambda i, j: (0, 0)),
            pl.BlockSpec((1, d), lambda i, j: (0, 0)),
        ],
        out_specs=pl.BlockSpec((bm, d), lambda i, j: (i, 0)),
        scratch_shapes=[pltpu.VMEM((bm, d), BF16)],
        compiler_params=pltpu.CompilerParams(
            dimension_semantics=("arbitrary", "arbitrary"), vmem_limit_bytes=48 * MIB),
        name="mlp",
    )(x2d, w1, b1, w2, b2, ln_g, ln_b)


def kernel(x, positions, w_in, b_in, m_conv_w, m_conv_b, m_norm_g, r_norm_g, w_branch_m, w_branch_r,
           w_out, b_out, ln1_g, ln1_b, w_ff1, b_ff1, w_ff2, b_ff2, ln2_g, ln2_b):
    batch, seq, d = x.shape
    depth = w_in.shape[0]
    n_chunks = seq // CHUNK
    alpha = (2.0 * depth) ** 0.25
    gate_lo = 2 * M_QK_W + 2 * M_V_W

    half = R_DQK // 2
    inv_freq = ROPE_BASE ** (-jnp.arange(half, dtype=F32) / half)
    inv_freq = jnp.concatenate([inv_freq, inv_freq]).reshape(1, R_DQK)
    sign = jnp.concatenate([-jnp.ones((half,), F32), jnp.ones((half,), F32)]).reshape(1, R_DQK)
    pos_p = positions.astype(F32).reshape(batch * n_chunks, 2, CHUNK // 2).transpose(0, 2, 1)
    pos_p = jnp.repeat(pos_p, half, axis=2).reshape(batch * n_chunks * (CHUNK // 2), R_DQK)

    h = x.reshape(batch * seq, d)
    for l in range(depth):
        u, gates = _proj(h, w_in[l].T, b_in[l].reshape(1, -1),
                         gate_lo=gate_lo, n_gate=2 * M_HEADS)
        hh, ho, w1b, w2b = _rec(u, gates, pos_p, m_conv_w[l], m_conv_b[l].reshape(1, -1), inv_freq, sign,
                                batch=batch, n_chunks=n_chunks, to_bf16=(w_ff1[l], w_ff2[l]))
        merged = _branch(hh, ho, u, m_norm_g[l].reshape(1, -1), r_norm_g[l].reshape(1, -1),
                         w_branch_m[l], w_branch_r[l])
        h1 = _outproj(merged, h, w_out[l], b_out[l].reshape(1, -1), ln1_g[l].reshape(1, -1),
                      ln1_b[l].reshape(1, -1), alpha=alpha)
        h = _mlp(h1, w1b, b_ff1[l].reshape(1, -1), w2b,
                 b_ff2[l].reshape(1, -1), ln2_g[l].reshape(1, -1), ln2_b[l].reshape(1, -1), alpha=alpha)
    return h.reshape(batch, seq, d)
```

```python
import functools
import math

import jax
import jax.numpy as jnp
from jax import lax
from jax.experimental import pallas as pl
from jax.experimental.pallas import tpu as pltpu

M_HEADS, M_DQK, M_DV = 4, 256, 512
R_HEADS, R_DQK, R_DV = 8, 128, 256
CONV_K = 4
CHUNK = 128
ROPE_BASE = 10000.0
LN_EPS = 1e-5
M_QK_W = M_HEADS * M_DQK
M_V_W = M_HEADS * M_DV
R_QK_W = R_HEADS * R_DQK
R_V_W = R_HEADS * R_DV

LANES = 128
SUBLANES = 8
GATE_PAD = LANES
MIB = 1024 * 1024

F32 = jnp.float32
BF16 = jnp.bfloat16

def _sigmoid(x):
    return 1.0 / (1.0 + jnp.exp(-x))


def _log_sigmoid(x):
    return jnp.minimum(x, 0.0) - jnp.log1p(jnp.exp(-jnp.abs(x)))


def _dot(a, b):
    return jnp.dot(a, b, preferred_element_type=F32)


def _dot_nt(a, b):
    return lax.dot_general(a, b, (((1,), (1,)), ((), ())), preferred_element_type=F32)


def _layer_norm(z, g, b):
    mu = jnp.mean(z, axis=-1, keepdims=True)
    d = z - mu
    var = jnp.mean(d * d, axis=-1, keepdims=True)
    return d * lax.rsqrt(var + LN_EPS) * g + b


def _head_norm(h, g):
    mu = jnp.mean(h, axis=-1, keepdims=True)
    d = h - mu
    var = jnp.mean(d * d, axis=-1, keepdims=True)
    return d * lax.rsqrt(var + LN_EPS) * g


def _col_tile(step, gate_blk, n_tiles):
    return lax.rem(step + gate_blk, n_tiles)


def _proj_kernel(gate_blk, n_gate, x_ref, wa_ref, wb_ref, ba_ref, bb_ref, u_ref, g_ref, xb_scr):
    j = pl.program_id(1)
    c = _col_tile(j, gate_blk, pl.num_programs(1))
    bn = wa_ref.shape[0]

    @pl.when(j == 0)
    def _():
        xb_scr[...] = x_ref[...].astype(BF16)

    tail = c >= gate_blk
    w_shift = jnp.concatenate([wa_ref[n_gate:bn, :], wb_ref[...]], axis=0)
    w = jnp.where(tail, w_shift, wa_ref[...]).astype(BF16)
    b_shift = jnp.concatenate([ba_ref[...], bb_ref[...]], axis=1)[:, n_gate:n_gate + bn]
    b = jnp.where(tail, b_shift, ba_ref[...])
    xb = xb_scr[...]
    u_ref[...] = (_dot_nt(xb, w) + b).astype(u_ref.dtype)

    @pl.when(j == 0)
    def _():
        g_ref[...] = _dot_nt(xb, wa_ref[0:GATE_PAD, :].astype(BF16)) + ba_ref[:, 0:GATE_PAD]


def _proj(x2d, w_t, b, *, gate_lo, n_gate, bm=1024, bn=1024):
    m, d = x2d.shape
    n = w_t.shape[0] - n_gate
    assert gate_lo % bn == 0 and n % bn == 0 and m % bm == 0
    assert n_gate % SUBLANES == 0 and n_gate <= LANES
    gate_blk = gate_lo // bn
    n_j = n // bn
    col = lambda j: _col_tile(j, gate_blk, n_j)
    return pl.pallas_call(
        functools.partial(_proj_kernel, gate_blk, n_gate),
        out_shape=(jax.ShapeDtypeStruct((m, n), BF16), jax.ShapeDtypeStruct((m, GATE_PAD), F32)),
        grid=(m // bm, n_j),
        in_specs=[
            pl.BlockSpec((bm, d), lambda i, j: (i, 0)),
            pl.BlockSpec((bn, d), lambda i, j: (col(j), 0)),
            pl.BlockSpec((n_gate, d), lambda i, j: ((col(j) + 1) * (bn // n_gate), 0)),
            pl.BlockSpec((1, bn), lambda i, j: (0, col(j))),
            pl.BlockSpec((1, LANES), lambda i, j: (0, (col(j) + 1) * (bn // LANES))),
        ],
        out_specs=(
            pl.BlockSpec((bm, bn), lambda i, j: (i, col(j))),
            pl.BlockSpec((bm, GATE_PAD), lambda i, j: (i, 0)),
        ),
        scratch_shapes=[pltpu.VMEM((bm, d), BF16)],
        compiler_params=pltpu.CompilerParams(
            dimension_semantics=("arbitrary", "arbitrary"), vmem_limit_bytes=56 * MIB),
        name="proj",
    )(x2d, w_t, w_t, b, b)


def _conv_shift_matrix():
    L = CHUNK
    row = lax.broadcasted_iota(jnp.int32, (L, 2 * L), 0)
    col = lax.broadcasted_iota(jnp.int32, (L, 2 * L), 1)
    blocks = [jnp.where(col == row + (L - (CONV_K - 1) + j), 1.0, 0.0) for j in range(CONV_K)]
    return jnp.concatenate(blocks, axis=0).astype(BF16)


def _mlstm_chunk(q_ref, k_ref, v_ref, g_ref, cw_ref, cb_ref, out_ref, c_ref, n_ref, m_ref,
                 x2_ref, shift_ref):
    L = CHUNK

    x2_ref[L:2 * L, 0:M_QK_W] = q_ref[...]
    x2_ref[L:2 * L, M_QK_W:2 * M_QK_W] = k_ref[...]
    taps = _dot(shift_ref[...], x2_ref[...])
    conv = cb_ref[...]
    for j in range(CONV_K):
        conv = conv + cw_ref[j:j + 1, :] * taps[j * L:(j + 1) * L, :]
    x2_ref[0:L, :] = x2_ref[L:2 * L, :]
    qk = conv * _sigmoid(conv)

    g = g_ref[...]
    lfg = _log_sigmoid(g)
    g_rows = g.T[0:2 * M_HEADS, :]
    lf_rows = _log_sigmoid(g_rows)

    row_t = lax.broadcasted_iota(jnp.int32, (L, L), 0)
    col_s = lax.broadcasted_iota(jnp.int32, (L, L), 1)
    causal = col_s <= row_t
    lane = lax.broadcasted_iota(jnp.int32, (L, GATE_PAD), 1)

    heads = range(M_HEADS)
    qh = [qk[:, h * M_DQK:(h + 1) * M_DQK] * (M_DQK ** -0.5) for h in heads]
    kh = [qk[:, M_QK_W + h * M_DQK:M_QK_W + (h + 1) * M_DQK] for h in heads]
    qb = [q.astype(BF16) for q in qh]
    vb = [v_ref[:, h * M_DV:(h + 1) * M_DV] for h in heads]
    scores = [_dot_nt(qb[h], kh[h].astype(BF16)) for h in heads]
    inter = [_dot(qb[h], c_ref[h].astype(BF16)) for h in heads]

    gate = []
    for h in heads:
        li_row = g_rows[h:h + 1, :]
        lf_row = lf_rows[M_HEADS + h:M_HEADS + h + 1, :]
        li_col = jnp.sum(jnp.where(lane == h, g, 0.0), axis=1, keepdims=True)
        lf_col = jnp.sum(jnp.where(lane == M_HEADS + h, lfg, 0.0), axis=1, keepdims=True)
        b_col = jnp.sum(jnp.where(causal, lf_row, 0.0), axis=1, keepdims=True)
        b_row = jnp.sum(jnp.where(row_t <= col_s, lf_col, 0.0), axis=0, keepdims=True)
        g_tot = jnp.sum(lf_row, axis=1, keepdims=True)
        m_prev = m_ref[h:h + 1, 0:1]

        d_log = jnp.where(causal, b_col - b_row + li_row, -jnp.inf)
        m_inter = b_col + m_prev
        m_t = jnp.maximum(m_inter, jnp.max(d_log, axis=1, keepdims=True))
        w_intra = jnp.exp(d_log - m_t)
        w_inter = jnp.exp(m_inter - m_t)

        w_log_row = g_tot - b_row + li_row
        w_log_col = g_tot - b_col + li_col
        m_new = jnp.maximum(g_tot + m_prev, jnp.max(w_log_row, axis=1, keepdims=True))
        w_s = jnp.exp(w_log_col - m_new)
        decay = jnp.exp(g_tot + m_prev - m_new)
        gate.append((m_t, w_intra, w_inter, m_new, w_s, decay))

    for h in heads:
        m_t, w_intra, w_inter, m_new, w_s, decay = gate[h]
        n_old = n_ref[h:h + 1, :]
        s = scores[h] * w_intra
        num = _dot(s.astype(BF16), vb[h]) + w_inter * inter[h]
        den = (jnp.sum(s, axis=1, keepdims=True)
               + w_inter * jnp.sum(qh[h] * n_old, axis=1, keepdims=True))
        hh = num * (1.0 / jnp.maximum(jnp.abs(den), jnp.exp(-m_t)))

        kw = kh[h] * w_s
        c_ref[h] = decay * c_ref[h] + _dot(kw.T.astype(BF16), vb[h])
        n_ref[h:h + 1, :] = decay * n_old + jnp.sum(kw, axis=0, keepdims=True)
        m_ref[h:h + 1, :] = jnp.broadcast_to(m_new, (1, LANES))

        out_ref[:, h * M_DV:(h + 1) * M_DV] = hh.astype(out_ref.dtype)


def _retention_constants(decay_ref, xi_ref, zeta_ref):
    L = CHUNK
    t_col = lax.broadcasted_iota(jnp.int32, (L, 1), 0).astype(F32)
    rel = (lax.broadcasted_iota(jnp.int32, (L, L), 0)
           - lax.broadcasted_iota(jnp.int32, (L, L), 1)).astype(F32)
    scale = R_DQK ** -0.5
    for h in range(R_HEADS):
        log_gamma = math.log1p(-(2.0 ** (-5.0 - h)))
        decay_ref[h] = jnp.where(rel >= 0.0, jnp.exp(jnp.maximum(rel, 0.0) * log_gamma), 0.0) * scale
        xi_ref[h] = jnp.exp((t_col + 1.0) * log_gamma) * scale
        zeta_ref[h] = jnp.exp((L - 1.0 - t_col) * log_gamma)


def _retention_chunk(q_ref, k_ref, v_ref, pos_ref, invf_ref, sign_ref, out_ref, r_ref,
                     decay_ref, xi_ref, zeta_ref):
    L = CHUNK
    half = R_DQK // 2
    assert L == R_DQK
    lane = lax.broadcasted_iota(jnp.int32, (L // 2, R_DQK), 1)
    pos_rows = jnp.broadcast_to(pos_ref[0].astype(F32), (L, L)).T
    ang = jnp.where(lane < half, pos_rows[0:L // 2, :], pos_rows[L // 2:L, :]) * invf_ref[...]

    def unpack(packed):
        swapped = pltpu.roll(packed, half, 1)
        top = jnp.where(lane < half, packed, swapped)
        bottom = jnp.where(lane < half, swapped, packed)
        return jnp.concatenate([top, bottom], axis=0)

    cos = unpack(jnp.cos(ang))
    sin = unpack(jnp.sin(ang)) * sign_ref[...]

    heads = range(R_HEADS)
    qr, kr = [], []
    for h in heads:
        qs = slice(h * R_DQK, (h + 1) * R_DQK)
        xq = q_ref[:, qs].astype(F32)
        xk = k_ref[:, qs].astype(F32)
        qr.append((xq * cos + pltpu.roll(xq, half, 1) * sin).astype(BF16))
        kr.append(xk * cos + pltpu.roll(xk, half, 1) * sin)
    vb = [v_ref[:, h * R_DV:(h + 1) * R_DV] for h in heads]
    scores = [_dot_nt(qr[h], kr[h].astype(BF16)) for h in heads]
    cross = [_dot(qr[h], r_ref[h].astype(BF16)) for h in heads]

    for h in heads:
        g_chunk = math.exp(L * math.log1p(-(2.0 ** (-5.0 - h))))
        p = scores[h] * decay_ref[h]
        o = _dot(p.astype(BF16), vb[h]) + cross[h] * xi_ref[h]
        r_ref[h] = g_chunk * r_ref[h] + _dot((kr[h] * zeta_ref[h]).T.astype(BF16), vb[h])
        out_ref[:, h * R_DV:(h + 1) * R_DV] = o.astype(out_ref.dtype)


def _rec_kernel(n_chunks, n_cast, mq_ref, mk_ref, mv_ref, g_ref, cw_ref, cb_ref,
                rq_ref, rk_ref, rv_ref, pos_ref, invf_ref, sign_ref, *refs):
    cast_in, refs = refs[:n_cast], refs[n_cast:]
    hm_ref, hr_ref = refs[:2]
    cast_out, refs = refs[2:2 + n_cast], refs[2 + n_cast:]
    c_ref, n_ref, m_ref, x2_ref, shift_ref, r_ref, decay_ref, xi_ref, zeta_ref = refs

    for src, dst in zip(cast_in, cast_out):
        dst[...] = src[...].astype(BF16)

    @pl.when(pl.program_id(0) == 0)
    def _():
        shift_ref[...] = _conv_shift_matrix()
        _retention_constants(decay_ref, xi_ref, zeta_ref)

    @pl.when(lax.rem(pl.program_id(0), n_chunks) == 0)
    def _():
        c_ref[...] = jnp.zeros_like(c_ref)
        n_ref[...] = jnp.zeros_like(n_ref)
        m_ref[...] = jnp.zeros_like(m_ref)
        r_ref[...] = jnp.zeros_like(r_ref)
        x2_ref[0:CHUNK, :] = jnp.zeros((CHUNK, 2 * M_QK_W), BF16)

    _mlstm_chunk(mq_ref, mk_ref, mv_ref, g_ref, cw_ref, cb_ref, hm_ref, c_ref, n_ref, m_ref,
                 x2_ref, shift_ref)
    _retention_chunk(rq_ref, rk_ref, rv_ref, pos_ref, invf_ref, sign_ref, hr_ref, r_ref,
                     decay_ref, xi_ref, zeta_ref)


def _rec(u, gates, pos_p, conv_w, conv_b, inv_freq, sign, *, batch, n_chunks, to_bf16=()):
    L = CHUNK
    total = batch * n_chunks
    assert all(w.ndim == 2 and w.shape[0] % (total * 2 * SUBLANES) == 0 for w in to_bf16)
    slab = lambda w: pl.BlockSpec((w.shape[0] // total, w.shape[1]), lambda t: (t, 0))
    r_q = (2 * M_QK_W + 2 * M_V_W) // R_QK_W
    r_v = (2 * M_QK_W + 2 * M_V_W + 2 * R_QK_W) // R_V_W
    const = lambda shape: pl.BlockSpec(shape, lambda t: (0, 0))
    return pl.pallas_call(
        functools.partial(_rec_kernel, n_chunks, len(to_bf16)),
        out_shape=(jax.ShapeDtypeStruct((total * L, M_V_W), BF16),
                   jax.ShapeDtypeStruct((total * L, R_V_W), BF16),
                   *[jax.ShapeDtypeStruct(w.shape, BF16) for w in to_bf16]),
        grid=(total,),
        in_specs=[
            pl.BlockSpec((L, M_QK_W), lambda t: (t, 0)),
            pl.BlockSpec((L, M_QK_W), lambda t: (t, 1)),
            pl.BlockSpec((L, M_V_W), lambda t: (t, 1)),
            pl.BlockSpec((L, GATE_PAD), lambda t: (t, 0)),
            const((CONV_K, 2 * M_QK_W)),
            const((1, 2 * M_QK_W)),
            pl.BlockSpec((L, R_QK_W), lambda t: (t, r_q)),
            pl.BlockSpec((L, R_QK_W), lambda t: (t, r_q + 1)),
            pl.BlockSpec((L, R_V_W), lambda t: (t, r_v)),
            pl.BlockSpec((1, 1, L), lambda t: (t, 0, 0)),
            const((1, R_DQK)),
            const((1, R_DQK)),
            *[slab(w) for w in to_bf16],
        ],
        out_specs=(pl.BlockSpec((L, M_V_W), lambda t: (t, 0)),
                   pl.BlockSpec((L, R_V_W), lambda t: (t, 0)),
                   *[slab(w) for w in to_bf16]),
        scratch_shapes=[
            pltpu.VMEM((M_HEADS, M_DQK, M_DV), F32),
            pltpu.VMEM((SUBLANES, M_DQK), F32),
            pltpu.VMEM((SUBLANES, LANES), F32),
            pltpu.VMEM((2 * L, 2 * M_QK_W), BF16),
            pltpu.VMEM((CONV_K * L, 2 * L), BF16),
            pltpu.VMEM((R_HEADS, R_DQK, R_DV), F32),
            pltpu.VMEM((R_HEADS, L, L), F32),
            pltpu.VMEM((R_HEADS, L, 1), F32),
            pltpu.VMEM((R_HEADS, L, 1), F32),
        ],
        compiler_params=pltpu.CompilerParams(
            dimension_semantics=("arbitrary",), vmem_limit_bytes=40 * MIB),
        name="rec",
    )(u, u, u, gates, conv_w, conv_b, u, u, u, pos_p, inv_freq, sign, *to_bf16)


def _load_weight_bf16(w_hbm, w_scr, stage, sem):
    n_slots, rows, _ = stage.shape
    n = w_hbm.shape[0] // rows

    def copy(c, slot):
        src = w_hbm.at[pl.ds(pl.multiple_of(c * rows, rows), rows), :]
        return pltpu.make_async_copy(src, stage.at[slot], sem.at[slot])

    copy(0, 0).start()

    def body(c, carry):
        slot = lax.rem(c, n_slots)

        @pl.when(c + 1 < n)
        def _():
            copy(c + 1, lax.rem(c + 1, n_slots)).start()

        copy(c, slot).wait()
        w_scr[pl.ds(pl.multiple_of(c * rows, rows), rows), :] = stage[slot].astype(BF16)
        return carry

    lax.fori_loop(0, n, body, 0)


def _branch_kernel(hh_ref, ho_ref, mo_ref, rg_ref, ga_ref, gb_ref, mng_ref, rng_ref, wm_hbm, wr_hbm,
                   out_ref, hm_scr, hr_scr, wm_ref, wr_ref, stage_ref, sem_ref):
    @pl.when(pl.program_id(0) == 0)
    def _():
        _load_weight_bf16(wm_hbm, wm_ref, stage_ref, sem_ref)
        _load_weight_bf16(wr_hbm, wr_ref, stage_ref, sem_ref)

    for h in range(M_HEADS):
        sl = slice(h * M_DV, (h + 1) * M_DV)
        y = _head_norm(hh_ref[:, sl].astype(F32), mng_ref[:, sl]) * _sigmoid(mo_ref[:, sl].astype(F32))
        hm_scr[:, sl] = y.astype(BF16)
    for h in range(R_HEADS):
        sl = slice(h * R_DV, (h + 1) * R_DV)
        sg = rg_ref[:, sl].astype(F32)
        y = _head_norm(ho_ref[:, sl].astype(F32), rng_ref[:, sl]) * (sg * _sigmoid(sg))
        hr_scr[:, sl] = y.astype(BF16)

    ya = _dot(hm_scr[...], wm_ref[...])
    yb = _dot(hr_scr[...], wr_ref[...])
    merged = _sigmoid(ga_ref[...].astype(F32)) * ya + _sigmoid(gb_ref[...].astype(F32)) * yb
    out_ref[...] = merged.astype(out_ref.dtype)


def _branch(hh, ho, u, m_norm_g, r_norm_g, wm, wr, *, bm=256, stage_rows=256):
    m, d = hh.shape[0], wm.shape[1]
    o_blk = (2 * M_QK_W + M_V_W) // M_V_W
    s_blk = (2 * M_QK_W + 2 * M_V_W + 2 * R_QK_W + R_V_W) // R_V_W
    g_a = (2 * M_QK_W + 2 * M_V_W + 2 * R_QK_W + 2 * R_V_W) // d
    assert wm.shape == (M_V_W, d) and wr.shape == (R_V_W, d) and M_V_W == R_V_W == d
    assert m % bm == 0 and d % stage_rows == 0
    const = lambda shape: pl.BlockSpec(shape, lambda i: (0, 0))
    in_hbm = pl.BlockSpec(memory_space=pl.ANY)
    return pl.pallas_call(
        _branch_kernel,
        out_shape=jax.ShapeDtypeStruct((m, d), BF16),
        grid=(m // bm,),
        in_specs=[
            pl.BlockSpec((bm, M_V_W), lambda i: (i, 0)),
            pl.BlockSpec((bm, R_V_W), lambda i: (i, 0)),
            pl.BlockSpec((bm, M_V_W), lambda i: (i, o_blk)),
            pl.BlockSpec((bm, R_V_W), lambda i: (i, s_blk)),
            pl.BlockSpec((bm, d), lambda i: (i, g_a)),
            pl.BlockSpec((bm, d), lambda i: (i, g_a + 1)),
            const((1, M_V_W)),
            const((1, R_V_W)),
            in_hbm,
            in_hbm,
        ],
        out_specs=pl.BlockSpec((bm, d), lambda i: (i, 0)),
        scratch_shapes=[
            pltpu.VMEM((bm, M_V_W), BF16),
            pltpu.VMEM((bm, R_V_W), BF16),
            pltpu.VMEM((M_V_W, d), BF16),
            pltpu.VMEM((R_V_W, d), BF16),
            pltpu.VMEM((2, stage_rows, d), F32),
            pltpu.SemaphoreType.DMA((2,)),
        ],
        compiler_params=pltpu.CompilerParams(
            dimension_semantics=("arbitrary",), vmem_limit_bytes=52 * MIB),
        name="branch",
    )(hh, ho, u, u, u, u, m_norm_g, r_norm_g, wm, wr)


def _outproj_kernel(alpha, mg_ref, x_ref, wo_hbm, bo_ref, lg_ref, lb_ref, out_ref, wo_ref, stage_ref, sem_ref):
    @pl.when(pl.program_id(0) == 0)
    def _():
        _load_weight_bf16(wo_hbm, wo_ref, stage_ref, sem_ref)

    y = _dot(mg_ref[...], wo_ref[...]) + bo_ref[...]
    out_ref[...] = _layer_norm(alpha * x_ref[...] + y, lg_ref[...], lb_ref[...])


def _outproj(merged, x2d, wo, bo, ln_g, ln_b, *, alpha, bm=512, stage_rows=256):
    m, d = x2d.shape
    assert wo.shape == (d, d) and m % bm == 0 and d % stage_rows == 0
    const = lambda shape: pl.BlockSpec(shape, lambda i: (0, 0))
    return pl.pallas_call(
        functools.partial(_outproj_kernel, alpha),
        out_shape=jax.ShapeDtypeStruct((m, d), F32),
        grid=(m // bm,),
        in_specs=[
            pl.BlockSpec((bm, d), lambda i: (i, 0)),
            pl.BlockSpec((bm, d), lambda i: (i, 0)),
            pl.BlockSpec(memory_space=pl.ANY),
            const((1, d)),
            const((1, d)),
            const((1, d)),
        ],
        out_specs=pl.BlockSpec((bm, d), lambda i: (i, 0)),
        scratch_shapes=[
            pltpu.VMEM((d, d), BF16),
            pltpu.VMEM((2, stage_rows, d), F32),
            pltpu.SemaphoreType.DMA((2,)),
        ],
        compiler_params=pltpu.CompilerParams(
            dimension_semantics=("arbitrary",), vmem_limit_bytes=48 * MIB),
        name="outproj",
    )(merged, x2d, wo, bo, ln_g, ln_b)


def _mlp_kernel(alpha, x_ref, w1_ref, b1_ref, w2_ref, b2_ref, lg_ref, lb_ref, out_ref, xb_ref):
    j = pl.program_id(1)

    @pl.when(j == 0)
    def _():
        xb_ref[...] = x_ref[...].astype(BF16)
        out_ref[...] = jnp.zeros_like(out_ref)

    hid = jnp.maximum(_dot(xb_ref[...], w1_ref[...]) + b1_ref[...], 0.0)
    out_ref[...] += _dot((hid * hid).astype(BF16), w2_ref[...])

    @pl.when(j == pl.num_programs(1) - 1)
    def _():
        z = alpha * x_ref[...] + (out_ref[...] + b2_ref[...])
        out_ref[...] = _layer_norm(z, lg_ref[...], lb_ref[...])


def _mlp(x2d, w1, b1, w2, b2, ln_g, ln_b, *, alpha, bm=512, bf=1024):
    m, d = x2d.shape
    f = w1.shape[1]
    return pl.pallas_call(
        functools.partial(_mlp_kernel, alpha),
        out_shape=jax.ShapeDtypeStruct((m, d), F32),
        grid=(m // bm, f // bf),
        in_specs=[
            pl.BlockSpec((bm, d), lambda i, j: (i, 0)),
            pl.BlockSpec((d, bf), lambda i, j: (0, j)),
            pl.BlockSpec((1, bf), lambda i, j: (0, j)),
            pl.BlockSpec((bf, d), lambda i, j: (j, 0)),
            pl.BlockSpec((1, d), lambda i, j: (0, 0)),
            pl.BlockSpec((1, d), lambda i, j: (0, 0)),
            pl.BlockSpec((1, d), lambda i, j: (0, 0)),
        ],
        out_specs=pl.BlockSpec((bm, d), lambda i, j: (i, 0)),
        scratch_shapes=[pltpu.VMEM((bm, d), BF16)],
        compiler_params=pltpu.CompilerParams(
            dimension_semantics=("arbitrary", "arbitrary"), vmem_limit_bytes=48 * MIB),
        name="mlp",
    )(x2d, w1, b1, w2, b2, ln_g, ln_b)


def kernel(x, positions, w_in, b_in, m_conv_w, m_conv_b, m_norm_g, r_norm_g, w_branch_m, w_branch_r,
           w_out, b_out, ln1_g, ln1_b, w_ff1, b_ff1, w_ff2, b_ff2, ln2_g, ln2_b):
    batch, seq, d = x.shape
    depth = w_in.shape[0]
    n_chunks = seq // CHUNK
    alpha = (2.0 * depth) ** 0.25
    gate_lo = 2 * M_QK_W + 2 * M_V_W

    half = R_DQK // 2
    inv_freq = ROPE_BASE ** (-jnp.arange(half, dtype=F32) / half)
    inv_freq = jnp.concatenate([inv_freq, inv_freq]).reshape(1, R_DQK)
    sign = jnp.concatenate([-jnp.ones((half,), F32), jnp.ones((half,), F32)]).reshape(1, R_DQK)
    pos_p = positions.reshape(batch * n_chunks, 1, CHUNK)

    h = x.reshape(batch * seq, d)
    for l in range(depth):
        u, gates = _proj(h, w_in[l].T, b_in[l].reshape(1, -1),
                         gate_lo=gate_lo, n_gate=2 * M_HEADS)
        hh, ho, w1b, w2b = _rec(u, gates, pos_p, m_conv_w[l], m_conv_b[l].reshape(1, -1), inv_freq, sign,
                                batch=batch, n_chunks=n_chunks, to_bf16=(w_ff1[l], w_ff2[l]))
        merged = _branch(hh, ho, u, m_norm_g[l].reshape(1, -1), r_norm_g[l].reshape(1, -1),
                         w_branch_m[l], w_branch_r[l])
        h1 = _outproj(merged, h, w_out[l], b_out[l].reshape(1, -1), ln1_g[l].reshape(1, -1),
                      ln1_b[l].reshape(1, -1), alpha=alpha)
        h = _mlp(h1, w1b, b_ff1[l].reshape(1, -1), w2b,
                 b_ff2[l].reshape(1, -1), ln2_g[l].reshape(1, -1), ln2_b[l].reshape(1, -1), alpha=alpha)
    return h.reshape(batch, seq, d)
```

```python
import functools
import math

import jax
import jax.numpy as jnp
from jax import lax
from jax.experimental import pallas as pl
from jax.experimental.pallas import tpu as pltpu

M_HEADS, M_DQK, M_DV = 4, 256, 512
R_HEADS, R_DQK, R_DV = 8, 128, 256
CONV_K = 4
CHUNK = 128
ROPE_BASE = 10000.0
LN_EPS = 1e-5
M_QK_W = M_HEADS * M_DQK
M_V_W = M_HEADS * M_DV
R_QK_W = R_HEADS * R_DQK
R_V_W = R_HEADS * R_DV

LANES = 128
SUBLANES = 8
GATE_PAD = LANES
MIB = 1024 * 1024

F32 = jnp.float32
BF16 = jnp.bfloat16

def _sigmoid(x):
    return 1.0 / (1.0 + jnp.exp(-x))


def _log_sigmoid(x):
    return jnp.minimum(x, 0.0) - jnp.log1p(jnp.exp(-jnp.abs(x)))


def _dot(a, b):
    return jnp.dot(a, b, preferred_element_type=F32)


def _dot_nt(a, b):
    return lax.dot_general(a, b, (((1,), (1,)), ((), ())), preferred_element_type=F32)


def _layer_norm(z, g, b):
    mu = jnp.mean(z, axis=-1, keepdims=True)
    d = z - mu
    var = jnp.mean(d * d, axis=-1, keepdims=True)
    return d * lax.rsqrt(var + LN_EPS) * g + b


def _head_norm(h, g):
    mu = jnp.mean(h, axis=-1, keepdims=True)
    d = h - mu
    var = jnp.mean(d * d, axis=-1, keepdims=True)
    return d * lax.rsqrt(var + LN_EPS) * g


def _col_tile(step, gate_blk, n_tiles):
    return lax.rem(step + gate_blk, n_tiles)


def _proj_kernel(gate_blk, n_gate, x_ref, wa_ref, wb_ref, ba_ref, bb_ref, u_ref, g_ref, xb_scr):
    j = pl.program_id(1)
    c = _col_tile(j, gate_blk, pl.num_programs(1))
    bn = wa_ref.shape[0]

    @pl.when(j == 0)
    def _():
        xb_scr[...] = x_ref[...].astype(BF16)

    tail = c >= gate_blk
    w_shift = jnp.concatenate([wa_ref[n_gate:bn, :], wb_ref[...]], axis=0)
    w = jnp.where(tail, w_shift, wa_ref[...]).astype(BF16)
    b_shift = jnp.concatenate([ba_ref[...], bb_ref[...]], axis=1)[:, n_gate:n_gate + bn]
    b = jnp.where(tail, b_shift, ba_ref[...])
    xb = xb_scr[...]
    u_ref[...] = (_dot_nt(xb, w) + b).astype(u_ref.dtype)

    @pl.when(j == 0)
    def _():
        g_ref[...] = _dot_nt(xb, wa_ref[0:GATE_PAD, :].astype(BF16)) + ba_ref[:, 0:GATE_PAD]


def _proj(x2d, w_t, b, *, gate_lo, n_gate, bm=1024, bn=1024):
    m, d = x2d.shape
    n = w_t.shape[0] - n_gate
    assert gate_lo % bn == 0 and n % bn == 0 and m % bm == 0
    assert n_gate % SUBLANES == 0 and n_gate <= LANES
    gate_blk = gate_lo // bn
    n_j = n // bn
    col = lambda j: _col_tile(j, gate_blk, n_j)
    return pl.pallas_call(
        functools.partial(_proj_kernel, gate_blk, n_gate),
        out_shape=(jax.ShapeDtypeStruct((m, n), BF16), jax.ShapeDtypeStruct((m, GATE_PAD), F32)),
        grid=(m // bm, n_j),
        in_specs=[
            pl.BlockSpec((bm, d), lambda i, j: (i, 0)),
            pl.BlockSpec((bn, d), lambda i, j: (col(j), 0)),
            pl.BlockSpec((n_gate, d), lambda i, j: ((col(j) + 1) * (bn // n_gate), 0)),
            pl.BlockSpec((1, bn), lambda i, j: (0, col(j))),
            pl.BlockSpec((1, LANES), lambda i, j: (0, (col(j) + 1) * (bn // LANES))),
        ],
        out_specs=(
            pl.BlockSpec((bm, bn), lambda i, j: (i, col(j))),
            pl.BlockSpec((bm, GATE_PAD), lambda i, j: (i, 0)),
        ),
        scratch_shapes=[pltpu.VMEM((bm, d), BF16)],
        compiler_params=pltpu.CompilerParams(
            dimension_semantics=("arbitrary", "arbitrary"), vmem_limit_bytes=56 * MIB),
        name="proj",
    )(x2d, w_t, w_t, b, b)


def _conv_shift_matrix():
    L = CHUNK
    row = lax.broadcasted_iota(jnp.int32, (L, 2 * L), 0)
    col = lax.broadcasted_iota(jnp.int32, (L, 2 * L), 1)
    blocks = [jnp.where(col == row + (L - (CONV_K - 1) + j), 1.0, 0.0) for j in range(CONV_K)]
    return jnp.concatenate(blocks, axis=0).astype(BF16)


def _mlstm_chunk(q_ref, k_ref, v_ref, g_ref, cw_ref, cb_ref, out_ref, c_ref, n_ref, m_ref,
                 x2_ref, shift_ref):
    L = CHUNK

    x2_ref[L:2 * L, 0:M_QK_W] = q_ref[...]
    x2_ref[L:2 * L, M_QK_W:2 * M_QK_W] = k_ref[...]
    taps = _dot(shift_ref[...], x2_ref[...])
    conv = cb_ref[...]
    for j in range(CONV_K):
        conv = conv + cw_ref[j:j + 1, :] * taps[j * L:(j + 1) * L, :]
    x2_ref[0:L, :] = x2_ref[L:2 * L, :]
    qk = conv * _sigmoid(conv)

    g = g_ref[...]
    lfg = _log_sigmoid(g)
    g_rows = g.T[0:2 * M_HEADS, :]
    lf_rows = _log_sigmoid(g_rows)

    row_t = lax.broadcasted_iota(jnp.int32, (L, L), 0)
    col_s = lax.broadcasted_iota(jnp.int32, (L, L), 1)
    causal = col_s <= row_t
    lane = lax.broadcasted_iota(jnp.int32, (L, GATE_PAD), 1)

    heads = range(M_HEADS)
    qh = [qk[:, h * M_DQK:(h + 1) * M_DQK] * (M_DQK ** -0.5) for h in heads]
    kh = [qk[:, M_QK_W + h * M_DQK:M_QK_W + (h + 1) * M_DQK] for h in heads]
    qb = [q.astype(BF16) for q in qh]
    vb = [v_ref[:, h * M_DV:(h + 1) * M_DV] for h in heads]
    scores = [_dot_nt(qb[h], kh[h].astype(BF16)) for h in heads]
    inter = [_dot(qb[h], c_ref[h].astype(BF16)) for h in heads]
    yield

    gate = []
    for h in heads:
        li_row = g_rows[h:h + 1, :]
        lf_row = lf_rows[M_HEADS + h:M_HEADS + h + 1, :]
        li_col = jnp.sum(jnp.where(lane == h, g, 0.0), axis=1, keepdims=True)
        lf_col = jnp.sum(jnp.where(lane == M_HEADS + h, lfg, 0.0), axis=1, keepdims=True)
        b_col = jnp.sum(jnp.where(causal, lf_row, 0.0), axis=1, keepdims=True)
        b_row = jnp.sum(jnp.where(row_t <= col_s, lf_col, 0.0), axis=0, keepdims=True)
        g_tot = jnp.sum(lf_row, axis=1, keepdims=True)
        m_prev = m_ref[h:h + 1, 0:1]

        d_log = jnp.where(causal, b_col - b_row + li_row, -jnp.inf)
        m_inter = b_col + m_prev
        m_t = jnp.maximum(m_inter, jnp.max(d_log, axis=1, keepdims=True))
        w_intra = jnp.exp(d_log - m_t)
        w_inter = jnp.exp(m_inter - m_t)

        w_log_row = g_tot - b_row + li_row
        w_log_col = g_tot - b_col + li_col
        m_new = jnp.maximum(g_tot + m_prev, jnp.max(w_log_row, axis=1, keepdims=True))
        w_s = jnp.exp(w_log_col - m_new)
        decay = jnp.exp(g_tot + m_prev - m_new)
        gate.append((m_t, w_intra, w_inter, m_new, w_s, decay))
    yield

    for h in heads:
        m_t, w_intra, w_inter, m_new, w_s, decay = gate[h]
        n_old = n_ref[h:h + 1, :]
        s = scores[h] * w_intra
        num = _dot(s.astype(BF16), vb[h]) + w_inter * inter[h]
        den = (jnp.sum(s, axis=1, keepdims=True)
               + w_inter * jnp.sum(qh[h] * n_old, axis=1, keepdims=True))
        hh = num * (1.0 / jnp.maximum(jnp.abs(den), jnp.exp(-m_t)))

        kw = kh[h] * w_s
        c_ref[h] = decay * c_ref[h] + _dot(kw.T.astype(BF16), vb[h])
        n_ref[h:h + 1, :] = decay * n_old + jnp.sum(kw, axis=0, keepdims=True)
        m_ref[h:h + 1, :] = jnp.broadcast_to(m_new, (1, LANES))

        out_ref[:, h * M_DV:(h + 1) * M_DV] = hh.astype(out_ref.dtype)


def _retention_constants(decay_ref, xi_ref, zeta_ref):
    L = CHUNK
    t_col = lax.broadcasted_iota(jnp.int32, (L, 1), 0).astype(F32)
    rel = (lax.broadcasted_iota(jnp.int32, (L, L), 0)
           - lax.broadcasted_iota(jnp.int32, (L, L), 1)).astype(F32)
    scale = R_DQK ** -0.5
    for h in range(R_HEADS):
        log_gamma = math.log1p(-(2.0 ** (-5.0 - h)))
        decay_ref[h] = jnp.where(rel >= 0.0, jnp.exp(jnp.maximum(rel, 0.0) * log_gamma), 0.0) * scale
        xi_ref[h] = jnp.exp((t_col + 1.0) * log_gamma) * scale
        zeta_ref[h] = jnp.exp((L - 1.0 - t_col) * log_gamma)


def _retention_chunk(q_ref, k_ref, v_ref, pos_ref, invf_ref, sign_ref, out_ref, r_ref,
                     decay_ref, xi_ref, zeta_ref):
    L = CHUNK
    half = R_DQK // 2
    assert L == R_DQK
    lane = lax.broadcasted_iota(jnp.int32, (L // 2, R_DQK), 1)
    pos_rows = jnp.broadcast_to(pos_ref[0].astype(F32), (L, L)).T
    ang = jnp.where(lane < half, pos_rows[0:L // 2, :], pos_rows[L // 2:L, :]) * invf_ref[...]

    def unpack(packed):
        swapped = pltpu.roll(packed, half, 1)
        top = jnp.where(lane < half, packed, swapped)
        bottom = jnp.where(lane < half, swapped, packed)
        return jnp.concatenate([top, bottom], axis=0)

    cos = unpack(jnp.cos(ang))
    sin = unpack(jnp.sin(ang)) * sign_ref[...]

    heads = range(R_HEADS)
    qr, kr = [], []
    for h in heads:
        qs = slice(h * R_DQK, (h + 1) * R_DQK)
        xq = q_ref[:, qs].astype(F32)
        xk = k_ref[:, qs].astype(F32)
        qr.append((xq * cos + pltpu.roll(xq, half, 1) * sin).astype(BF16))
        kr.append(xk * cos + pltpu.roll(xk, half, 1) * sin)
    vb = [v_ref[:, h * R_DV:(h + 1) * R_DV] for h in heads]
    scores = [_dot_nt(qr[h], kr[h].astype(BF16)) for h in heads]
    cross = [_dot(qr[h], r_ref[h].astype(BF16)) for h in heads]
    yield

    for h in heads:
        g_chunk = math.exp(L * math.log1p(-(2.0 ** (-5.0 - h))))
        p = scores[h] * decay_ref[h]
        o = _dot(p.astype(BF16), vb[h]) + cross[h] * xi_ref[h]
        r_ref[h] = g_chunk * r_ref[h] + _dot((kr[h] * zeta_ref[h]).T.astype(BF16), vb[h])
        out_ref[:, h * R_DV:(h + 1) * R_DV] = o.astype(out_ref.dtype)


def _rec_kernel(batch, n_cast, mq_ref, mk_ref, mv_ref, g_ref, cw_ref, cb_ref,
                rq_ref, rk_ref, rv_ref, pos_ref, invf_ref, sign_ref, *refs):
    cast_in, refs = refs[:n_cast], refs[n_cast:]
    hm_ref, hr_ref = refs[:2]
    cast_out, refs = refs[2:2 + n_cast], refs[2 + n_cast:]
    c_ref, n_ref, m_ref, x2_ref, shift_ref, r_ref, decay_ref, xi_ref, zeta_ref = refs

    @pl.when(pl.program_id(0) == 0)
    def _():
        shift_ref[...] = _conv_shift_matrix()
        _retention_constants(decay_ref, xi_ref, zeta_ref)
        c_ref[...] = jnp.zeros_like(c_ref)
        n_ref[...] = jnp.zeros_like(n_ref)
        m_ref[...] = jnp.zeros_like(m_ref)
        r_ref[...] = jnp.zeros_like(r_ref)
        x2_ref[:, 0:CHUNK, :] = jnp.zeros((batch, CHUNK, 2 * M_QK_W), BF16)

    mlstm = [_mlstm_chunk(mq_ref.at[b], mk_ref.at[b], mv_ref.at[b], g_ref.at[b], cw_ref, cb_ref,
                          hm_ref.at[b], c_ref.at[b], n_ref.at[b], m_ref.at[b], x2_ref.at[b], shift_ref)
             for b in range(batch)]
    retention = [_retention_chunk(rq_ref.at[b], rk_ref.at[b], rv_ref.at[b], pos_ref.at[b], invf_ref,
                                  sign_ref, hr_ref.at[b], r_ref.at[b], decay_ref, xi_ref, zeta_ref)
                 for b in range(batch)]
    for gens in (mlstm, retention):
        live = list(gens)
        while live:
            live = [g for g in live if next(g, StopIteration) is not StopIteration]

    for src, dst in zip(cast_in, cast_out):
        dst[...] = src[...].astype(BF16)


def _rec(u, gates, positions, conv_w, conv_b, inv_freq, sign, *, batch, n_chunks, to_bf16=()):
    L = CHUNK
    seq = n_chunks * L
    assert all(w.ndim == 2 and w.shape[0] % (n_chunks * 2 * SUBLANES) == 0 for w in to_bf16)
    slab = lambda w: pl.BlockSpec((w.shape[0] // n_chunks, w.shape[1]), lambda c: (c, 0))
    r_q = (2 * M_QK_W + 2 * M_V_W) // R_QK_W
    r_v = (2 * M_QK_W + 2 * M_V_W + 2 * R_QK_W) // R_V_W
    const = lambda shape: pl.BlockSpec(shape, lambda c: (0, 0))
    rows = lambda width, col: pl.BlockSpec((batch, L, width), lambda c: (0, c, col))
    u3 = u.reshape(batch, seq, u.shape[1])
    hm, hr, *casts = pl.pallas_call(
        functools.partial(_rec_kernel, batch, len(to_bf16)),
        out_shape=(jax.ShapeDtypeStruct((batch, seq, M_V_W), BF16),
                   jax.ShapeDtypeStruct((batch, seq, R_V_W), BF16),
                   *[jax.ShapeDtypeStruct(w.shape, BF16) for w in to_bf16]),
        grid=(n_chunks,),
        in_specs=[
            rows(M_QK_W, 0),
            rows(M_QK_W, 1),
            rows(M_V_W, 1),
            rows(GATE_PAD, 0),
            const((CONV_K, 2 * M_QK_W)),
            const((1, 2 * M_QK_W)),
            rows(R_QK_W, r_q),
            rows(R_QK_W, r_q + 1),
            rows(R_V_W, r_v),
            pl.BlockSpec((batch, 1, 1, L), lambda c: (0, c, 0, 0)),
            const((1, R_DQK)),
            const((1, R_DQK)),
            *[slab(w) for w in to_bf16],
        ],
        out_specs=(rows(M_V_W, 0), rows(R_V_W, 0), *[slab(w) for w in to_bf16]),
        scratch_shapes=[
            pltpu.VMEM((batch, M_HEADS, M_DQK, M_DV), F32),
            pltpu.VMEM((batch, SUBLANES, M_DQK), F32),
            pltpu.VMEM((batch, SUBLANES, LANES), F32),
            pltpu.VMEM((batch, 2 * L, 2 * M_QK_W), BF16),
            pltpu.VMEM((CONV_K * L, 2 * L), BF16),
            pltpu.VMEM((batch, R_HEADS, R_DQK, R_DV), F32),
            pltpu.VMEM((R_HEADS, L, L), F32),
            pltpu.VMEM((R_HEADS, L, 1), F32),
            pltpu.VMEM((R_HEADS, L, 1), F32),
        ],
        compiler_params=pltpu.CompilerParams(
            dimension_semantics=("arbitrary",), vmem_limit_bytes=48 * MIB),
        name="rec",
    )(u3, u3, u3, gates.reshape(batch, seq, GATE_PAD), conv_w, conv_b, u3, u3, u3,
      positions.reshape(batch, n_chunks, 1, L), inv_freq, sign, *to_bf16)
    return (hm.reshape(batch * seq, M_V_W), hr.reshape(batch * seq, R_V_W), *casts)


def _load_weight_bf16(w_hbm, w_scr, stage, sem):
    n_slots, rows, _ = stage.shape
    n = w_hbm.shape[0] // rows

    def copy(c, slot):
        src = w_hbm.at[pl.ds(pl.multiple_of(c * rows, rows), rows), :]
        return pltpu.make_async_copy(src, stage.at[slot], sem.at[slot])

    copy(0, 0).start()

    def body(c, carry):
        slot = lax.rem(c, n_slots)

        @pl.when(c + 1 < n)
        def _():
            copy(c + 1, lax.rem(c + 1, n_slots)).start()

        copy(c, slot).wait()
        w_scr[pl.ds(pl.multiple_of(c * rows, rows), rows), :] = stage[slot].astype(BF16)
        return carry

    lax.fori_loop(0, n, body, 0)


def _branch_kernel(hh_ref, ho_ref, mo_ref, rg_ref, ga_ref, gb_ref, mng_ref, rng_ref, wm_hbm, wr_hbm,
                   out_ref, hm_scr, hr_scr, wm_ref, wr_ref, stage_ref, sem_ref):
    @pl.when(pl.program_id(0) == 0)
    def _():
        _load_weight_bf16(wm_hbm, wm_ref, stage_ref, sem_ref)
        _load_weight_bf16(wr_hbm, wr_ref, stage_ref, sem_ref)

    for h in range(M_HEADS):
        sl = slice(h * M_DV, (h + 1) * M_DV)
        y = _head_norm(hh_ref[:, sl].astype(F32), mng_ref[:, sl]) * _sigmoid(mo_ref[:, sl].astype(F32))
        hm_scr[:, sl] = y.astype(BF16)
    for h in range(R_HEADS):
        sl = slice(h * R_DV, (h + 1) * R_DV)
        sg = rg_ref[:, sl].astype(F32)
        y = _head_norm(ho_ref[:, sl].astype(F32), rng_ref[:, sl]) * (sg * _sigmoid(sg))
        hr_scr[:, sl] = y.astype(BF16)

    ya = _dot(hm_scr[...], wm_ref[...])
    yb = _dot(hr_scr[...], wr_ref[...])
    merged = _sigmoid(ga_ref[...].astype(F32)) * ya + _sigmoid(gb_ref[...].astype(F32)) * yb
    out_ref[...] = merged.astype(out_ref.dtype)


def _branch(hh, ho, u, m_norm_g, r_norm_g, wm, wr, *, bm=256, stage_rows=256):
    m, d = hh.shape[0], wm.shape[1]
    o_blk = (2 * M_QK_W + M_V_W) // M_V_W
    s_blk = (2 * M_QK_W + 2 * M_V_W + 2 * R_QK_W + R_V_W) // R_V_W
    g_a = (2 * M_QK_W + 2 * M_V_W + 2 * R_QK_W + 2 * R_V_W) // d
    assert wm.shape == (M_V_W, d) and wr.shape == (R_V_W, d) and M_V_W == R_V_W == d
    assert m % bm == 0 and d % stage_rows == 0
    const = lambda shape: pl.BlockSpec(shape, lambda i: (0, 0))
    in_hbm = pl.BlockSpec(memory_space=pl.ANY)
    return pl.pallas_call(
        _branch_kernel,
        out_shape=jax.ShapeDtypeStruct((m, d), BF16),
        grid=(m // bm,),
        in_specs=[
            pl.BlockSpec((bm, M_V_W), lambda i: (i, 0)),
            pl.BlockSpec((bm, R_V_W), lambda i: (i, 0)),
            pl.BlockSpec((bm, M_V_W), lambda i: (i, o_blk)),
            pl.BlockSpec((bm, R_V_W), lambda i: (i, s_blk)),
            pl.BlockSpec((bm, d), lambda i: (i, g_a)),
            pl.BlockSpec((bm, d), lambda i: (i, g_a + 1)),
            const((1, M_V_W)),
            const((1, R_V_W)),
            in_hbm,
            in_hbm,
        ],
        out_specs=pl.BlockSpec((bm, d), lambda i: (i, 0)),
        scratch_shapes=[
            pltpu.VMEM((bm, M_V_W), BF16),
            pltpu.VMEM((bm, R_V_W), BF16),
            pltpu.VMEM((M_V_W, d), BF16),
            pltpu.VMEM((R_V_W, d), BF16),
            pltpu.VMEM((2, stage_rows, d), F32),
            pltpu.SemaphoreType.DMA((2,)),
        ],
        compiler_params=pltpu.CompilerParams(
            dimension_semantics=("arbitrary",), vmem_limit_bytes=52 * MIB),
        name="branch",
    )(hh, ho, u, u, u, u, m_norm_g, r_norm_g, wm, wr)


def _outproj_kernel(alpha, mg_ref, x_ref, wo_hbm, bo_ref, lg_ref, lb_ref, out_ref, wo_ref, stage_ref, sem_ref):
    @pl.when(pl.program_id(0) == 0)
    def _():
        _load_weight_bf16(wo_hbm, wo_ref, stage_ref, sem_ref)

    bm = out_ref.shape[0]
    for r0 in range(0, bm, bm // 2):
        rs = slice(r0, r0 + bm // 2)
        y = _dot(mg_ref[rs, :], wo_ref[...]) + bo_ref[...]
        out_ref[rs, :] = _layer_norm(alpha * x_ref[rs, :] + y, lg_ref[...], lb_ref[...])


def _outproj(merged, x2d, wo, bo, ln_g, ln_b, *, alpha, bm=512, stage_rows=256):
    m, d = x2d.shape
    assert wo.shape == (d, d) and m % bm == 0 and d % stage_rows == 0
    const = lambda shape: pl.BlockSpec(shape, lambda i: (0, 0))
    return pl.pallas_call(
        functools.partial(_outproj_kernel, alpha),
        out_shape=jax.ShapeDtypeStruct((m, d), F32),
        grid=(m // bm,),
        in_specs=[
            pl.BlockSpec((bm, d), lambda i: (i, 0)),
            pl.BlockSpec((bm, d), lambda i: (i, 0)),
            pl.BlockSpec(memory_space=pl.ANY),
            const((1, d)),
            const((1, d)),
            const((1, d)),
        ],
        out_specs=pl.BlockSpec((bm, d), lambda i: (i, 0)),
        scratch_shapes=[
            pltpu.VMEM((d, d), BF16),
            pltpu.VMEM((2, stage_rows, d), F32),
            pltpu.SemaphoreType.DMA((2,)),
        ],
        compiler_params=pltpu.CompilerParams(
            dimension_semantics=("arbitrary",), vmem_limit_bytes=48 * MIB),
        name="outproj",
    )(merged, x2d, wo, bo, ln_g, ln_b)


def _mlp_kernel(alpha, x_ref, w1_ref, b1_ref, w2_ref, b2_ref, lg_ref, lb_ref, out_ref, xb_ref):
    j = pl.program_id(1)

    @pl.when(j == 0)
    def _():
        xb_ref[...] = x_ref[...].astype(BF16)
        out_ref[...] = jnp.zeros_like(out_ref)

    hid = jnp.maximum(_dot(xb_ref[...], w1_ref[...]) + b1_ref[...], 0.0)
    out_ref[...] += _dot((hid * hid).astype(BF16), w2_ref[...])

    @pl.when(j == pl.num_programs(1) - 1)
    def _():
        z = alpha * x_ref[...] + (out_ref[...] + b2_ref[...])
        out_ref[...] = _layer_norm(z, lg_ref[...], lb_ref[...])


def _mlp(x2d, w1, b1, w2, b2, ln_g, ln_b, *, alpha, bm=512, bf=1024):
    m, d = x2d.shape
    f = w1.shape[1]
    return pl.pallas_call(
        functools.partial(_mlp_kernel, alpha),
        out_shape=jax.ShapeDtypeStruct((m, d), F32),
        grid=(m // bm, f // bf),
        in_specs=[
            pl.BlockSpec((bm, d), lambda i, j: (i, 0)),
            pl.BlockSpec((d, bf), lambda i, j: (0, j)),
            pl.BlockSpec((1, bf), lambda i, j: (0, j)),
            pl.BlockSpec((bf, d), lambda i, j: (j, 0)),
            pl.BlockSpec((1, d), lambda i, j: (0, 0)),
            pl.BlockSpec((1, d), lambda i, j: (0, 0)),
            pl.BlockSpec((1, d), lambda i, j: (0, 0)),
        ],
        out_specs=pl.BlockSpec((bm, d), lambda i, j: (i, 0)),
        scratch_shapes=[pltpu.VMEM((bm, d), BF16)],
        compiler_params=pltpu.CompilerParams(
            dimension_semantics=("arbitrary", "arbitrary"), vmem_limit_bytes=48 * MIB),
        name="mlp",
    )(x2d, w1, b1, w2, b2, ln_g, ln_b)


def kernel(x, positions, w_in, b_in, m_conv_w, m_conv_b, m_norm_g, r_norm_g, w_branch_m, w_branch_r,
           w_out, b_out, ln1_g, ln1_b, w_ff1, b_ff1, w_ff2, b_ff2, ln2_g, ln2_b):
    batch, seq, d = x.shape
    depth = w_in.shape[0]
    n_chunks = seq // CHUNK
    alpha = (2.0 * depth) ** 0.25
    gate_lo = 2 * M_QK_W + 2 * M_V_W

    half = R_DQK // 2
    inv_freq = ROPE_BASE ** (-jnp.arange(half, dtype=F32) / half)
    inv_freq = jnp.concatenate([inv_freq, inv_freq]).reshape(1, R_DQK)
    sign = jnp.concatenate([-jnp.ones((half,), F32), jnp.ones((half,), F32)]).reshape(1, R_DQK)
    h = x.reshape(batch * seq, d)
    for l in range(depth):
        u, gates = _proj(h, w_in[l].T, b_in[l].reshape(1, -1),
                         gate_lo=gate_lo, n_gate=2 * M_HEADS)
        hh, ho, w1b, w2b = _rec(u, gates, positions, m_conv_w[l], m_conv_b[l].reshape(1, -1), inv_freq, sign,
                                batch=batch, n_chunks=n_chunks, to_bf16=(w_ff1[l], w_ff2[l]))
        merged = _branch(hh, ho, u, m_norm_g[l].reshape(1, -1), r_norm_g[l].reshape(1, -1),
                         w_branch_m[l], w_branch_r[l])
        h1 = _outproj(merged, h, w_out[l], b_out[l].reshape(1, -1), ln1_g[l].reshape(1, -1),
                      ln1_b[l].reshape(1, -1), alpha=alpha)
        h = _mlp(h1, w1b, b_ff1[l].reshape(1, -1), w2b,
                 b_ff2[l].reshape(1, -1), ln2_g[l].reshape(1, -1), ln2_b[l].reshape(1, -1), alpha=alpha)
    return h.reshape(batch, seq, d)
```

```python
import functools
import math

import jax
import jax.numpy as jnp
from jax import lax
from jax.experimental import pallas as pl
from jax.experimental.pallas import tpu as pltpu

M_HEADS, M_DQK, M_DV = 4, 256, 512
R_HEADS, R_DQK, R_DV = 8, 128, 256
CONV_K = 4
CHUNK = 128
ROPE_BASE = 10000.0
LN_EPS = 1e-5
M_QK_W = M_HEADS * M_DQK
M_V_W = M_HEADS * M_DV
R_QK_W = R_HEADS * R_DQK
R_V_W = R_HEADS * R_DV

LANES = 128
SUBLANES = 8
GATE_PAD = LANES
MIB = 1024 * 1024

F32 = jnp.float32
BF16 = jnp.bfloat16

def _sigmoid(x):
    return 1.0 / (1.0 + jnp.exp(-x))


def _log_sigmoid(x):
    return jnp.minimum(x, 0.0) - jnp.log1p(jnp.exp(-jnp.abs(x)))


def _dot(a, b):
    return jnp.dot(a, b, preferred_element_type=F32)


def _dot_nt(a, b):
    return lax.dot_general(a, b, (((1,), (1,)), ((), ())), preferred_element_type=F32)


def _layer_norm(z, g, b):
    mu = jnp.mean(z, axis=-1, keepdims=True)
    d = z - mu
    var = jnp.mean(d * d, axis=-1, keepdims=True)
    return d * lax.rsqrt(var + LN_EPS) * g + b


def _head_norm(h, g):
    mu = jnp.mean(h, axis=-1, keepdims=True)
    d = h - mu
    var = jnp.mean(d * d, axis=-1, keepdims=True)
    return d * lax.rsqrt(var + LN_EPS) * g


def _col_tile(step, gate_blk, n_tiles):
    return lax.rem(step + gate_blk, n_tiles)


def _proj_kernel(gate_blk, n_gate, x_ref, wa_ref, wb_ref, ba_ref, bb_ref, u_ref, g_ref, xb_scr):
    j = pl.program_id(1)
    c = _col_tile(j, gate_blk, pl.num_programs(1))
    bn = wa_ref.shape[0]

    @pl.when(j == 0)
    def _():
        xb_scr[...] = x_ref[...].astype(BF16)

    tail = c >= gate_blk
    w_shift = jnp.concatenate([wa_ref[n_gate:bn, :], wb_ref[...]], axis=0)
    w = jnp.where(tail, w_shift, wa_ref[...]).astype(BF16)
    b_shift = jnp.concatenate([ba_ref[...], bb_ref[...]], axis=1)[:, n_gate:n_gate + bn]
    b = jnp.where(tail, b_shift, ba_ref[...])
    xb = xb_scr[...]
    u_ref[...] = (_dot_nt(xb, w) + b).astype(u_ref.dtype)

    @pl.when(j == 0)
    def _():
        g_ref[...] = _dot_nt(xb, wa_ref[0:GATE_PAD, :].astype(BF16)) + ba_ref[:, 0:GATE_PAD]


def _proj(x2d, w_t, b, *, gate_lo, n_gate, bm=1024, bn=1024):
    m, d = x2d.shape
    n = w_t.shape[0] - n_gate
    assert gate_lo % bn == 0 and n % bn == 0 and m % bm == 0
    assert n_gate % SUBLANES == 0 and n_gate <= LANES
    gate_blk = gate_lo // bn
    n_j = n // bn
    col = lambda j: _col_tile(j, gate_blk, n_j)
    return pl.pallas_call(
        functools.partial(_proj_kernel, gate_blk, n_gate),
        out_shape=(jax.ShapeDtypeStruct((m, n), BF16), jax.ShapeDtypeStruct((m, GATE_PAD), F32)),
        grid=(m // bm, n_j),
        in_specs=[
            pl.BlockSpec((bm, d), lambda i, j: (i, 0)),
            pl.BlockSpec((bn, d), lambda i, j: (col(j), 0)),
            pl.BlockSpec((n_gate, d), lambda i, j: ((col(j) + 1) * (bn // n_gate), 0)),
            pl.BlockSpec((1, bn), lambda i, j: (0, col(j))),
            pl.BlockSpec((1, LANES), lambda i, j: (0, (col(j) + 1) * (bn // LANES))),
        ],
        out_specs=(
            pl.BlockSpec((bm, bn), lambda i, j: (i, col(j))),
            pl.BlockSpec((bm, GATE_PAD), lambda i, j: (i, 0)),
        ),
        scratch_shapes=[pltpu.VMEM((bm, d), BF16)],
        compiler_params=pltpu.CompilerParams(
            dimension_semantics=("arbitrary", "arbitrary"), vmem_limit_bytes=56 * MIB),
        name="proj",
    )(x2d, w_t, w_t, b, b)


def _conv_shift_matrix():
    L = CHUNK
    row = lax.broadcasted_iota(jnp.int32, (L, 2 * L), 0)
    col = lax.broadcasted_iota(jnp.int32, (L, 2 * L), 1)
    blocks = [jnp.where(col == row + (L - (CONV_K - 1) + j), 1.0, 0.0) for j in range(CONV_K)]
    return jnp.concatenate(blocks, axis=0).astype(BF16)


def _mlstm_chunk(q_ref, k_ref, v_ref, g_ref, cw_ref, cb_ref, out_ref, c_ref, n_ref, m_ref,
                 x2_ref, shift_ref):
    L = CHUNK

    x2_ref[L:2 * L, 0:M_QK_W] = q_ref[...]
    x2_ref[L:2 * L, M_QK_W:2 * M_QK_W] = k_ref[...]
    taps = _dot(shift_ref[...], x2_ref[...])
    conv = cb_ref[...]
    for j in range(CONV_K):
        conv = conv + cw_ref[j:j + 1, :] * taps[j * L:(j + 1) * L, :]
    x2_ref[0:L, :] = x2_ref[L:2 * L, :]
    qk = conv * _sigmoid(conv)

    g = g_ref[...]
    lfg = _log_sigmoid(g)
    g_rows = g.T[0:2 * M_HEADS, :]
    lf_rows = _log_sigmoid(g_rows)

    row_t = lax.broadcasted_iota(jnp.int32, (L, L), 0)
    col_s = lax.broadcasted_iota(jnp.int32, (L, L), 1)
    causal = col_s <= row_t
    lane = lax.broadcasted_iota(jnp.int32, (L, GATE_PAD), 1)

    heads = range(M_HEADS)
    qh = [qk[:, h * M_DQK:(h + 1) * M_DQK] * (M_DQK ** -0.5) for h in heads]
    kh = [qk[:, M_QK_W + h * M_DQK:M_QK_W + (h + 1) * M_DQK] for h in heads]
    qb = [q.astype(BF16) for q in qh]
    vb = [v_ref[:, h * M_DV:(h + 1) * M_DV] for h in heads]
    scores = [_dot_nt(qb[h], kh[h].astype(BF16)) for h in heads]
    inter = [_dot(qb[h], c_ref[h].astype(BF16)) for h in heads]
    yield

    gate = []
    for h in heads:
        li_row = g_rows[h:h + 1, :]
        lf_row = lf_rows[M_HEADS + h:M_HEADS + h + 1, :]
        li_col = jnp.sum(jnp.where(lane == h, g, 0.0), axis=1, keepdims=True)
        lf_col = jnp.sum(jnp.where(lane == M_HEADS + h, lfg, 0.0), axis=1, keepdims=True)
        b_col = jnp.sum(jnp.where(causal, lf_row, 0.0), axis=1, keepdims=True)
        b_row = jnp.sum(jnp.where(row_t <= col_s, lf_col, 0.0), axis=0, keepdims=True)
        g_tot = jnp.sum(lf_row, axis=1, keepdims=True)
        m_prev = m_ref[h:h + 1, 0:1]

        d_log = jnp.where(causal, b_col - b_row + li_row, -jnp.inf)
        m_inter = b_col + m_prev
        m_t = jnp.maximum(m_inter, jnp.max(d_log, axis=1, keepdims=True))
        w_intra = jnp.exp(d_log - m_t)
        w_inter = jnp.exp(m_inter - m_t)

        w_log_row = g_tot - b_row + li_row
        w_log_col = g_tot - b_col + li_col
        m_new = jnp.maximum(g_tot + m_prev, jnp.max(w_log_row, axis=1, keepdims=True))
        w_s = jnp.exp(w_log_col - m_new)
        decay = jnp.exp(g_tot + m_prev - m_new)
        gate.append((m_t, w_intra, w_inter, m_new, w_s, decay))
    yield

    for h in heads:
        m_t, w_intra, w_inter, m_new, w_s, decay = gate[h]
        n_old = n_ref[h:h + 1, :]
        s = scores[h] * w_intra
        num = _dot(s.astype(BF16), vb[h]) + w_inter * inter[h]
        den = (jnp.sum(s, axis=1, keepdims=True)
               + w_inter * jnp.sum(qh[h] * n_old, axis=1, keepdims=True))
        hh = num * (1.0 / jnp.maximum(jnp.abs(den), jnp.exp(-m_t)))

        kw = kh[h] * w_s
        c_ref[h] = decay * c_ref[h] + _dot(kw.T.astype(BF16), vb[h])
        n_ref[h:h + 1, :] = decay * n_old + jnp.sum(kw, axis=0, keepdims=True)
        m_ref[h:h + 1, :] = jnp.broadcast_to(m_new, (1, LANES))

        out_ref[:, h * M_DV:(h + 1) * M_DV] = hh.astype(out_ref.dtype)


def _retention_constants(decay_ref, xi_ref, zeta_ref):
    L = CHUNK
    t_col = lax.broadcasted_iota(jnp.int32, (L, 1), 0).astype(F32)
    rel = (lax.broadcasted_iota(jnp.int32, (L, L), 0)
           - lax.broadcasted_iota(jnp.int32, (L, L), 1)).astype(F32)
    scale = R_DQK ** -0.5
    for h in range(R_HEADS):
        log_gamma = math.log1p(-(2.0 ** (-5.0 - h)))
        decay_ref[h] = jnp.where(rel >= 0.0, jnp.exp(jnp.maximum(rel, 0.0) * log_gamma), 0.0) * scale
        xi_ref[h] = jnp.exp((t_col + 1.0) * log_gamma) * scale
        zeta_ref[h] = jnp.exp((L - 1.0 - t_col) * log_gamma)


def _retention_chunk(q_ref, k_ref, v_ref, pos_ref, invf_ref, sign_ref, out_ref, r_ref,
                     decay_ref, xi_ref, zeta_ref):
    L = CHUNK
    half = R_DQK // 2
    assert L == R_DQK
    lane = lax.broadcasted_iota(jnp.int32, (L // 2, R_DQK), 1)
    pos_rows = jnp.broadcast_to(pos_ref[0].astype(F32), (L, L)).T
    ang = jnp.where(lane < half, pos_rows[0:L // 2, :], pos_rows[L // 2:L, :]) * invf_ref[...]

    def unpack(packed):
        swapped = pltpu.roll(packed, half, 1)
        top = jnp.where(lane < half, packed, swapped)
        bottom = jnp.where(lane < half, swapped, packed)
        return jnp.concatenate([top, bottom], axis=0)

    cos = unpack(jnp.cos(ang))
    sin = unpack(jnp.sin(ang)) * sign_ref[...]

    heads = range(R_HEADS)
    qr, kr = [], []
    for h in heads:
        qs = slice(h * R_DQK, (h + 1) * R_DQK)
        xq = q_ref[:, qs].astype(F32)
        xk = k_ref[:, qs].astype(F32)
        qr.append((xq * cos + pltpu.roll(xq, half, 1) * sin).astype(BF16))
        kr.append(xk * cos + pltpu.roll(xk, half, 1) * sin)
    vb = [v_ref[:, h * R_DV:(h + 1) * R_DV] for h in heads]
    scores = [_dot_nt(qr[h], kr[h].astype(BF16)) for h in heads]
    cross = [_dot(qr[h], r_ref[h].astype(BF16)) for h in heads]
    yield

    for h in heads:
        g_chunk = math.exp(L * math.log1p(-(2.0 ** (-5.0 - h))))
        p = scores[h] * decay_ref[h]
        o = _dot(p.astype(BF16), vb[h]) + cross[h] * xi_ref[h]
        r_ref[h] = g_chunk * r_ref[h] + _dot((kr[h] * zeta_ref[h]).T.astype(BF16), vb[h])
        out_ref[:, h * R_DV:(h + 1) * R_DV] = o.astype(out_ref.dtype)


def _rec_kernel(batch, n_cast, mq_ref, mk_ref, mv_ref, g_ref, cw_ref, cb_ref,
                rq_ref, rk_ref, rv_ref, pos_ref, invf_ref, sign_ref, *refs):
    cast_in, refs = refs[:n_cast], refs[n_cast:]
    hm_ref, hr_ref = refs[:2]
    cast_out, refs = refs[2:2 + n_cast], refs[2 + n_cast:]
    c_ref, n_ref, m_ref, x2_ref, shift_ref, r_ref, decay_ref, xi_ref, zeta_ref = refs

    @pl.when(pl.program_id(0) == 0)
    def _():
        shift_ref[...] = _conv_shift_matrix()
        _retention_constants(decay_ref, xi_ref, zeta_ref)
        c_ref[...] = jnp.zeros_like(c_ref)
        n_ref[...] = jnp.zeros_like(n_ref)
        m_ref[...] = jnp.zeros_like(m_ref)
        r_ref[...] = jnp.zeros_like(r_ref)
        x2_ref[:, 0:CHUNK, :] = jnp.zeros((batch, CHUNK, 2 * M_QK_W), BF16)

    mlstm = [_mlstm_chunk(mq_ref.at[b], mk_ref.at[b], mv_ref.at[b], g_ref.at[b], cw_ref, cb_ref,
                          hm_ref.at[b], c_ref.at[b], n_ref.at[b], m_ref.at[b], x2_ref.at[b], shift_ref)
             for b in range(batch)]
    retention = [_retention_chunk(rq_ref.at[b], rk_ref.at[b], rv_ref.at[b], pos_ref.at[b], invf_ref,
                                  sign_ref, hr_ref.at[b], r_ref.at[b], decay_ref, xi_ref, zeta_ref)
                 for b in range(batch)]
    for gens in (mlstm, retention):
        live = list(gens)
        while live:
            live = [g for g in live if next(g, StopIteration) is not StopIteration]

    for src, dst in zip(cast_in, cast_out):
        dst[...] = src[...].astype(BF16)


def _rec(u, gates, positions, conv_w, conv_b, inv_freq, sign, *, batch, n_chunks, to_bf16=()):
    L = CHUNK
    seq = n_chunks * L
    assert all(w.ndim == 2 and w.shape[0] % (n_chunks * 2 * SUBLANES) == 0 for w in to_bf16)
    slab = lambda w: pl.BlockSpec((w.shape[0] // n_chunks, w.shape[1]), lambda c: (c, 0))
    r_q = (2 * M_QK_W + 2 * M_V_W) // R_QK_W
    r_v = (2 * M_QK_W + 2 * M_V_W + 2 * R_QK_W) // R_V_W
    const = lambda shape: pl.BlockSpec(shape, lambda c: (0, 0))
    rows = lambda width, col: pl.BlockSpec((batch, L, width), lambda c: (0, c, col))
    u3 = u.reshape(batch, seq, u.shape[1])
    hm, hr, *casts = pl.pallas_call(
        functools.partial(_rec_kernel, batch, len(to_bf16)),
        out_shape=(jax.ShapeDtypeStruct((batch, seq, M_V_W), BF16),
                   jax.ShapeDtypeStruct((batch, seq, R_V_W), BF16),
                   *[jax.ShapeDtypeStruct(w.shape, BF16) for w in to_bf16]),
        grid=(n_chunks,),
        in_specs=[
            rows(M_QK_W, 0),
            rows(M_QK_W, 1),
            rows(M_V_W, 1),
            rows(GATE_PAD, 0),
            const((CONV_K, 2 * M_QK_W)),
            const((1, 2 * M_QK_W)),
            rows(R_QK_W, r_q),
            rows(R_QK_W, r_q + 1),
            rows(R_V_W, r_v),
            pl.BlockSpec((batch, 1, 1, L), lambda c: (0, c, 0, 0)),
            const((1, R_DQK)),
            const((1, R_DQK)),
            *[slab(w) for w in to_bf16],
        ],
        out_specs=(rows(M_V_W, 0), rows(R_V_W, 0), *[slab(w) for w in to_bf16]),
        scratch_shapes=[
            pltpu.VMEM((batch, M_HEADS, M_DQK, M_DV), F32),
            pltpu.VMEM((batch, SUBLANES, M_DQK), F32),
            pltpu.VMEM((batch, SUBLANES, LANES), F32),
            pltpu.VMEM((batch, 2 * L, 2 * M_QK_W), BF16),
            pltpu.VMEM((CONV_K * L, 2 * L), BF16),
            pltpu.VMEM((batch, R_HEADS, R_DQK, R_DV), F32),
            pltpu.VMEM((R_HEADS, L, L), F32),
            pltpu.VMEM((R_HEADS, L, 1), F32),
            pltpu.VMEM((R_HEADS, L, 1), F32),
        ],
        compiler_params=pltpu.CompilerParams(
            dimension_semantics=("arbitrary",), vmem_limit_bytes=48 * MIB),
        name="rec",
    )(u3, u3, u3, gates.reshape(batch, seq, GATE_PAD), conv_w, conv_b, u3, u3, u3,
      positions.reshape(batch, n_chunks, 1, L), inv_freq, sign, *to_bf16)
    return (hm.reshape(batch * seq, M_V_W), hr.reshape(batch * seq, R_V_W), *casts)


def _load_weight_bf16(w_hbm, w_scr, stage, sem):
    n_slots, rows, _ = stage.shape
    n = w_hbm.shape[0] // rows

    def copy(c, slot):
        src = w_hbm.at[pl.ds(pl.multiple_of(c * rows, rows), rows), :]
        return pltpu.make_async_copy(src, stage.at[slot], sem.at[slot])

    copy(0, 0).start()

    def body(c, carry):
        slot = lax.rem(c, n_slots)

        @pl.when(c + 1 < n)
        def _():
            copy(c + 1, lax.rem(c + 1, n_slots)).start()

        copy(c, slot).wait()
        w_scr[pl.ds(pl.multiple_of(c * rows, rows), rows), :] = stage[slot].astype(BF16)
        return carry

    lax.fori_loop(0, n, body, 0)


def _branch_kernel(hh_ref, ho_ref, mo_ref, rg_ref, ga_ref, gb_ref, mng_ref, rng_ref, wm_hbm, wr_hbm,
                   out_ref, hm_scr, hr_scr, wm_ref, wr_ref, stage_ref, sem_ref):
    @pl.when(pl.program_id(0) == 0)
    def _():
        _load_weight_bf16(wm_hbm, wm_ref, stage_ref, sem_ref)
        _load_weight_bf16(wr_hbm, wr_ref, stage_ref, sem_ref)

    for h in range(M_HEADS):
        sl = slice(h * M_DV, (h + 1) * M_DV)
        y = _head_norm(hh_ref[:, sl].astype(F32), mng_ref[:, sl]) * _sigmoid(mo_ref[:, sl].astype(F32))
        hm_scr[:, sl] = y.astype(BF16)
    for h in range(R_HEADS):
        sl = slice(h * R_DV, (h + 1) * R_DV)
        sg = rg_ref[:, sl].astype(F32)
        y = _head_norm(ho_ref[:, sl].astype(F32), rng_ref[:, sl]) * (sg * _sigmoid(sg))
        hr_scr[:, sl] = y.astype(BF16)

    ya = _dot(hm_scr[...], wm_ref[...])
    yb = _dot(hr_scr[...], wr_ref[...])
    merged = _sigmoid(ga_ref[...].astype(F32)) * ya + _sigmoid(gb_ref[...].astype(F32)) * yb
    out_ref[...] = merged.astype(out_ref.dtype)


def _branch(hh, ho, u, m_norm_g, r_norm_g, wm, wr, *, bm=256, stage_rows=256):
    m, d = hh.shape[0], wm.shape[1]
    o_blk = (2 * M_QK_W + M_V_W) // M_V_W
    s_blk = (2 * M_QK_W + 2 * M_V_W + 2 * R_QK_W + R_V_W) // R_V_W
    g_a = (2 * M_QK_W + 2 * M_V_W + 2 * R_QK_W + 2 * R_V_W) // d
    assert wm.shape == (M_V_W, d) and wr.shape == (R_V_W, d) and M_V_W == R_V_W == d
    assert m % bm == 0 and d % stage_rows == 0
    const = lambda shape: pl.BlockSpec(shape, lambda i: (0, 0))
    in_hbm = pl.BlockSpec(memory_space=pl.ANY)
    return pl.pallas_call(
        _branch_kernel,
        out_shape=jax.ShapeDtypeStruct((m, d), BF16),
        grid=(m // bm,),
        in_specs=[
            pl.BlockSpec((bm, M_V_W), lambda i: (i, 0)),
            pl.BlockSpec((bm, R_V_W), lambda i: (i, 0)),
            pl.BlockSpec((bm, M_V_W), lambda i: (i, o_blk)),
            pl.BlockSpec((bm, R_V_W), lambda i: (i, s_blk)),
            pl.BlockSpec((bm, d), lambda i: (i, g_a)),
            pl.BlockSpec((bm, d), lambda i: (i, g_a + 1)),
            const((1, M_V_W)),
            const((1, R_V_W)),
            in_hbm,
            in_hbm,
        ],
        out_specs=pl.BlockSpec((bm, d), lambda i: (i, 0)),
        scratch_shapes=[
            pltpu.VMEM((bm, M_V_W), BF16),
            pltpu.VMEM((bm, R_V_W), BF16),
            pltpu.VMEM((M_V_W, d), BF16),
            pltpu.VMEM((R_V_W, d), BF16),
            pltpu.VMEM((2, stage_rows, d), F32),
            pltpu.SemaphoreType.DMA((2,)),
        ],
        compiler_params=pltpu.CompilerParams(
            dimension_semantics=("arbitrary",), vmem_limit_bytes=52 * MIB),
        name="branch",
    )(hh, ho, u, u, u, u, m_norm_g, r_norm_g, wm, wr)


def _outproj_kernel(alpha, mg_ref, x_ref, wo_hbm, bo_ref, lg_ref, lb_ref, out_ref, wo_ref, stage_ref, sem_ref):
    @pl.when(pl.program_id(0) == 0)
    def _():
        _load_weight_bf16(wo_hbm, wo_ref, stage_ref, sem_ref)

    bm = out_ref.shape[0]
    for r0 in range(0, bm, bm // 2):
        rs = slice(r0, r0 + bm // 2)
        y = _dot(mg_ref[rs, :], wo_ref[...]) + bo_ref[...]
        out_ref[rs, :] = _layer_norm(alpha * x_ref[rs, :] + y, lg_ref[...], lb_ref[...])


def _outproj(merged, x2d, wo, bo, ln_g, ln_b, *, alpha, bm=512, stage_rows=256):
    m, d = x2d.shape
    assert wo.shape == (d, d) and m % bm == 0 and d % stage_rows == 0
    const = lambda shape: pl.BlockSpec(shape, lambda i: (0, 0))
    return pl.pallas_call(
        functools.partial(_outproj_kernel, alpha),
        out_shape=jax.ShapeDtypeStruct((m, d), F32),
        grid=(m // bm,),
        in_specs=[
            pl.BlockSpec((bm, d), lambda i: (i, 0)),
            pl.BlockSpec((bm, d), lambda i: (i, 0)),
            pl.BlockSpec(memory_space=pl.ANY),
            const((1, d)),
            const((1, d)),
            const((1, d)),
        ],
        out_specs=pl.BlockSpec((bm, d), lambda i: (i, 0)),
        scratch_shapes=[
            pltpu.VMEM((d, d), BF16),
            pltpu.VMEM((2, stage_rows, d), F32),
            pltpu.SemaphoreType.DMA((2,)),
        ],
        compiler_params=pltpu.CompilerParams(
            dimension_semantics=("arbitrary",), vmem_limit_bytes=48 * MIB),
        name="outproj",
    )(merged, x2d, wo, bo, ln_g, ln_b)


def _mlp_kernel(alpha, x_ref, w1_ref, b1_ref, w2_ref, b2_ref, lg_ref, lb_ref, out_ref, xb_ref):
    j = pl.program_id(1)

    @pl.when(j == 0)
    def _():
        xb_ref[...] = x_ref[...].astype(BF16)
        out_ref[...] = jnp.zeros_like(out_ref)

    hid = jnp.maximum(_dot(xb_ref[...], w1_ref[...]) + b1_ref[...], 0.0)
    out_ref[...] += _dot((hid * hid).astype(BF16), w2_ref[...])

    @pl.when(j == pl.num_programs(1) - 1)
    def _():
        z = alpha * x_ref[...] + (out_ref[...] + b2_ref[...])
        out_ref[...] = _layer_norm(z, lg_ref[...], lb_ref[...])


def _mlp(x2d, w1, b1, w2, b2, ln_g, ln_b, *, alpha, bm=512, bf=2048):
    m, d = x2d.shape
    f = w1.shape[1]
    return pl.pallas_call(
        functools.partial(_mlp_kernel, alpha),
        out_shape=jax.ShapeDtypeStruct((m, d), F32),
        grid=(m // bm, f // bf),
        in_specs=[
            pl.BlockSpec((bm, d), lambda i, j: (i, 0)),
            pl.BlockSpec((d, bf), lambda i, j: (0, j)),
            pl.BlockSpec((1, bf), lambda i, j: (0, j)),
            pl.BlockSpec((bf, d), lambda i, j: (j, 0)),
            pl.BlockSpec((1, d), lambda i, j: (0, 0)),
            pl.BlockSpec((1, d), lambda i, j: (0, 0)),
            pl.BlockSpec((1, d), lambda i, j: (0, 0)),
        ],
        out_specs=pl.BlockSpec((bm, d), lambda i, j: (i, 0)),
        scratch_shapes=[pltpu.VMEM((bm, d), BF16)],
        compiler_params=pltpu.CompilerParams(
            dimension_semantics=("arbitrary", "arbitrary"), vmem_limit_bytes=60 * MIB),
        name="mlp",
    )(x2d, w1, b1, w2, b2, ln_g, ln_b)


def kernel(x, positions, w_in, b_in, m_conv_w, m_conv_b, m_norm_g, r_norm_g, w_branch_m, w_branch_r,
           w_out, b_out, ln1_g, ln1_b, w_ff1, b_ff1, w_ff2, b_ff2, ln2_g, ln2_b):
    batch, seq, d = x.shape
    depth = w_in.shape[0]
    n_chunks = seq // CHUNK
    alpha = (2.0 * depth) ** 0.25
    gate_lo = 2 * M_QK_W + 2 * M_V_W

    half = R_DQK // 2
    inv_freq = ROPE_BASE ** (-jnp.arange(half, dtype=F32) / half)
    inv_freq = jnp.concatenate([inv_freq, inv_freq]).reshape(1, R_DQK)
    sign = jnp.concatenate([-jnp.ones((half,), F32), jnp.ones((half,), F32)]).reshape(1, R_DQK)
    h = x.reshape(batch * seq, d)
    for l in range(depth):
        u, gates = _proj(h, w_in[l].T, b_in[l].reshape(1, -1),
                         gate_lo=gate_lo, n_gate=2 * M_HEADS)
        hh, ho, w1b, w2b = _rec(u, gates, positions, m_conv_w[l], m_conv_b[l].reshape(1, -1), inv_freq, sign,
                                batch=batch, n_chunks=n_chunks, to_bf16=(w_ff1[l], w_ff2[l]))
        merged = _branch(hh, ho, u, m_norm_g[l].reshape(1, -1), r_norm_g[l].reshape(1, -1),
                         w_branch_m[l], w_branch_r[l])
        h1 = _outproj(merged, h, w_out[l], b_out[l].reshape(1, -1), ln1_g[l].reshape(1, -1),
                      ln1_b[l].reshape(1, -1), alpha=alpha)
        h = _mlp(h1, w1b, b_ff1[l].reshape(1, -1), w2b,
                 b_ff2[l].reshape(1, -1), ln2_g[l].reshape(1, -1), ln2_b[l].reshape(1, -1), alpha=alpha)
    return h.reshape(batch, seq, d)
```

```python
import functools
import math

import jax
import jax.numpy as jnp
from jax import lax
from jax.experimental import pallas as pl
from jax.experimental.pallas import tpu as pltpu

M_HEADS, M_DQK, M_DV = 4, 256, 512
R_HEADS, R_DQK, R_DV = 8, 128, 256
CONV_K = 4
CHUNK = 128
ROPE_BASE = 10000.0
LN_EPS = 1e-5
M_QK_W = M_HEADS * M_DQK
M_V_W = M_HEADS * M_DV
R_QK_W = R_HEADS * R_DQK
R_V_W = R_HEADS * R_DV

LANES = 128
SUBLANES = 8
GATE_PAD = LANES
MIB = 1024 * 1024

F32 = jnp.float32
BF16 = jnp.bfloat16

def _sigmoid(x):
    return 1.0 / (1.0 + jnp.exp(-x))


def _log_sigmoid(x):
    return jnp.minimum(x, 0.0) - jnp.log1p(jnp.exp(-jnp.abs(x)))


def _dot(a, b):
    return jnp.dot(a, b, preferred_element_type=F32)


def _dot_nt(a, b):
    return lax.dot_general(a, b, (((1,), (1,)), ((), ())), preferred_element_type=F32)


def _layer_norm(z, g, b):
    mu = jnp.mean(z, axis=-1, keepdims=True)
    d = z - mu
    var = jnp.mean(d * d, axis=-1, keepdims=True)
    return d * lax.rsqrt(var + LN_EPS) * g + b


def _head_norm(h, g):
    mu = jnp.mean(h, axis=-1, keepdims=True)
    d = h - mu
    var = jnp.mean(d * d, axis=-1, keepdims=True)
    return d * lax.rsqrt(var + LN_EPS) * g


def _col_tile(step, gate_blk, n_tiles):
    return lax.rem(step + gate_blk, n_tiles)


def _proj_kernel(gate_blk, n_gate, x_ref, wa_ref, wb_ref, ba_ref, bb_ref, u_ref, g_ref, xb_scr):
    j = pl.program_id(1)
    c = _col_tile(j, gate_blk, pl.num_programs(1))
    bn = wa_ref.shape[0]

    @pl.when(j == 0)
    def _():
        xb_scr[...] = x_ref[...].astype(BF16)

    tail = c >= gate_blk
    w_shift = jnp.concatenate([wa_ref[n_gate:bn, :], wb_ref[...]], axis=0)
    w = jnp.where(tail, w_shift, wa_ref[...]).astype(BF16)
    b_shift = jnp.concatenate([ba_ref[...], bb_ref[...]], axis=1)[:, n_gate:n_gate + bn]
    b = jnp.where(tail, b_shift, ba_ref[...])
    xb = xb_scr[...]
    u_ref[...] = (_dot_nt(xb, w) + b).astype(u_ref.dtype)

    @pl.when(j == 0)
    def _():
        g_ref[...] = _dot_nt(xb, wa_ref[0:GATE_PAD, :].astype(BF16)) + ba_ref[:, 0:GATE_PAD]


def _proj(x2d, w_t, b, *, gate_lo, n_gate, bm=1024, bn=1024):
    m, d = x2d.shape
    n = w_t.shape[0] - n_gate
    assert gate_lo % bn == 0 and n % bn == 0 and m % bm == 0
    assert n_gate % SUBLANES == 0 and n_gate <= LANES
    gate_blk = gate_lo // bn
    n_j = n // bn
    col = lambda j: _col_tile(j, gate_blk, n_j)
    return pl.pallas_call(
        functools.partial(_proj_kernel, gate_blk, n_gate),
        out_shape=(jax.ShapeDtypeStruct((m, n), BF16), jax.ShapeDtypeStruct((m, GATE_PAD), F32)),
        grid=(m // bm, n_j),
        in_specs=[
            pl.BlockSpec((bm, d), lambda i, j: (i, 0)),
            pl.BlockSpec((bn, d), lambda i, j: (col(j), 0)),
            pl.BlockSpec((n_gate, d), lambda i, j: ((col(j) + 1) * (bn // n_gate), 0)),
            pl.BlockSpec((1, bn), lambda i, j: (0, col(j))),
            pl.BlockSpec((1, LANES), lambda i, j: (0, (col(j) + 1) * (bn // LANES))),
        ],
        out_specs=(
            pl.BlockSpec((bm, bn), lambda i, j: (i, col(j))),
            pl.BlockSpec((bm, GATE_PAD), lambda i, j: (i, 0)),
        ),
        scratch_shapes=[pltpu.VMEM((bm, d), BF16)],
        compiler_params=pltpu.CompilerParams(
            dimension_semantics=("arbitrary", "arbitrary"), vmem_limit_bytes=56 * MIB),
        name="proj",
    )(x2d, w_t, w_t, b, b)


def _conv_shift_matrix():
    L = CHUNK
    row = lax.broadcasted_iota(jnp.int32, (L, 2 * L), 0)
    col = lax.broadcasted_iota(jnp.int32, (L, 2 * L), 1)
    blocks = [jnp.where(col == row + (L - (CONV_K - 1) + j), 1.0, 0.0) for j in range(CONV_K)]
    return jnp.concatenate(blocks, axis=0).astype(BF16)


def _mlstm_chunk(q_ref, k_ref, v_ref, g_ref, cw_ref, cb_ref, out_ref, c_ref, n_ref, m_ref,
                 x2_ref, shift_ref):
    L = CHUNK

    x2_ref[L:2 * L, 0:M_QK_W] = q_ref[...]
    x2_ref[L:2 * L, M_QK_W:2 * M_QK_W] = k_ref[...]
    taps = _dot(shift_ref[...], x2_ref[...])
    conv = cb_ref[...]
    for j in range(CONV_K):
        conv = conv + cw_ref[j:j + 1, :] * taps[j * L:(j + 1) * L, :]
    x2_ref[0:L, :] = x2_ref[L:2 * L, :]
    qk = conv * _sigmoid(conv)

    g = g_ref[...]
    lfg = _log_sigmoid(g)
    g_rows = g.T[0:2 * M_HEADS, :]
    lf_rows = _log_sigmoid(g_rows)

    row_t = lax.broadcasted_iota(jnp.int32, (L, L), 0)
    col_s = lax.broadcasted_iota(jnp.int32, (L, L), 1)
    causal = col_s <= row_t
    lane = lax.broadcasted_iota(jnp.int32, (L, GATE_PAD), 1)

    heads = range(M_HEADS)
    qh = [qk[:, h * M_DQK:(h + 1) * M_DQK] * (M_DQK ** -0.5) for h in heads]
    kh = [qk[:, M_QK_W + h * M_DQK:M_QK_W + (h + 1) * M_DQK] for h in heads]
    qb = [q.astype(BF16) for q in qh]
    vb = [v_ref[:, h * M_DV:(h + 1) * M_DV] for h in heads]
    scores = [_dot_nt(qb[h], kh[h].astype(BF16)) for h in heads]
    inter = [_dot(qb[h], c_ref[h].astype(BF16)) for h in heads]
    yield

    gate = []
    for h in heads:
        li_row = g_rows[h:h + 1, :]
        lf_row = lf_rows[M_HEADS + h:M_HEADS + h + 1, :]
        li_col = jnp.sum(jnp.where(lane == h, g, 0.0), axis=1, keepdims=True)
        lf_col = jnp.sum(jnp.where(lane == M_HEADS + h, lfg, 0.0), axis=1, keepdims=True)
        b_col = jnp.sum(jnp.where(causal, lf_row, 0.0), axis=1, keepdims=True)
        b_row = jnp.sum(jnp.where(row_t <= col_s, lf_col, 0.0), axis=0, keepdims=True)
        g_tot = jnp.sum(lf_row, axis=1, keepdims=True)
        m_prev = m_ref[h:h + 1, 0:1]

        d_log = jnp.where(causal, b_col - b_row + li_row, -jnp.inf)
        m_inter = b_col + m_prev
        m_t = jnp.maximum(m_inter, jnp.max(d_log, axis=1, keepdims=True))
        w_intra = jnp.exp(d_log - m_t)
        w_inter = jnp.exp(m_inter - m_t)

        w_log_row = g_tot - b_row + li_row
        w_log_col = g_tot - b_col + li_col
        m_new = jnp.maximum(g_tot + m_prev, jnp.max(w_log_row, axis=1, keepdims=True))
        w_s = jnp.exp(w_log_col - m_new)
        decay = jnp.exp(g_tot + m_prev - m_new)
        gate.append((m_t, w_intra, w_inter, m_new, w_s, decay))
    yield

    for h in heads:
        m_t, w_intra, w_inter, m_new, w_s, decay = gate[h]
        n_old = n_ref[h:h + 1, :]
        s = scores[h] * w_intra
        num = _dot(s.astype(BF16), vb[h]) + w_inter * inter[h]
        den = (jnp.sum(s, axis=1, keepdims=True)
               + w_inter * jnp.sum(qh[h] * n_old, axis=1, keepdims=True))
        hh = num * (1.0 / jnp.maximum(jnp.abs(den), jnp.exp(-m_t)))

        kw = kh[h] * w_s
        c_ref[h] = decay * c_ref[h] + _dot(kw.T.astype(BF16), vb[h])
        n_ref[h:h + 1, :] = decay * n_old + jnp.sum(kw, axis=0, keepdims=True)
        m_ref[h:h + 1, :] = jnp.broadcast_to(m_new, (1, LANES))

        out_ref[:, h * M_DV:(h + 1) * M_DV] = hh.astype(out_ref.dtype)


def _retention_constants(decay_ref, xi_ref, zeta_ref):
    L = CHUNK
    t_col = lax.broadcasted_iota(jnp.int32, (L, 1), 0).astype(F32)
    rel = (lax.broadcasted_iota(jnp.int32, (L, L), 0)
           - lax.broadcasted_iota(jnp.int32, (L, L), 1)).astype(F32)
    scale = R_DQK ** -0.5
    for h in range(R_HEADS):
        log_gamma = math.log1p(-(2.0 ** (-5.0 - h)))
        decay_ref[h] = jnp.where(rel >= 0.0, jnp.exp(jnp.maximum(rel, 0.0) * log_gamma), 0.0) * scale
        xi_ref[h] = jnp.exp((t_col + 1.0) * log_gamma) * scale
        zeta_ref[h] = jnp.exp((L - 1.0 - t_col) * log_gamma)


def _retention_chunk(q_ref, k_ref, v_ref, pos_ref, invf_ref, sign_ref, out_ref, r_ref,
                     decay_ref, xi_ref, zeta_ref):
    L = CHUNK
    half = R_DQK // 2
    assert L == R_DQK
    lane = lax.broadcasted_iota(jnp.int32, (L // 2, R_DQK), 1)
    pos_rows = jnp.broadcast_to(pos_ref[0].astype(F32), (L, L)).T
    ang = jnp.where(lane < half, pos_rows[0:L // 2, :], pos_rows[L // 2:L, :]) * invf_ref[...]

    def unpack(packed):
        swapped = pltpu.roll(packed, half, 1)
        top = jnp.where(lane < half, packed, swapped)
        bottom = jnp.where(lane < half, swapped, packed)
        return jnp.concatenate([top, bottom], axis=0)

    cos = unpack(jnp.cos(ang))
    sin = unpack(jnp.sin(ang)) * sign_ref[...]

    heads = range(R_HEADS)
    qr, kr = [], []
    for h in heads:
        qs = slice(h * R_DQK, (h + 1) * R_DQK)
        xq = q_ref[:, qs].astype(F32)
        xk = k_ref[:, qs].astype(F32)
        qr.append((xq * cos + pltpu.roll(xq, half, 1) * sin).astype(BF16))
        kr.append(xk * cos + pltpu.roll(xk, half, 1) * sin)
    vb = [v_ref[:, h * R_DV:(h + 1) * R_DV] for h in heads]
    scores = [_dot_nt(qr[h], kr[h].astype(BF16)) for h in heads]
    cross = [_dot(qr[h], r_ref[h].astype(BF16)) for h in heads]
    yield

    for h in heads:
        g_chunk = math.exp(L * math.log1p(-(2.0 ** (-5.0 - h))))
        p = scores[h] * decay_ref[h]
        o = _dot(p.astype(BF16), vb[h]) + cross[h] * xi_ref[h]
        r_ref[h] = g_chunk * r_ref[h] + _dot((kr[h] * zeta_ref[h]).T.astype(BF16), vb[h])
        out_ref[:, h * R_DV:(h + 1) * R_DV] = o.astype(out_ref.dtype)


def _rec_kernel(batch, n_cast, mq_ref, mk_ref, mv_ref, g_ref, cw_ref, cb_ref,
                rq_ref, rk_ref, rv_ref, pos_ref, invf_ref, sign_ref, *refs):
    cast_in, refs = refs[:n_cast], refs[n_cast:]
    hm_ref, hr_ref = refs[:2]
    cast_out, refs = refs[2:2 + n_cast], refs[2 + n_cast:]
    c_ref, n_ref, m_ref, x2_ref, shift_ref, r_ref, decay_ref, xi_ref, zeta_ref = refs

    @pl.when(pl.program_id(0) == 0)
    def _():
        shift_ref[...] = _conv_shift_matrix()
        _retention_constants(decay_ref, xi_ref, zeta_ref)
        c_ref[...] = jnp.zeros_like(c_ref)
        n_ref[...] = jnp.zeros_like(n_ref)
        m_ref[...] = jnp.zeros_like(m_ref)
        r_ref[...] = jnp.zeros_like(r_ref)
        x2_ref[:, 0:CHUNK, :] = jnp.zeros((batch, CHUNK, 2 * M_QK_W), BF16)

    mlstm = [_mlstm_chunk(mq_ref.at[b], mk_ref.at[b], mv_ref.at[b], g_ref.at[b], cw_ref, cb_ref,
                          hm_ref.at[b], c_ref.at[b], n_ref.at[b], m_ref.at[b], x2_ref.at[b], shift_ref)
             for b in range(batch)]
    retention = [_retention_chunk(rq_ref.at[b], rk_ref.at[b], rv_ref.at[b], pos_ref.at[b], invf_ref,
                                  sign_ref, hr_ref.at[b], r_ref.at[b], decay_ref, xi_ref, zeta_ref)
                 for b in range(batch)]
    for gens in (mlstm, retention):
        live = list(gens)
        while live:
            live = [g for g in live if next(g, StopIteration) is not StopIteration]

    for src, dst in zip(cast_in, cast_out):
        dst[...] = src[...].astype(BF16)


def _rec(u, gates, positions, conv_w, conv_b, inv_freq, sign, *, batch, n_chunks, to_bf16=()):
    L = CHUNK
    seq = n_chunks * L
    assert all(w.ndim == 2 and w.shape[0] % (n_chunks * 2 * SUBLANES) == 0 for w in to_bf16)
    slab = lambda w: pl.BlockSpec((w.shape[0] // n_chunks, w.shape[1]), lambda c: (c, 0))
    r_q = (2 * M_QK_W + 2 * M_V_W) // R_QK_W
    r_v = (2 * M_QK_W + 2 * M_V_W + 2 * R_QK_W) // R_V_W
    const = lambda shape: pl.BlockSpec(shape, lambda c: (0, 0))
    rows = lambda width, col: pl.BlockSpec((batch, L, width), lambda c: (0, c, col))
    u3 = u.reshape(batch, seq, u.shape[1])
    hm, hr, *casts = pl.pallas_call(
        functools.partial(_rec_kernel, batch, len(to_bf16)),
        out_shape=(jax.ShapeDtypeStruct((batch, seq, M_V_W), BF16),
                   jax.ShapeDtypeStruct((batch, seq, R_V_W), BF16),
                   *[jax.ShapeDtypeStruct(w.shape, BF16) for w in to_bf16]),
        grid=(n_chunks,),
        in_specs=[
            rows(M_QK_W, 0),
            rows(M_QK_W, 1),
            rows(M_V_W, 1),
            rows(GATE_PAD, 0),
            const((CONV_K, 2 * M_QK_W)),
            const((1, 2 * M_QK_W)),
            rows(R_QK_W, r_q),
            rows(R_QK_W, r_q + 1),
            rows(R_V_W, r_v),
            pl.BlockSpec((batch, 1, 1, L), lambda c: (0, c, 0, 0)),
            const((1, R_DQK)),
            const((1, R_DQK)),
            *[slab(w) for w in to_bf16],
        ],
        out_specs=(rows(M_V_W, 0), rows(R_V_W, 0), *[slab(w) for w in to_bf16]),
        scratch_shapes=[
            pltpu.VMEM((batch, M_HEADS, M_DQK, M_DV), F32),
            pltpu.VMEM((batch, SUBLANES, M_DQK), F32),
            pltpu.VMEM((batch, SUBLANES, LANES), F32),
            pltpu.VMEM((batch, 2 * L, 2 * M_QK_W), BF16),
            pltpu.VMEM((CONV_K * L, 2 * L), BF16),
            pltpu.VMEM((batch, R_HEADS, R_DQK, R_DV), F32),
            pltpu.VMEM((R_HEADS, L, L), F32),
            pltpu.VMEM((R_HEADS, L, 1), F32),
            pltpu.VMEM((R_HEADS, L, 1), F32),
        ],
        compiler_params=pltpu.CompilerParams(
            dimension_semantics=("arbitrary",), vmem_limit_bytes=48 * MIB),
        name="rec",
    )(u3, u3, u3, gates.reshape(batch, seq, GATE_PAD), conv_w, conv_b, u3, u3, u3,
      positions.reshape(batch, n_chunks, 1, L), inv_freq, sign, *to_bf16)
    return (hm.reshape(batch * seq, M_V_W), hr.reshape(batch * seq, R_V_W), *casts)


def _resident(shape):
    return pl.BlockSpec(shape, lambda i: (0,) * len(shape), pipeline_mode=pl.Buffered(1))


def _branch_kernel(hh_ref, ho_ref, mo_ref, rg_ref, ga_ref, gb_ref, mng_ref, rng_ref, wm_ref, wr_ref,
                   out_ref, hm_scr, hr_scr):
    def m_heads(h0, h1):
        for h in range(h0, h1):
            sl = slice(h * M_DV, (h + 1) * M_DV)
            y = _head_norm(hh_ref[:, sl].astype(F32), mng_ref[:, sl]) * _sigmoid(mo_ref[:, sl].astype(F32))
            hm_scr[:, sl] = y.astype(BF16)
        rows = slice(h0 * M_DV, h1 * M_DV)
        return _dot(hm_scr[:, rows], wm_ref[rows, :])

    def r_heads(h0, h1):
        for h in range(h0, h1):
            sl = slice(h * R_DV, (h + 1) * R_DV)
            sg = rg_ref[:, sl].astype(F32)
            y = _head_norm(ho_ref[:, sl].astype(F32), rng_ref[:, sl]) * (sg * _sigmoid(sg))
            hr_scr[:, sl] = y.astype(BF16)
        rows = slice(h0 * R_DV, h1 * R_DV)
        return _dot(hr_scr[:, rows], wr_ref[rows, :])

    ya = m_heads(0, M_HEADS // 2) + m_heads(M_HEADS // 2, M_HEADS)
    yb = r_heads(0, R_HEADS // 2) + r_heads(R_HEADS // 2, R_HEADS)
    merged = _sigmoid(ga_ref[...].astype(F32)) * ya + _sigmoid(gb_ref[...].astype(F32)) * yb
    out_ref[...] = merged.astype(out_ref.dtype)


def _branch(hh, ho, u, m_norm_g, r_norm_g, wm, wr, *, bm=512):
    m, d = hh.shape[0], wm.shape[1]
    o_blk = (2 * M_QK_W + M_V_W) // M_V_W
    s_blk = (2 * M_QK_W + 2 * M_V_W + 2 * R_QK_W + R_V_W) // R_V_W
    g_a = (2 * M_QK_W + 2 * M_V_W + 2 * R_QK_W + 2 * R_V_W) // d
    assert wm.shape == (M_V_W, d) and wr.shape == (R_V_W, d) and M_V_W == R_V_W == d
    assert m % bm == 0
    return pl.pallas_call(
        _branch_kernel,
        out_shape=jax.ShapeDtypeStruct((m, d), BF16),
        grid=(m // bm,),
        in_specs=[
            pl.BlockSpec((bm, M_V_W), lambda i: (i, 0)),
            pl.BlockSpec((bm, R_V_W), lambda i: (i, 0)),
            pl.BlockSpec((bm, M_V_W), lambda i: (i, o_blk)),
            pl.BlockSpec((bm, R_V_W), lambda i: (i, s_blk)),
            pl.BlockSpec((bm, d), lambda i: (i, g_a)),
            pl.BlockSpec((bm, d), lambda i: (i, g_a + 1)),
            _resident((1, M_V_W)),
            _resident((1, R_V_W)),
            _resident((M_V_W, d)),
            _resident((R_V_W, d)),
        ],
        out_specs=pl.BlockSpec((bm, d), lambda i: (i, 0)),
        scratch_shapes=[
            pltpu.VMEM((bm, M_V_W), BF16),
            pltpu.VMEM((bm, R_V_W), BF16),
        ],
        compiler_params=pltpu.CompilerParams(
            dimension_semantics=("arbitrary",), vmem_limit_bytes=56 * MIB),
        name="branch",
    )(hh, ho, u, u, u, u, m_norm_g, r_norm_g, wm, wr)


def _outproj_kernel(alpha, mg_ref, x_ref, wo_ref, bo_ref, lg_ref, lb_ref, out_ref):
    bm = out_ref.shape[0]
    for r0 in range(0, bm, bm // 2):
        rs = slice(r0, r0 + bm // 2)
        y = _dot(mg_ref[rs, :], wo_ref[...]) + bo_ref[...]
        out_ref[rs, :] = _layer_norm(alpha * x_ref[rs, :] + y, lg_ref[...], lb_ref[...])


def _outproj(merged, x2d, wo, bo, ln_g, ln_b, *, alpha, bm=512):
    m, d = x2d.shape
    assert wo.shape == (d, d) and m % bm == 0
    return pl.pallas_call(
        functools.partial(_outproj_kernel, alpha),
        out_shape=jax.ShapeDtypeStruct((m, d), F32),
        grid=(m // bm,),
        in_specs=[
            pl.BlockSpec((bm, d), lambda i: (i, 0)),
            pl.BlockSpec((bm, d), lambda i: (i, 0)),
            _resident((d, d)),
            _resident((1, d)),
            _resident((1, d)),
            _resident((1, d)),
        ],
        out_specs=pl.BlockSpec((bm, d), lambda i: (i, 0)),
        compiler_params=pltpu.CompilerParams(
            dimension_semantics=("arbitrary",), vmem_limit_bytes=48 * MIB),
        name="outproj",
    )(merged, x2d, wo, bo, ln_g, ln_b)


def _mlp_kernel(alpha, x_ref, w1_ref, b1_ref, w2_ref, b2_ref, lg_ref, lb_ref, out_ref, xb_ref):
    j = pl.program_id(1)

    @pl.when(j == 0)
    def _():
        xb_ref[...] = x_ref[...].astype(BF16)
        out_ref[...] = jnp.zeros_like(out_ref)

    hid = jnp.maximum(_dot(xb_ref[...], w1_ref[...]) + b1_ref[...], 0.0)
    out_ref[...] += _dot((hid * hid).astype(BF16), w2_ref[...])

    @pl.when(j == pl.num_programs(1) - 1)
    def _():
        z = alpha * x_ref[...] + (out_ref[...] + b2_ref[...])
        out_ref[...] = _layer_norm(z, lg_ref[...], lb_ref[...])


def _mlp(x2d, w1, b1, w2, b2, ln_g, ln_b, *, alpha, bm=512, bf=2048):
    m, d = x2d.shape
    f = w1.shape[1]
    return pl.pallas_call(
        functools.partial(_mlp_kernel, alpha),
        out_shape=jax.ShapeDtypeStruct((m, d), F32),
        grid=(m // bm, f // bf),
        in_specs=[
            pl.BlockSpec((bm, d), lambda i, j: (i, 0)),
            pl.BlockSpec((d, bf), lambda i, j: (0, j)),
            pl.BlockSpec((1, bf), lambda i, j: (0, j)),
            pl.BlockSpec((bf, d), lambda i, j: (j, 0)),
            pl.BlockSpec((1, d), lambda i, j: (0, 0)),
            pl.BlockSpec((1, d), lambda i, j: (0, 0)),
            pl.BlockSpec((1, d), lambda i, j: (0, 0)),
        ],
        out_specs=pl.BlockSpec((bm, d), lambda i, j: (i, 0)),
        scratch_shapes=[pltpu.VMEM((bm, d), BF16)],
        compiler_params=pltpu.CompilerParams(
            dimension_semantics=("arbitrary", "arbitrary"), vmem_limit_bytes=60 * MIB),
        name="mlp",
    )(x2d, w1, b1, w2, b2, ln_g, ln_b)


def kernel(x, positions, w_in, b_in, m_conv_w, m_conv_b, m_norm_g, r_norm_g, w_branch_m, w_branch_r,
           w_out, b_out, ln1_g, ln1_b, w_ff1, b_ff1, w_ff2, b_ff2, ln2_g, ln2_b):
    batch, seq, d = x.shape
    depth = w_in.shape[0]
    n_chunks = seq // CHUNK
    alpha = (2.0 * depth) ** 0.25
    gate_lo = 2 * M_QK_W + 2 * M_V_W

    half = R_DQK // 2
    inv_freq = ROPE_BASE ** (-jnp.arange(half, dtype=F32) / half)
    inv_freq = jnp.concatenate([inv_freq, inv_freq]).reshape(1, R_DQK)
    sign = jnp.concatenate([-jnp.ones((half,), F32), jnp.ones((half,), F32)]).reshape(1, R_DQK)
    h = x.reshape(batch * seq, d)
    for l in range(depth):
        u, gates = _proj(h, w_in[l].T, b_in[l].reshape(1, -1),
                         gate_lo=gate_lo, n_gate=2 * M_HEADS)
        hh, ho, wmb, wrb, wob, w1b, w2b = _rec(
            u, gates, positions, m_conv_w[l], m_conv_b[l].reshape(1, -1), inv_freq, sign,
            batch=batch, n_chunks=n_chunks,
            to_bf16=(w_branch_m[l], w_branch_r[l], w_out[l], w_ff1[l], w_ff2[l]))
        merged = _branch(hh, ho, u, m_norm_g[l].reshape(1, -1), r_norm_g[l].reshape(1, -1), wmb, wrb)
        h1 = _outproj(merged, h, wob, b_out[l].reshape(1, -1), ln1_g[l].reshape(1, -1),
                      ln1_b[l].reshape(1, -1), alpha=alpha)
        h = _mlp(h1, w1b, b_ff1[l].reshape(1, -1), w2b,
                 b_ff2[l].reshape(1, -1), ln2_g[l].reshape(1, -1), ln2_b[l].reshape(1, -1), alpha=alpha)
    return h.reshape(batch, seq, d)
```

```python
import functools
import math

import jax
import jax.numpy as jnp
from jax import lax
from jax.experimental import pallas as pl
from jax.experimental.pallas import tpu as pltpu

M_HEADS, M_DQK, M_DV = 4, 256, 512
R_HEADS, R_DQK, R_DV = 8, 128, 256
CONV_K = 4
CHUNK = 128
ROPE_BASE = 10000.0
LN_EPS = 1e-5
M_QK_W = M_HEADS * M_DQK
M_V_W = M_HEADS * M_DV
R_QK_W = R_HEADS * R_DQK
R_V_W = R_HEADS * R_DV

LANES = 128
SUBLANES = 8
GATE_PAD = LANES
MIB = 1024 * 1024

F32 = jnp.float32
BF16 = jnp.bfloat16

def _sigmoid(x):
    return 1.0 / (1.0 + jnp.exp(-x))


def _log_sigmoid(x):
    return jnp.minimum(x, 0.0) - jnp.log1p(jnp.exp(-jnp.abs(x)))


def _dot(a, b):
    return jnp.dot(a, b, preferred_element_type=F32)


def _dot_nt(a, b):
    return lax.dot_general(a, b, (((1,), (1,)), ((), ())), preferred_element_type=F32)


def _layer_norm(z, g, b):
    mu = jnp.mean(z, axis=-1, keepdims=True)
    d = z - mu
    var = jnp.mean(d * d, axis=-1, keepdims=True)
    return d * lax.rsqrt(var + LN_EPS) * g + b


def _head_norm(h, g):
    mu = jnp.mean(h, axis=-1, keepdims=True)
    d = h - mu
    var = jnp.mean(d * d, axis=-1, keepdims=True)
    return d * lax.rsqrt(var + LN_EPS) * g


def _col_tile(step, gate_blk, n_tiles):
    return lax.rem(step + gate_blk, n_tiles)


def _proj_kernel(gate_blk, n_gate, x_ref, w_hbm, wg_ref, ba_ref, bb_ref, bg_ref, u_ref, g_ref,
                 xb_scr, w_ring, sem):
    i, j = pl.program_id(0), pl.program_id(1)
    n_j = pl.num_programs(1)
    step = i * n_j + j
    n_steps = pl.num_programs(0) * n_j
    n_slots, bn, _ = w_ring.shape
    c = _col_tile(j, gate_blk, n_j)

    def weight_copy(s):
        cs = _col_tile(lax.rem(s, n_j), gate_blk, n_j)
        row0 = pl.multiple_of(cs * bn + jnp.where(cs >= gate_blk, n_gate, 0), SUBLANES)
        slot = lax.rem(s, n_slots)
        return pltpu.make_async_copy(w_hbm.at[pl.ds(row0, bn), :], w_ring.at[slot], sem.at[slot])

    @pl.when(step == 0)
    def _():
        for s in range(n_slots - 1):
            weight_copy(jnp.int32(s)).start()

    @pl.when(step + (n_slots - 1) < n_steps)
    def _():
        weight_copy(step + (n_slots - 1)).start()

    @pl.when(j == 0)
    def _():
        xb_scr[...] = x_ref[...].astype(BF16)

    weight_copy(step).wait()
    w = w_ring[lax.rem(step, n_slots)].astype(BF16)
    b_shift = jnp.concatenate([ba_ref[...], bb_ref[...]], axis=1)[:, n_gate:n_gate + bn]
    b = jnp.where(c >= gate_blk, b_shift, ba_ref[...])
    xb = xb_scr[...]
    u_ref[...] = (_dot_nt(xb, w) + b).astype(u_ref.dtype)

    @pl.when(j == 0)
    def _():
        g_ref[...] = _dot_nt(xb, wg_ref[...].astype(BF16)) + bg_ref[...]


def _proj(x2d, w_t, b, *, gate_lo, n_gate, bm=1024, bn=1024, n_slots=3):
    m, d = x2d.shape
    n = w_t.shape[0] - n_gate
    assert gate_lo % bn == 0 and n % bn == 0 and m % bm == 0
    assert n_gate % SUBLANES == 0 and n_gate <= LANES and gate_lo % GATE_PAD == 0
    gate_blk = gate_lo // bn
    n_j = n // bn
    assert (m // bm) * n_j >= n_slots - 1
    col = lambda j: _col_tile(j, gate_blk, n_j)
    fixed = lambda shape, blk: pl.BlockSpec(shape, lambda i, j: blk, pipeline_mode=pl.Buffered(1))
    return pl.pallas_call(
        functools.partial(_proj_kernel, gate_blk, n_gate),
        out_shape=(jax.ShapeDtypeStruct((m, n), BF16), jax.ShapeDtypeStruct((m, GATE_PAD), F32)),
        grid=(m // bm, n_j),
        in_specs=[
            pl.BlockSpec((bm, d), lambda i, j: (i, 0)),
            pl.BlockSpec(memory_space=pl.ANY),
            fixed((GATE_PAD, d), (gate_lo // GATE_PAD, 0)),
            pl.BlockSpec((1, bn), lambda i, j: (0, col(j))),
            pl.BlockSpec((1, LANES), lambda i, j: (0, (col(j) + 1) * (bn // LANES))),
            fixed((1, GATE_PAD), (0, gate_lo // GATE_PAD)),
        ],
        out_specs=(
            pl.BlockSpec((bm, bn), lambda i, j: (i, col(j))),
            pl.BlockSpec((bm, GATE_PAD), lambda i, j: (i, 0)),
        ),
        scratch_shapes=[
            pltpu.VMEM((bm, d), BF16),
            pltpu.VMEM((n_slots, bn, d), F32),
            pltpu.SemaphoreType.DMA((n_slots,)),
        ],
        compiler_params=pltpu.CompilerParams(
            dimension_semantics=("arbitrary", "arbitrary"), vmem_limit_bytes=60 * MIB),
        name="proj",
    )(x2d, w_t, w_t, b, b, b)


def _conv_shift_matrix():
    L = CHUNK
    row = lax.broadcasted_iota(jnp.int32, (L, 2 * L), 0)
    col = lax.broadcasted_iota(jnp.int32, (L, 2 * L), 1)
    blocks = [jnp.where(col == row + (L - (CONV_K - 1) + j), 1.0, 0.0) for j in range(CONV_K)]
    return jnp.concatenate(blocks, axis=0).astype(BF16)


def _mlstm_chunk(q_ref, k_ref, v_ref, g_ref, cw_ref, cb_ref, out_ref, c_ref, n_ref, m_ref,
                 x2_ref, shift_ref):
    L = CHUNK

    x2_ref[L:2 * L, 0:M_QK_W] = q_ref[...]
    x2_ref[L:2 * L, M_QK_W:2 * M_QK_W] = k_ref[...]
    taps = _dot(shift_ref[...], x2_ref[...])
    conv = cb_ref[...]
    for j in range(CONV_K):
        conv = conv + cw_ref[j:j + 1, :] * taps[j * L:(j + 1) * L, :]
    x2_ref[0:L, :] = x2_ref[L:2 * L, :]
    qk = conv * _sigmoid(conv)

    g = g_ref[...]
    lfg = _log_sigmoid(g)
    g_rows = g.T[0:2 * M_HEADS, :]
    lf_rows = _log_sigmoid(g_rows)

    row_t = lax.broadcasted_iota(jnp.int32, (L, L), 0)
    col_s = lax.broadcasted_iota(jnp.int32, (L, L), 1)
    causal = col_s <= row_t
    lane = lax.broadcasted_iota(jnp.int32, (L, GATE_PAD), 1)

    heads = range(M_HEADS)
    qh = [qk[:, h * M_DQK:(h + 1) * M_DQK] * (M_DQK ** -0.5) for h in heads]
    kh = [qk[:, M_QK_W + h * M_DQK:M_QK_W + (h + 1) * M_DQK] for h in heads]
    qb = [q.astype(BF16) for q in qh]
    vb = [v_ref[:, h * M_DV:(h + 1) * M_DV] for h in heads]
    scores = [_dot_nt(qb[h], kh[h].astype(BF16)) for h in heads]
    inter = [_dot(qb[h], c_ref[h].astype(BF16)) for h in heads]
    yield

    gate = []
    for h in heads:
        li_row = g_rows[h:h + 1, :]
        lf_row = lf_rows[M_HEADS + h:M_HEADS + h + 1, :]
        li_col = jnp.sum(jnp.where(lane == h, g, 0.0), axis=1, keepdims=True)
        lf_col = jnp.sum(jnp.where(lane == M_HEADS + h, lfg, 0.0), axis=1, keepdims=True)
        b_col = jnp.sum(jnp.where(causal, lf_row, 0.0), axis=1, keepdims=True)
        b_row = jnp.sum(jnp.where(row_t <= col_s, lf_col, 0.0), axis=0, keepdims=True)
        g_tot = jnp.sum(lf_row, axis=1, keepdims=True)
        m_prev = m_ref[h:h + 1, 0:1]

        d_log = jnp.where(causal, b_col - b_row + li_row, -jnp.inf)
        m_inter = b_col + m_prev
        m_t = jnp.maximum(m_inter, jnp.max(d_log, axis=1, keepdims=True))
        w_intra = jnp.exp(d_log - m_t)
        w_inter = jnp.exp(m_inter - m_t)

        w_log_row = g_tot - b_row + li_row
        w_log_col = g_tot - b_col + li_col
        m_new = jnp.maximum(g_tot + m_prev, jnp.max(w_log_row, axis=1, keepdims=True))
        w_s = jnp.exp(w_log_col - m_new)
        decay = jnp.exp(g_tot + m_prev - m_new)
        gate.append((m_t, w_intra, w_inter, m_new, w_s, decay))
    yield

    for h in heads:
        m_t, w_intra, w_inter, m_new, w_s, decay = gate[h]
        n_old = n_ref[h:h + 1, :]
        s = scores[h] * w_intra
        num = _dot(s.astype(BF16), vb[h]) + w_inter * inter[h]
        den = (jnp.sum(s, axis=1, keepdims=True)
               + w_inter * jnp.sum(qh[h] * n_old, axis=1, keepdims=True))
        hh = num * (1.0 / jnp.maximum(jnp.abs(den), jnp.exp(-m_t)))

        kw = kh[h] * w_s
        c_ref[h] = decay * c_ref[h] + _dot(kw.T.astype(BF16), vb[h])
        n_ref[h:h + 1, :] = decay * n_old + jnp.sum(kw, axis=0, keepdims=True)
        m_ref[h:h + 1, :] = jnp.broadcast_to(m_new, (1, LANES))

        out_ref[:, h * M_DV:(h + 1) * M_DV] = hh.astype(out_ref.dtype)


def _retention_constants(decay_ref, xi_ref, zeta_ref):
    L = CHUNK
    t_col = lax.broadcasted_iota(jnp.int32, (L, 1), 0).astype(F32)
    rel = (lax.broadcasted_iota(jnp.int32, (L, L), 0)
           - lax.broadcasted_iota(jnp.int32, (L, L), 1)).astype(F32)
    scale = R_DQK ** -0.5
    for h in range(R_HEADS):
        log_gamma = math.log1p(-(2.0 ** (-5.0 - h)))
        decay_ref[h] = jnp.where(rel >= 0.0, jnp.exp(jnp.maximum(rel, 0.0) * log_gamma), 0.0) * scale
        xi_ref[h] = jnp.exp((t_col + 1.0) * log_gamma) * scale
        zeta_ref[h] = jnp.exp((L - 1.0 - t_col) * log_gamma)


def _retention_chunk(q_ref, k_ref, v_ref, pos_ref, invf_ref, sign_ref, out_ref, r_ref,
                     decay_ref, xi_ref, zeta_ref):
    L = CHUNK
    half = R_DQK // 2
    assert L == R_DQK
    lane = lax.broadcasted_iota(jnp.int32, (L // 2, R_DQK), 1)
    pos_rows = jnp.broadcast_to(pos_ref[0].astype(F32), (L, L)).T
    ang = jnp.where(lane < half, pos_rows[0:L // 2, :], pos_rows[L // 2:L, :]) * invf_ref[...]

    def unpack(packed):
        swapped = pltpu.roll(packed, half, 1)
        top = jnp.where(lane < half, packed, swapped)
        bottom = jnp.where(lane < half, swapped, packed)
        return jnp.concatenate([top, bottom], axis=0)

    cos = unpack(jnp.cos(ang))
    sin = unpack(jnp.sin(ang)) * sign_ref[...]

    heads = range(R_HEADS)
    qr, kr = [], []
    for h in heads:
        qs = slice(h * R_DQK, (h + 1) * R_DQK)
        xq = q_ref[:, qs].astype(F32)
        xk = k_ref[:, qs].astype(F32)
        qr.append((xq * cos + pltpu.roll(xq, half, 1) * sin).astype(BF16))
        kr.append(xk * cos + pltpu.roll(xk, half, 1) * sin)
    vb = [v_ref[:, h * R_DV:(h + 1) * R_DV] for h in heads]
    scores = [_dot_nt(qr[h], kr[h].astype(BF16)) for h in heads]
    cross = [_dot(qr[h], r_ref[h].astype(BF16)) for h in heads]
    yield

    for h in heads:
        g_chunk = math.exp(L * math.log1p(-(2.0 ** (-5.0 - h))))
        p = scores[h] * decay_ref[h]
        o = _dot(p.astype(BF16), vb[h]) + cross[h] * xi_ref[h]
        r_ref[h] = g_chunk * r_ref[h] + _dot((kr[h] * zeta_ref[h]).T.astype(BF16), vb[h])
        out_ref[:, h * R_DV:(h + 1) * R_DV] = o.astype(out_ref.dtype)


def _rec_kernel(batch, n_cast, mq_ref, mk_ref, mv_ref, g_ref, cw_ref, cb_ref,
                rq_ref, rk_ref, rv_ref, pos_ref, invf_ref, sign_ref, *refs):
    cast_in, refs = refs[:n_cast], refs[n_cast:]
    hm_ref, hr_ref = refs[:2]
    cast_out, refs = refs[2:2 + n_cast], refs[2 + n_cast:]
    c_ref, n_ref, m_ref, x2_ref, shift_ref, r_ref, decay_ref, xi_ref, zeta_ref = refs

    @pl.when(pl.program_id(0) == 0)
    def _():
        shift_ref[...] = _conv_shift_matrix()
        _retention_constants(decay_ref, xi_ref, zeta_ref)
        c_ref[...] = jnp.zeros_like(c_ref)
        n_ref[...] = jnp.zeros_like(n_ref)
        m_ref[...] = jnp.zeros_like(m_ref)
        r_ref[...] = jnp.zeros_like(r_ref)
        x2_ref[:, 0:CHUNK, :] = jnp.zeros((batch, CHUNK, 2 * M_QK_W), BF16)

    mlstm = [_mlstm_chunk(mq_ref.at[b], mk_ref.at[b], mv_ref.at[b], g_ref.at[b], cw_ref, cb_ref,
                          hm_ref.at[b], c_ref.at[b], n_ref.at[b], m_ref.at[b], x2_ref.at[b], shift_ref)
             for b in range(batch)]
    retention = [_retention_chunk(rq_ref.at[b], rk_ref.at[b], rv_ref.at[b], pos_ref.at[b], invf_ref,
                                  sign_ref, hr_ref.at[b], r_ref.at[b], decay_ref, xi_ref, zeta_ref)
                 for b in range(batch)]
    for gens in (mlstm, retention):
        live = list(gens)
        while live:
            live = [g for g in live if next(g, StopIteration) is not StopIteration]

    for src, dst in zip(cast_in, cast_out):
        dst[...] = src[...].astype(BF16)


def _rec(u, gates, positions, conv_w, conv_b, inv_freq, sign, *, batch, n_chunks, to_bf16=()):
    L = CHUNK
    seq = n_chunks * L
    assert all(w.ndim == 2 and w.shape[0] % (n_chunks * 2 * SUBLANES) == 0 for w in to_bf16)
    slab = lambda w: pl.BlockSpec((w.shape[0] // n_chunks, w.shape[1]), lambda c: (c, 0))
    r_q = (2 * M_QK_W + 2 * M_V_W) // R_QK_W
    r_v = (2 * M_QK_W + 2 * M_V_W + 2 * R_QK_W) // R_V_W
    const = lambda shape: pl.BlockSpec(shape, lambda c: (0, 0))
    rows = lambda width, col: pl.BlockSpec((batch, L, width), lambda c: (0, c, col))
    u3 = u.reshape(batch, seq, u.shape[1])
    hm, hr, *casts = pl.pallas_call(
        functools.partial(_rec_kernel, batch, len(to_bf16)),
        out_shape=(jax.ShapeDtypeStruct((batch, seq, M_V_W), BF16),
                   jax.ShapeDtypeStruct((batch, seq, R_V_W), BF16),
                   *[jax.ShapeDtypeStruct(w.shape, BF16) for w in to_bf16]),
        grid=(n_chunks,),
        in_specs=[
            rows(M_QK_W, 0),
            rows(M_QK_W, 1),
            rows(M_V_W, 1),
            rows(GATE_PAD, 0),
            const((CONV_K, 2 * M_QK_W)),
            const((1, 2 * M_QK_W)),
            rows(R_QK_W, r_q),
            rows(R_QK_W, r_q + 1),
            rows(R_V_W, r_v),
            pl.BlockSpec((batch, 1, 1, L), lambda c: (0, c, 0, 0)),
            const((1, R_DQK)),
            const((1, R_DQK)),
            *[slab(w) for w in to_bf16],
        ],
        out_specs=(rows(M_V_W, 0), rows(R_V_W, 0), *[slab(w) for w in to_bf16]),
        scratch_shapes=[
            pltpu.VMEM((batch, M_HEADS, M_DQK, M_DV), F32),
            pltpu.VMEM((batch, SUBLANES, M_DQK), F32),
            pltpu.VMEM((batch, SUBLANES, LANES), F32),
            pltpu.VMEM((batch, 2 * L, 2 * M_QK_W), BF16),
            pltpu.VMEM((CONV_K * L, 2 * L), BF16),
            pltpu.VMEM((batch, R_HEADS, R_DQK, R_DV), F32),
            pltpu.VMEM((R_HEADS, L, L), F32),
            pltpu.VMEM((R_HEADS, L, 1), F32),
            pltpu.VMEM((R_HEADS, L, 1), F32),
        ],
        compiler_params=pltpu.CompilerParams(
            dimension_semantics=("arbitrary",), vmem_limit_bytes=48 * MIB),
        name="rec",
    )(u3, u3, u3, gates.reshape(batch, seq, GATE_PAD), conv_w, conv_b, u3, u3, u3,
      positions.reshape(batch, n_chunks, 1, L), inv_freq, sign, *to_bf16)
    return (hm.reshape(batch * seq, M_V_W), hr.reshape(batch * seq, R_V_W), *casts)


def _resident(shape):
    return pl.BlockSpec(shape, lambda i: (0,) * len(shape), pipeline_mode=pl.Buffered(1))


def _branch_kernel(hh_ref, ho_ref, mo_ref, rg_ref, ga_ref, gb_ref, mng_ref, rng_ref, wm_ref, wr_ref,
                   out_ref, hm_scr, hr_scr):
    def m_heads(h0, h1):
        for h in range(h0, h1):
            sl = slice(h * M_DV, (h + 1) * M_DV)
            y = _head_norm(hh_ref[:, sl].astype(F32), mng_ref[:, sl]) * _sigmoid(mo_ref[:, sl].astype(F32))
            hm_scr[:, sl] = y.astype(BF16)
        rows = slice(h0 * M_DV, h1 * M_DV)
        return _dot(hm_scr[:, rows], wm_ref[rows, :])

    def r_heads(h0, h1):
        for h in range(h0, h1):
            sl = slice(h * R_DV, (h + 1) * R_DV)
            sg = rg_ref[:, sl].astype(F32)
            y = _head_norm(ho_ref[:, sl].astype(F32), rng_ref[:, sl]) * (sg * _sigmoid(sg))
            hr_scr[:, sl] = y.astype(BF16)
        rows = slice(h0 * R_DV, h1 * R_DV)
        return _dot(hr_scr[:, rows], wr_ref[rows, :])

    ya = m_heads(0, M_HEADS // 2) + m_heads(M_HEADS // 2, M_HEADS)
    yb = r_heads(0, R_HEADS // 2) + r_heads(R_HEADS // 2, R_HEADS)
    merged = _sigmoid(ga_ref[...].astype(F32)) * ya + _sigmoid(gb_ref[...].astype(F32)) * yb
    out_ref[...] = merged.astype(out_ref.dtype)


def _branch(hh, ho, u, m_norm_g, r_norm_g, wm, wr, *, bm=512):
    m, d = hh.shape[0], wm.shape[1]
    o_blk = (2 * M_QK_W + M_V_W) // M_V_W
    s_blk = (2 * M_QK_W + 2 * M_V_W + 2 * R_QK_W + R_V_W) // R_V_W
    g_a = (2 * M_QK_W + 2 * M_V_W + 2 * R_QK_W + 2 * R_V_W) // d
    assert wm.shape == (M_V_W, d) and wr.shape == (R_V_W, d) and M_V_W == R_V_W == d
    assert m % bm == 0
    return pl.pallas_call(
        _branch_kernel,
        out_shape=jax.ShapeDtypeStruct((m, d), BF16),
        grid=(m // bm,),
        in_specs=[
            pl.BlockSpec((bm, M_V_W), lambda i: (i, 0)),
            pl.BlockSpec((bm, R_V_W), lambda i: (i, 0)),
            pl.BlockSpec((bm, M_V_W), lambda i: (i, o_blk)),
            pl.BlockSpec((bm, R_V_W), lambda i: (i, s_blk)),
            pl.BlockSpec((bm, d), lambda i: (i, g_a)),
            pl.BlockSpec((bm, d), lambda i: (i, g_a + 1)),
            _resident((1, M_V_W)),
            _resident((1, R_V_W)),
            _resident((M_V_W, d)),
            _resident((R_V_W, d)),
        ],
        out_specs=pl.BlockSpec((bm, d), lambda i: (i, 0)),
        scratch_shapes=[
            pltpu.VMEM((bm, M_V_W), BF16),
            pltpu.VMEM((bm, R_V_W), BF16),
        ],
        compiler_params=pltpu.CompilerParams(
            dimension_semantics=("arbitrary",), vmem_limit_bytes=56 * MIB),
        name="branch",
    )(hh, ho, u, u, u, u, m_norm_g, r_norm_g, wm, wr)


def _outproj_kernel(alpha, mg_ref, x_ref, wo_ref, bo_ref, lg_ref, lb_ref, out_ref):
    bm = out_ref.shape[0]
    for r0 in range(0, bm, bm // 2):
        rs = slice(r0, r0 + bm // 2)
        y = _dot(mg_ref[rs, :], wo_ref[...]) + bo_ref[...]
        out_ref[rs, :] = _layer_norm(alpha * x_ref[rs, :] + y, lg_ref[...], lb_ref[...])


def _outproj(merged, x2d, wo, bo, ln_g, ln_b, *, alpha, bm=512):
    m, d = x2d.shape
    assert wo.shape == (d, d) and m % bm == 0
    return pl.pallas_call(
        functools.partial(_outproj_kernel, alpha),
        out_shape=jax.ShapeDtypeStruct((m, d), F32),
        grid=(m // bm,),
        in_specs=[
            pl.BlockSpec((bm, d), lambda i: (i, 0)),
            pl.BlockSpec((bm, d), lambda i: (i, 0)),
            _resident((d, d)),
            _resident((1, d)),
            _resident((1, d)),
            _resident((1, d)),
        ],
        out_specs=pl.BlockSpec((bm, d), lambda i: (i, 0)),
        compiler_params=pltpu.CompilerParams(
            dimension_semantics=("arbitrary",), vmem_limit_bytes=48 * MIB),
        name="outproj",
    )(merged, x2d, wo, bo, ln_g, ln_b)


def _mlp_kernel(alpha, x_ref, w1_ref, b1_ref, w2_ref, b2_ref, lg_ref, lb_ref, out_ref, xb_ref):
    j = pl.program_id(1)

    @pl.when(j == 0)
    def _():
        xb_ref[...] = x_ref[...].astype(BF16)
        out_ref[...] = jnp.zeros_like(out_ref)

    hid = jnp.maximum(_dot(xb_ref[...], w1_ref[...]) + b1_ref[...], 0.0)
    out_ref[...] += _dot((hid * hid).astype(BF16), w2_ref[...])

    @pl.when(j == pl.num_programs(1) - 1)
    def _():
        z = alpha * x_ref[...] + (out_ref[...] + b2_ref[...])
        out_ref[...] = _layer_norm(z, lg_ref[...], lb_ref[...])


def _mlp(x2d, w1, b1, w2, b2, ln_g, ln_b, *, alpha, bm=512, bf=2048):
    m, d = x2d.shape
    f = w1.shape[1]
    return pl.pallas_call(
        functools.partial(_mlp_kernel, alpha),
        out_shape=jax.ShapeDtypeStruct((m, d), F32),
        grid=(m // bm, f // bf),
        in_specs=[
            pl.BlockSpec((bm, d), lambda i, j: (i, 0)),
            pl.BlockSpec((d, bf), lambda i, j: (0, j)),
            pl.BlockSpec((1, bf), lambda i, j: (0, j)),
            pl.BlockSpec((bf, d), lambda i, j: (j, 0)),
            pl.BlockSpec((1, d), lambda i, j: (0, 0)),
            pl.BlockSpec((1, d), lambda i, j: (0, 0)),
            pl.BlockSpec((1, d), lambda i, j: (0, 0)),
        ],
        out_specs=pl.BlockSpec((bm, d), lambda i, j: (i, 0)),
        scratch_shapes=[pltpu.VMEM((bm, d), BF16)],
        compiler_params=pltpu.CompilerParams(
            dimension_semantics=("arbitrary", "arbitrary"), vmem_limit_bytes=60 * MIB),
        name="mlp",
    )(x2d, w1, b1, w2, b2, ln_g, ln_b)


def kernel(x, positions, w_in, b_in, m_conv_w, m_conv_b, m_norm_g, r_norm_g, w_branch_m, w_branch_r,
           w_out, b_out, ln1_g, ln1_b, w_ff1, b_ff1, w_ff2, b_ff2, ln2_g, ln2_b):
    batch, seq, d = x.shape
    depth = w_in.shape[0]
    n_chunks = seq // CHUNK
    alpha = (2.0 * depth) ** 0.25
    gate_lo = 2 * M_QK_W + 2 * M_V_W

    half = R_DQK // 2
    inv_freq = ROPE_BASE ** (-jnp.arange(half, dtype=F32) / half)
    inv_freq = jnp.concatenate([inv_freq, inv_freq]).reshape(1, R_DQK)
    sign = jnp.concatenate([-jnp.ones((half,), F32), jnp.ones((half,), F32)]).reshape(1, R_DQK)
    h = x.reshape(batch * seq, d)
    for l in range(depth):
        u, gates = _proj(h, w_in[l].T, b_in[l].reshape(1, -1),
                         gate_lo=gate_lo, n_gate=2 * M_HEADS)
        hh, ho, wmb, wrb, wob, w1b, w2b = _rec(
            u, gates, positions, m_conv_w[l], m_conv_b[l].reshape(1, -1), inv_freq, sign,
            batch=batch, n_chunks=n_chunks,
            to_bf16=(w_branch_m[l], w_branch_r[l], w_out[l], w_ff1[l], w_ff2[l]))
        merged = _branch(hh, ho, u, m_norm_g[l].reshape(1, -1), r_norm_g[l].reshape(1, -1), wmb, wrb)
        h1 = _outproj(merged, h, wob, b_out[l].reshape(1, -1), ln1_g[l].reshape(1, -1),
                      ln1_b[l].reshape(1, -1), alpha=alpha)
        h = _mlp(h1, w1b, b_ff1[l].reshape(1, -1), w2b,
                 b_ff2[l].reshape(1, -1), ln2_g[l].reshape(1, -1), ln2_b[l].reshape(1, -1), alpha=alpha)
    return h.reshape(batch, seq, d)
```

```python
import functools
import math

import jax
import jax.numpy as jnp
from jax import lax
from jax.experimental import pallas as pl
from jax.experimental.pallas import tpu as pltpu

M_HEADS, M_DQK, M_DV = 4, 256, 512
R_HEADS, R_DQK, R_DV = 8, 128, 256
CONV_K = 4
CHUNK = 128
ROPE_BASE = 10000.0
LN_EPS = 1e-5
M_QK_W = M_HEADS * M_DQK
M_V_W = M_HEADS * M_DV
R_QK_W = R_HEADS * R_DQK
R_V_W = R_HEADS * R_DV

LANES = 128
SUBLANES = 8
GATE_PAD = LANES
MIB = 1024 * 1024

F32 = jnp.float32
BF16 = jnp.bfloat16

def _sigmoid(x):
    return 1.0 / (1.0 + jnp.exp(-x))


def _log_sigmoid(x):
    return jnp.minimum(x, 0.0) - jnp.log1p(jnp.exp(-jnp.abs(x)))


def _dot(a, b):
    return jnp.dot(a, b, preferred_element_type=F32)


def _dot_nt(a, b):
    return lax.dot_general(a, b, (((1,), (1,)), ((), ())), preferred_element_type=F32)


def _layer_norm(z, g, b):
    mu = jnp.mean(z, axis=-1, keepdims=True)
    d = z - mu
    var = jnp.mean(d * d, axis=-1, keepdims=True)
    return d * lax.rsqrt(var + LN_EPS) * g + b


def _head_norm(h, g):
    mu = jnp.mean(h, axis=-1, keepdims=True)
    d = h - mu
    var = jnp.mean(d * d, axis=-1, keepdims=True)
    return d * lax.rsqrt(var + LN_EPS) * g


def _col_tile(step, gate_blk, n_tiles):
    return lax.rem(step + gate_blk, n_tiles)


def _proj_kernel(gate_blk, n_gate, x_ref, w_hbm, wg_ref, bu_ref, bg_ref, u_ref, g_ref,
                 xb_scr, w_ring, sem):
    i, j = pl.program_id(0), pl.program_id(1)
    n_j = pl.num_programs(1)
    step = i * n_j + j
    n_steps = pl.num_programs(0) * n_j
    n_slots, bn, _ = w_ring.shape
    c = _col_tile(j, gate_blk, n_j)

    def weight_copy(s):
        cs = _col_tile(lax.rem(s, n_j), gate_blk, n_j)
        row0 = pl.multiple_of(cs * bn + jnp.where(cs >= gate_blk, n_gate, 0), SUBLANES)
        slot = lax.rem(s, n_slots)
        return pltpu.make_async_copy(w_hbm.at[pl.ds(row0, bn), :], w_ring.at[slot], sem.at[slot])

    @pl.when(step == 0)
    def _():
        for s in range(n_slots - 1):
            weight_copy(jnp.int32(s)).start()

    @pl.when(step + (n_slots - 1) < n_steps)
    def _():
        weight_copy(step + (n_slots - 1)).start()

    @pl.when(j == 0)
    def _():
        xb_scr[...] = x_ref[...].astype(BF16)

    weight_copy(step).wait()
    w = w_ring[lax.rem(step, n_slots)].astype(BF16)
    xb = xb_scr[...]
    u_ref[...] = (_dot_nt(xb, w) + bu_ref[pl.ds(c, 1), :]).astype(u_ref.dtype)

    @pl.when(j == 0)
    def _():
        g_ref[...] = _dot_nt(xb, wg_ref[...].astype(BF16)) + bg_ref[...]


def _proj(x2d, w_t, b, *, gate_lo, n_gate, bm=1024, bn=1024, n_slots=3):
    m, d = x2d.shape
    n = w_t.shape[0] - n_gate
    assert gate_lo % bn == 0 and n % bn == 0 and m % bm == 0
    assert n_gate % SUBLANES == 0 and n_gate <= LANES and gate_lo % GATE_PAD == 0
    gate_blk = gate_lo // bn
    n_j = n // bn
    assert (m // bm) * n_j >= n_slots - 1
    col = lambda j: _col_tile(j, gate_blk, n_j)
    b_u = jnp.concatenate([b[:, :gate_lo], b[:, gate_lo + n_gate:]], axis=1).reshape(n_j, bn)
    fixed = lambda shape, blk: pl.BlockSpec(shape, lambda i, j: blk, pipeline_mode=pl.Buffered(1))
    return pl.pallas_call(
        functools.partial(_proj_kernel, gate_blk, n_gate),
        out_shape=(jax.ShapeDtypeStruct((m, n), BF16), jax.ShapeDtypeStruct((m, GATE_PAD), F32)),
        grid=(m // bm, n_j),
        in_specs=[
            pl.BlockSpec((bm, d), lambda i, j: (i, 0)),
            pl.BlockSpec(memory_space=pl.ANY),
            fixed((GATE_PAD, d), (gate_lo // GATE_PAD, 0)),
            fixed((n_j, bn), (0, 0)),
            fixed((1, GATE_PAD), (0, gate_lo // GATE_PAD)),
        ],
        out_specs=(
            pl.BlockSpec((bm, bn), lambda i, j: (i, col(j))),
            pl.BlockSpec((bm, GATE_PAD), lambda i, j: (i, 0)),
        ),
        scratch_shapes=[
            pltpu.VMEM((bm, d), BF16),
            pltpu.VMEM((n_slots, bn, d), F32),
            pltpu.SemaphoreType.DMA((n_slots,)),
        ],
        compiler_params=pltpu.CompilerParams(
            dimension_semantics=("arbitrary", "arbitrary"), vmem_limit_bytes=56 * MIB),
        name="proj",
    )(x2d, w_t, w_t, b_u, b)


def _conv_shift_matrix():
    L = CHUNK
    row = lax.broadcasted_iota(jnp.int32, (L, 2 * L), 0)
    col = lax.broadcasted_iota(jnp.int32, (L, 2 * L), 1)
    blocks = [jnp.where(col == row + (L - (CONV_K - 1) + j), 1.0, 0.0) for j in range(CONV_K)]
    return jnp.concatenate(blocks, axis=0).astype(BF16)


def _mlstm_chunk(q_ref, k_ref, v_ref, g_ref, cw_ref, cb_ref, out_ref, c_ref, n_ref, m_ref,
                 x2_ref, shift_ref):
    L = CHUNK

    x2_ref[L:2 * L, 0:M_QK_W] = q_ref[...]
    x2_ref[L:2 * L, M_QK_W:2 * M_QK_W] = k_ref[...]
    taps = _dot(shift_ref[...], x2_ref[...])
    conv = cb_ref[...]
    for j in range(CONV_K):
        conv = conv + cw_ref[j:j + 1, :] * taps[j * L:(j + 1) * L, :]
    x2_ref[0:L, :] = x2_ref[L:2 * L, :]
    qk = conv * _sigmoid(conv)

    g = g_ref[...]
    lfg = _log_sigmoid(g)
    g_rows = g.T[0:2 * M_HEADS, :]
    lf_rows = _log_sigmoid(g_rows)

    row_t = lax.broadcasted_iota(jnp.int32, (L, L), 0)
    col_s = lax.broadcasted_iota(jnp.int32, (L, L), 1)
    causal = col_s <= row_t
    lane = lax.broadcasted_iota(jnp.int32, (L, GATE_PAD), 1)

    heads = range(M_HEADS)
    qh = [qk[:, h * M_DQK:(h + 1) * M_DQK] * (M_DQK ** -0.5) for h in heads]
    kh = [qk[:, M_QK_W + h * M_DQK:M_QK_W + (h + 1) * M_DQK] for h in heads]
    qb = [q.astype(BF16) for q in qh]
    vb = [v_ref[:, h * M_DV:(h + 1) * M_DV] for h in heads]
    scores = [_dot_nt(qb[h], kh[h].astype(BF16)) for h in heads]
    inter = [_dot(qb[h], c_ref[h].astype(BF16)) for h in heads]
    yield

    gate = []
    for h in heads:
        li_row = g_rows[h:h + 1, :]
        lf_row = lf_rows[M_HEADS + h:M_HEADS + h + 1, :]
        li_col = jnp.sum(jnp.where(lane == h, g, 0.0), axis=1, keepdims=True)
        lf_col = jnp.sum(jnp.where(lane == M_HEADS + h, lfg, 0.0), axis=1, keepdims=True)
        b_col = jnp.sum(jnp.where(causal, lf_row, 0.0), axis=1, keepdims=True)
        b_row = jnp.sum(jnp.where(row_t <= col_s, lf_col, 0.0), axis=0, keepdims=True)
        g_tot = jnp.sum(lf_row, axis=1, keepdims=True)
        m_prev = m_ref[h:h + 1, 0:1]

        d_log = jnp.where(causal, b_col - b_row + li_row, -jnp.inf)
        m_inter = b_col + m_prev
        m_t = jnp.maximum(m_inter, jnp.max(d_log, axis=1, keepdims=True))
        w_intra = jnp.exp(d_log - m_t)
        w_inter = jnp.exp(m_inter - m_t)

        w_log_row = g_tot - b_row + li_row
        w_log_col = g_tot - b_col + li_col
        m_new = jnp.maximum(g_tot + m_prev, jnp.max(w_log_row, axis=1, keepdims=True))
        w_s = jnp.exp(w_log_col - m_new)
        decay = jnp.exp(g_tot + m_prev - m_new)
        gate.append((m_t, w_intra, w_inter, m_new, w_s, decay))
    yield

    for h in heads:
        m_t, w_intra, w_inter, m_new, w_s, decay = gate[h]
        n_old = n_ref[h:h + 1, :]
        s = scores[h] * w_intra
        num = _dot(s.astype(BF16), vb[h]) + w_inter * inter[h]
        den = (jnp.sum(s, axis=1, keepdims=True)
               + w_inter * jnp.sum(qh[h] * n_old, axis=1, keepdims=True))
        hh = num * (1.0 / jnp.maximum(jnp.abs(den), jnp.exp(-m_t)))

        kw = kh[h] * w_s
        c_ref[h] = decay * c_ref[h] + _dot(kw.T.astype(BF16), vb[h])
        n_ref[h:h + 1, :] = decay * n_old + jnp.sum(kw, axis=0, keepdims=True)
        m_ref[h:h + 1, :] = jnp.broadcast_to(m_new, (1, LANES))

        out_ref[:, h * M_DV:(h + 1) * M_DV] = hh.astype(out_ref.dtype)


def _retention_constants(decay_ref, xi_ref, zeta_ref):
    L = CHUNK
    t_col = lax.broadcasted_iota(jnp.int32, (L, 1), 0).astype(F32)
    rel = (lax.broadcasted_iota(jnp.int32, (L, L), 0)
           - lax.broadcasted_iota(jnp.int32, (L, L), 1)).astype(F32)
    scale = R_DQK ** -0.5
    for h in range(R_HEADS):
        log_gamma = math.log1p(-(2.0 ** (-5.0 - h)))
        decay_ref[h] = jnp.where(rel >= 0.0, jnp.exp(jnp.maximum(rel, 0.0) * log_gamma), 0.0) * scale
        xi_ref[h] = jnp.exp((t_col + 1.0) * log_gamma) * scale
        zeta_ref[h] = jnp.exp((L - 1.0 - t_col) * log_gamma)


def _retention_chunk(q_ref, k_ref, v_ref, pos_ref, invf_ref, sign_ref, out_ref, r_ref,
                     decay_ref, xi_ref, zeta_ref):
    L = CHUNK
    half = R_DQK // 2
    assert L == R_DQK
    lane = lax.broadcasted_iota(jnp.int32, (L // 2, R_DQK), 1)
    pos_rows = jnp.broadcast_to(pos_ref[0].astype(F32), (L, L)).T
    ang = jnp.where(lane < half, pos_rows[0:L // 2, :], pos_rows[L // 2:L, :]) * invf_ref[...]

    def unpack(packed):
        swapped = pltpu.roll(packed, half, 1)
        top = jnp.where(lane < half, packed, swapped)
        bottom = jnp.where(lane < half, swapped, packed)
        return jnp.concatenate([top, bottom], axis=0)

    cos = unpack(jnp.cos(ang))
    sin = unpack(jnp.sin(ang)) * sign_ref[...]

    heads = range(R_HEADS)
    qr, kr = [], []
    for h in heads:
        qs = slice(h * R_DQK, (h + 1) * R_DQK)
        xq = q_ref[:, qs].astype(F32)
        xk = k_ref[:, qs].astype(F32)
        qr.append((xq * cos + pltpu.roll(xq, half, 1) * sin).astype(BF16))
        kr.append(xk * cos + pltpu.roll(xk, half, 1) * sin)
    vb = [v_ref[:, h * R_DV:(h + 1) * R_DV] for h in heads]
    scores = [_dot_nt(qr[h], kr[h].astype(BF16)) for h in heads]
    cross = [_dot(qr[h], r_ref[h].astype(BF16)) for h in heads]
    yield

    for h in heads:
        g_chunk = math.exp(L * math.log1p(-(2.0 ** (-5.0 - h))))
        p = scores[h] * decay_ref[h]
        o = _dot(p.astype(BF16), vb[h]) + cross[h] * xi_ref[h]
        r_ref[h] = g_chunk * r_ref[h] + _dot((kr[h] * zeta_ref[h]).T.astype(BF16), vb[h])
        out_ref[:, h * R_DV:(h + 1) * R_DV] = o.astype(out_ref.dtype)


def _rec_kernel(batch, n_cast, mq_ref, mk_ref, mv_ref, g_ref, cw_ref, cb_ref,
                rq_ref, rk_ref, rv_ref, pos_ref, invf_ref, sign_ref, *refs):
    cast_in, refs = refs[:n_cast], refs[n_cast:]
    hm_ref, hr_ref = refs[:2]
    cast_out, refs = refs[2:2 + n_cast], refs[2 + n_cast:]
    c_ref, n_ref, m_ref, x2_ref, shift_ref, r_ref, decay_ref, xi_ref, zeta_ref = refs

    @pl.when(pl.program_id(0) == 0)
    def _():
        shift_ref[...] = _conv_shift_matrix()
        _retention_constants(decay_ref, xi_ref, zeta_ref)
        c_ref[...] = jnp.zeros_like(c_ref)
        n_ref[...] = jnp.zeros_like(n_ref)
        m_ref[...] = jnp.zeros_like(m_ref)
        r_ref[...] = jnp.zeros_like(r_ref)
        x2_ref[:, 0:CHUNK, :] = jnp.zeros((batch, CHUNK, 2 * M_QK_W), BF16)

    mlstm = [_mlstm_chunk(mq_ref.at[b], mk_ref.at[b], mv_ref.at[b], g_ref.at[b], cw_ref, cb_ref,
                          hm_ref.at[b], c_ref.at[b], n_ref.at[b], m_ref.at[b], x2_ref.at[b], shift_ref)
             for b in range(batch)]
    retention = [_retention_chunk(rq_ref.at[b], rk_ref.at[b], rv_ref.at[b], pos_ref.at[b], invf_ref,
                                  sign_ref, hr_ref.at[b], r_ref.at[b], decay_ref, xi_ref, zeta_ref)
                 for b in range(batch)]
    for gens in (mlstm, retention):
        live = list(gens)
        while live:
            live = [g for g in live if next(g, StopIteration) is not StopIteration]

    for src, dst in zip(cast_in, cast_out):
        dst[...] = src[...].astype(BF16)


def _rec(u, gates, positions, conv_w, conv_b, inv_freq, sign, *, batch, n_chunks, to_bf16=()):
    L = CHUNK
    seq = n_chunks * L
    assert all(w.ndim == 2 and w.shape[0] % (n_chunks * 2 * SUBLANES) == 0 for w in to_bf16)
    slab = lambda w: pl.BlockSpec((w.shape[0] // n_chunks, w.shape[1]), lambda c: (c, 0))
    r_q = (2 * M_QK_W + 2 * M_V_W) // R_QK_W
    r_v = (2 * M_QK_W + 2 * M_V_W + 2 * R_QK_W) // R_V_W
    const = lambda shape: pl.BlockSpec(shape, lambda c: (0, 0))
    rows = lambda width, col: pl.BlockSpec((batch, L, width), lambda c: (0, c, col))
    u3 = u.reshape(batch, seq, u.shape[1])
    hm, hr, *casts = pl.pallas_call(
        functools.partial(_rec_kernel, batch, len(to_bf16)),
        out_shape=(jax.ShapeDtypeStruct((batch, seq, M_V_W), BF16),
                   jax.ShapeDtypeStruct((batch, seq, R_V_W), BF16),
                   *[jax.ShapeDtypeStruct(w.shape, BF16) for w in to_bf16]),
        grid=(n_chunks,),
        in_specs=[
            rows(M_QK_W, 0),
            rows(M_QK_W, 1),
            rows(M_V_W, 1),
            rows(GATE_PAD, 0),
            const((CONV_K, 2 * M_QK_W)),
            const((1, 2 * M_QK_W)),
            rows(R_QK_W, r_q),
            rows(R_QK_W, r_q + 1),
            rows(R_V_W, r_v),
            pl.BlockSpec((batch, 1, 1, L), lambda c: (0, c, 0, 0)),
            const((1, R_DQK)),
            const((1, R_DQK)),
            *[slab(w) for w in to_bf16],
        ],
        out_specs=(rows(M_V_W, 0), rows(R_V_W, 0), *[slab(w) for w in to_bf16]),
        scratch_shapes=[
            pltpu.VMEM((batch, M_HEADS, M_DQK, M_DV), F32),
            pltpu.VMEM((batch, SUBLANES, M_DQK), F32),
            pltpu.VMEM((batch, SUBLANES, LANES), F32),
            pltpu.VMEM((batch, 2 * L, 2 * M_QK_W), BF16),
            pltpu.VMEM((CONV_K * L, 2 * L), BF16),
            pltpu.VMEM((batch, R_HEADS, R_DQK, R_DV), F32),
            pltpu.VMEM((R_HEADS, L, L), F32),
            pltpu.VMEM((R_HEADS, L, 1), F32),
            pltpu.VMEM((R_HEADS, L, 1), F32),
        ],
        compiler_params=pltpu.CompilerParams(
            dimension_semantics=("arbitrary",), vmem_limit_bytes=46 * MIB),
        name="rec",
    )(u3, u3, u3, gates.reshape(batch, seq, GATE_PAD), conv_w, conv_b, u3, u3, u3,
      positions.reshape(batch, n_chunks, 1, L), inv_freq, sign, *to_bf16)
    return (hm.reshape(batch * seq, M_V_W), hr.reshape(batch * seq, R_V_W), *casts)


def _resident(shape):
    return pl.BlockSpec(shape, lambda i: (0,) * len(shape), pipeline_mode=pl.Buffered(1))


def _branch_kernel(hh_ref, ho_ref, mo_ref, rg_ref, ga_ref, gb_ref, mng_ref, rng_ref, wm_ref, wr_ref,
                   out_ref, hm_scr, hr_scr):
    def m_heads(h0, h1):
        for h in range(h0, h1):
            sl = slice(h * M_DV, (h + 1) * M_DV)
            y = _head_norm(hh_ref[:, sl].astype(F32), mng_ref[:, sl]) * _sigmoid(mo_ref[:, sl].astype(F32))
            hm_scr[:, sl] = y.astype(BF16)
        rows = slice(h0 * M_DV, h1 * M_DV)
        return _dot(hm_scr[:, rows], wm_ref[rows, :])

    def r_heads(h0, h1):
        for h in range(h0, h1):
            sl = slice(h * R_DV, (h + 1) * R_DV)
            sg = rg_ref[:, sl].astype(F32)
            y = _head_norm(ho_ref[:, sl].astype(F32), rng_ref[:, sl]) * (sg * _sigmoid(sg))
            hr_scr[:, sl] = y.astype(BF16)
        rows = slice(h0 * R_DV, h1 * R_DV)
        return _dot(hr_scr[:, rows], wr_ref[rows, :])

    ya = m_heads(0, M_HEADS // 2) + m_heads(M_HEADS // 2, M_HEADS)
    yb = r_heads(0, R_HEADS // 2) + r_heads(R_HEADS // 2, R_HEADS)
    merged = _sigmoid(ga_ref[...].astype(F32)) * ya + _sigmoid(gb_ref[...].astype(F32)) * yb
    out_ref[...] = merged.astype(out_ref.dtype)


def _branch(hh, ho, u, m_norm_g, r_norm_g, wm, wr, *, bm=512):
    m, d = hh.shape[0], wm.shape[1]
    o_blk = (2 * M_QK_W + M_V_W) // M_V_W
    s_blk = (2 * M_QK_W + 2 * M_V_W + 2 * R_QK_W + R_V_W) // R_V_W
    g_a = (2 * M_QK_W + 2 * M_V_W + 2 * R_QK_W + 2 * R_V_W) // d
    assert wm.shape == (M_V_W, d) and wr.shape == (R_V_W, d) and M_V_W == R_V_W == d
    assert m % bm == 0
    return pl.pallas_call(
        _branch_kernel,
        out_shape=jax.ShapeDtypeStruct((m, d), BF16),
        grid=(m // bm,),
        in_specs=[
            pl.BlockSpec((bm, M_V_W), lambda i: (i, 0)),
            pl.BlockSpec((bm, R_V_W), lambda i: (i, 0)),
            pl.BlockSpec((bm, M_V_W), lambda i: (i, o_blk)),
            pl.BlockSpec((bm, R_V_W), lambda i: (i, s_blk)),
            pl.BlockSpec((bm, d), lambda i: (i, g_a)),
            pl.BlockSpec((bm, d), lambda i: (i, g_a + 1)),
            _resident((1, M_V_W)),
            _resident((1, R_V_W)),
            _resident((M_V_W, d)),
            _resident((R_V_W, d)),
        ],
        out_specs=pl.BlockSpec((bm, d), lambda i: (i, 0)),
        scratch_shapes=[
            pltpu.VMEM((bm, M_V_W), BF16),
            pltpu.VMEM((bm, R_V_W), BF16),
        ],
        compiler_params=pltpu.CompilerParams(
            dimension_semantics=("arbitrary",), vmem_limit_bytes=53 * MIB),
        name="branch",
    )(hh, ho, u, u, u, u, m_norm_g, r_norm_g, wm, wr)


def _outproj_kernel(alpha, mg_ref, x_ref, wo_ref, bo_ref, lg_ref, lb_ref, out_ref):
    bm = out_ref.shape[0]
    for r0 in range(0, bm, bm // 2):
        rs = slice(r0, r0 + bm // 2)
        y = _dot(mg_ref[rs, :], wo_ref[...]) + bo_ref[...]
        out_ref[rs, :] = _layer_norm(alpha * x_ref[rs, :] + y, lg_ref[...], lb_ref[...])


def _outproj(merged, x2d, wo, bo, ln_g, ln_b, *, alpha, bm=512):
    m, d = x2d.shape
    assert wo.shape == (d, d) and m % bm == 0
    return pl.pallas_call(
        functools.partial(_outproj_kernel, alpha),
        out_shape=jax.ShapeDtypeStruct((m, d), F32),
        grid=(m // bm,),
        in_specs=[
            pl.BlockSpec((bm, d), lambda i: (i, 0)),
            pl.BlockSpec((bm, d), lambda i: (i, 0)),
            _resident((d, d)),
            _resident((1, d)),
            _resident((1, d)),
            _resident((1, d)),
        ],
        out_specs=pl.BlockSpec((bm, d), lambda i: (i, 0)),
        compiler_params=pltpu.CompilerParams(
            dimension_semantics=("arbitrary",), vmem_limit_bytes=34 * MIB),
        name="outproj",
    )(merged, x2d, wo, bo, ln_g, ln_b)


def _mlp_kernel(alpha, x_ref, w1_ref, b1_ref, w2_ref, b2_ref, lg_ref, lb_ref, out_ref, xb_ref):
    j = pl.program_id(1)

    @pl.when(j == 0)
    def _():
        xb_ref[...] = x_ref[...].astype(BF16)
        out_ref[...] = jnp.zeros_like(out_ref)

    hid = jnp.maximum(_dot(xb_ref[...], w1_ref[...]) + b1_ref[pl.ds(j, 1), :], 0.0)
    out_ref[...] += _dot((hid * hid).astype(BF16), w2_ref[...])

    @pl.when(j == pl.num_programs(1) - 1)
    def _():
        z = alpha * x_ref[...] + (out_ref[...] + b2_ref[...])
        out_ref[...] = _layer_norm(z, lg_ref[...], lb_ref[...])


def _mlp(x2d, w1, b1, w2, b2, ln_g, ln_b, *, alpha, bm=512, bf=2048):
    m, d = x2d.shape
    f = w1.shape[1]
    return pl.pallas_call(
        functools.partial(_mlp_kernel, alpha),
        out_shape=jax.ShapeDtypeStruct((m, d), F32),
        grid=(m // bm, f // bf),
        in_specs=[
            pl.BlockSpec((bm, d), lambda i, j: (i, 0)),
            pl.BlockSpec((d, bf), lambda i, j: (0, j)),
            pl.BlockSpec((f // bf, bf), lambda i, j: (0, 0)),
            pl.BlockSpec((bf, d), lambda i, j: (j, 0)),
            pl.BlockSpec((1, d), lambda i, j: (0, 0)),
            pl.BlockSpec((1, d), lambda i, j: (0, 0)),
            pl.BlockSpec((1, d), lambda i, j: (0, 0)),
        ],
        out_specs=pl.BlockSpec((bm, d), lambda i, j: (i, 0)),
        scratch_shapes=[pltpu.VMEM((bm, d), BF16)],
        compiler_params=pltpu.CompilerParams(
            dimension_semantics=("arbitrary", "arbitrary"), vmem_limit_bytes=58 * MIB),
        name="mlp",
    )(x2d, w1, b1.reshape(f // bf, bf), w2, b2, ln_g, ln_b)


def kernel(x, positions, w_in, b_in, m_conv_w, m_conv_b, m_norm_g, r_norm_g, w_branch_m, w_branch_r,
           w_out, b_out, ln1_g, ln1_b, w_ff1, b_ff1, w_ff2, b_ff2, ln2_g, ln2_b):
    batch, seq, d = x.shape
    depth = w_in.shape[0]
    n_chunks = seq // CHUNK
    alpha = (2.0 * depth) ** 0.25
    gate_lo = 2 * M_QK_W + 2 * M_V_W

    half = R_DQK // 2
    inv_freq = ROPE_BASE ** (-jnp.arange(half, dtype=F32) / half)
    inv_freq = jnp.concatenate([inv_freq, inv_freq]).reshape(1, R_DQK)
    sign = jnp.concatenate([-jnp.ones((half,), F32), jnp.ones((half,), F32)]).reshape(1, R_DQK)
    h = x.reshape(batch * seq, d)
    for l in range(depth):
        u, gates = _proj(h, w_in[l].T, b_in[l].reshape(1, -1),
                         gate_lo=gate_lo, n_gate=2 * M_HEADS)
        hh, ho, wmb, wrb, wob, w1b, w2b = _rec(
            u, gates, positions, m_conv_w[l], m_conv_b[l].reshape(1, -1), inv_freq, sign,
            batch=batch, n_chunks=n_chunks,
            to_bf16=(w_branch_m[l], w_branch_r[l], w_out[l], w_ff1[l], w_ff2[l]))
        merged = _branch(hh, ho, u, m_norm_g[l].reshape(1, -1), r_norm_g[l].reshape(1, -1), wmb, wrb)
        h1 = _outproj(merged, h, wob, b_out[l].reshape(1, -1), ln1_g[l].reshape(1, -1),
                      ln1_b[l].reshape(1, -1), alpha=alpha)
        h = _mlp(h1, w1b, b_ff1[l].reshape(1, -1), w2b,
                 b_ff2[l].reshape(1, -1), ln2_g[l].reshape(1, -1), ln2_b[l].reshape(1, -1), alpha=alpha)
    return h.reshape(batch, seq, d)
```
